```python
import math
import jax, jax.numpy as jnp
from jax import lax
import numpy as np

D_MODEL = 2048
BATCH = 2
SEQ = 16384
DEPTH = 1
DEC_BATCH = 8
DEC_SEQ = 4096
PAST_LEN = 128

D_MIX = D_MODEL
D_A = D_MIX // 2
D_B = D_MIX - D_A
CHUNK = 128
A_HEAD_DIM = 128
H_A = D_A // A_HEAD_DIM
W_IN_COLS = 2 * D_A + 3 * D_B
SHORT_CONV = 3
EMB_DIM = 33
FILTER_WIDTH = 64
DECAY_TARGET = 1e-2
FAST_DECAY_PCT = 0.3
SLOW_DECAY_PCT = 1.5
N_KEYS = 128
N_EXPERTS = N_KEYS * N_KEYS
PEER_HEADS = 8
PEER_TOPK = 16
D_KEY = 256
PEER_CHUNK = 128
EPS = 1e-6

kernel_name = 'hybrid_gmlp_hyena_peer_encoder'


def rms_norm(x, g):
    xf = x.astype(jnp.float32)
    y = xf * lax.rsqrt(jnp.mean(xf * xf, axis=-1, keepdims=True) + EPS)
    return (y * g.astype(jnp.float32)).astype(x.dtype)


def layer_norm(x, g, b):
    xf = x.astype(jnp.float32)
    mu = jnp.mean(xf, axis=-1, keepdims=True)
    var = jnp.mean(jnp.square(xf - mu), axis=-1, keepdims=True)
    y = (xf - mu) * lax.rsqrt(var + EPS)
    return (y * g.astype(jnp.float32) + b.astype(jnp.float32)).astype(x.dtype)


def gmlp_mixer(u_raw, v_raw, ln_g, ln_b, ws, bs):
    B, L, _ = u_raw.shape
    u = jax.nn.gelu(u_raw, approximate=False)
    v = layer_norm(jax.nn.gelu(v_raw, approximate=False), ln_g, ln_b)
    v = v.reshape(B, L // CHUNK, CHUNK, H_A, A_HEAD_DIM)
    mixed = jnp.einsum('hpq,bcqhd->bcphd', ws, v) + bs.T[:, :, None]
    return u * mixed.reshape(B, L, D_A)


def short_conv(x, w, b):
    L = x.shape[1]
    pad = SHORT_CONV // 2
    xp = jnp.pad(x, ((0, 0), (pad, pad), (0, 0)))
    acc = b
    for j in range(SHORT_CONV):
        acc = acc + xp[:, j:j + L] * w[j]
    return acc


def hyena_filters(L, w1, b1, w2, b2, w3, b3, freq, w4):
    f32 = jnp.float32
    pos = jnp.arange(L, dtype=f32)
    t = pos / f32(max(L - 1, 1))
    bands = (EMB_DIM - 1) // 2
    fr = jnp.linspace(1e-4, bands - 1, bands, dtype=f32)
    fw = (2.0 * math.pi / L) * pos[:, None] * fr[None, :]
    z = jnp.concatenate([t[:, None], jnp.cos(fw), -jnp.sin(fw)], axis=-1)
    fq = freq.astype(f32)
    h = jnp.sin(fq[0] * (z @ w1.astype(f32) + b1.astype(f32)))
    h = jnp.sin(fq[1] * (h @ w2.astype(f32) + b2.astype(f32)))
    h = jnp.sin(fq[2] * (h @ w3.astype(f32) + b3.astype(f32)))
    h = h @ w4.astype(f32)
    min_decay = math.log(DECAY_TARGET) / SLOW_DECAY_PCT
    max_decay = math.log(DECAY_TARGET) / FAST_DECAY_PCT
    deltas = jnp.abs(jnp.linspace(min_decay, max_decay, D_B, dtype=f32))
    window = jnp.exp(-t[:, None] * deltas[None, :])
    h_fwd = h[:, :D_B] * window
    h_bwd = h[:, D_B:] * window
    return jnp.concatenate([h_fwd, jnp.zeros((1, D_B), f32), h_bwd[:0:-1]], axis=0)


def bidir_long_conv(x, k):
    L = x.shape[1]
    xf = jnp.fft.rfft(x.astype(jnp.float32), n=2 * L, axis=1)
    kf = jnp.fft.rfft(k, n=2 * L, axis=0)
    y = jnp.fft.irfft(xf * kf[None], n=2 * L, axis=1)[:, :L]
    return y.astype(x.dtype)


def hyena_mixer(zb, conv_w, conv_b, w1, b1, w2, b2, w3, b3, freq, w4, bias):
    L = zb.shape[1]
    zc = short_conv(zb, conv_w, conv_b)
    x0 = zc[..., :D_B]
    x1 = zc[..., D_B:2 * D_B]
    v = zc[..., 2 * D_B:]
    g = x1 * v
    k = hyena_filters(L, w1, b1, w2, b2, w3, b3, freq, w4)
    return x0 * (bidir_long_conv(g, k) + g * bias)


def peer_ffn(x, wq, k1, k2, down, up):
    B, L, D = x.shape
    xc = x.reshape(-1, PEER_CHUNK, D)
    half = D_KEY // 2

    def block(xt):
        C = xt.shape[0]
        q = (xt @ wq).reshape(C, PEER_HEADS, 2, half)
        s1 = jnp.einsum('chd,hnd->chn', q[:, :, 0], k1).astype(jnp.float32)
        s2 = jnp.einsum('chd,hnd->chn', q[:, :, 1], k2).astype(jnp.float32)
        v1, i1 = lax.top_k(s1, PEER_TOPK)
        v2, i2 = lax.top_k(s2, PEER_TOPK)
        cand = (v1[..., :, None] + v2[..., None, :]).reshape(C, PEER_HEADS, PEER_TOPK * PEER_TOPK)
        cidx = (i1[..., :, None] * N_KEYS + i2[..., None, :]).reshape(C, PEER_HEADS, PEER_TOPK * PEER_TOPK)
        best, sel = lax.top_k(cand, PEER_TOPK)
        idx = jnp.take_along_axis(cidx, sel, axis=-1)
        gate = jax.nn.softmax(best, axis=-1)
        u = down[idx]
        a = jax.nn.gelu(jnp.einsum('chkd,cd->chk', u, xt).astype(jnp.float32), approximate=False)
        return jnp.einsum('chk,chkd->cd', (gate * a).astype(xt.dtype), up[idx])

    return lax.map(block, xc).reshape(B, L, D)


def encoder_layer(x, norm1, w_in, a_ln_g, a_ln_b, a_ws, a_bs, b_conv_w, b_conv_b,
                  hf_w1, hf_b1, hf_w2, hf_b2, hf_w3, hf_b3, hf_freq, hf_w4, b_bias,
                  mix_norm, w_out, norm2, peer_wq, peer_k1, peer_k2, peer_down, peer_up):
    h = rms_norm(x, norm1)
    z = h @ w_in
    ya = gmlp_mixer(z[..., :D_A], z[..., D_A:2 * D_A], a_ln_g, a_ln_b, a_ws, a_bs)
    yb = hyena_mixer(z[..., 2 * D_A:], b_conv_w, b_conv_b, hf_w1, hf_b1, hf_w2, hf_b2,
                     hf_w3, hf_b3, hf_freq, hf_w4, b_bias)
    y = jnp.concatenate([rms_norm(ya, mix_norm[:D_A]), rms_norm(yb, mix_norm[D_A:])], axis=-1)
    x = x + y @ w_out
    x = x + peer_ffn(rms_norm(x, norm2), peer_wq, peer_k1, peer_k2, peer_down, peer_up)
    return x


def trunk(x, layer_params, final_norm):
    for l in range(DEPTH):
        x = encoder_layer(x, *[p[l] for p in layer_params])
    return rms_norm(x, final_norm)


def setup_inputs(seed: int = 0) -> dict:
    key = jax.random.key(seed)
    ks = jax.random.split(key, 32)

    def nrm(k, shape, s):
        return jax.random.normal(k, shape, jnp.float32) * s

    D = D_MODEL
    return {
        'x_prompt': nrm(ks[0], (BATCH, SEQ, D), 1.0),
        'x_sample': nrm(ks[1], (DEC_BATCH, DEC_SEQ, D), 1.0),
        'norm1': 1.0 + nrm(ks[2], (DEPTH, D), 0.01),
        'w_in': nrm(ks[3], (DEPTH, D, W_IN_COLS), D ** -0.5),
        'a_ln_g': 1.0 + nrm(ks[4], (DEPTH, D_A), 0.01),
        'a_ln_b': nrm(ks[5], (DEPTH, D_A), 0.01),
        'a_ws': nrm(ks[6], (DEPTH, H_A, CHUNK, CHUNK), CHUNK ** -0.5),
        'a_bs': 1.0 + nrm(ks[7], (DEPTH, H_A, CHUNK), 0.1),
        'b_conv_w': nrm(ks[8], (DEPTH, SHORT_CONV, 3 * D_B), SHORT_CONV ** -0.5),
        'b_conv_b': nrm(ks[9], (DEPTH, 3 * D_B), 0.01),
        'hf_w1': nrm(ks[10], (DEPTH, EMB_DIM, FILTER_WIDTH), EMB_DIM ** -0.5),
        'hf_b1': nrm(ks[11], (DEPTH, FILTER_WIDTH), 0.1),
        'hf_w2': nrm(ks[12], (DEPTH, FILTER_WIDTH, FILTER_WIDTH), FILTER_WIDTH ** -0.5),
        'hf_b2': nrm(ks[13], (DEPTH, FILTER_WIDTH), 0.1),
        'hf_w3': nrm(ks[14], (DEPTH, FILTER_WIDTH, FILTER_WIDTH), FILTER_WIDTH ** -0.5),
        'hf_b3': nrm(ks[15], (DEPTH, FILTER_WIDTH), 0.1),
        'hf_freq': 1.0 + nrm(ks[16], (DEPTH, 3, FILTER_WIDTH), 0.01),
        'hf_w4': nrm(ks[17], (DEPTH, FILTER_WIDTH, 2 * D_B), 0.02 * FILTER_WIDTH ** -0.5),
        'b_bias': nrm(ks[18], (DEPTH, D_B), 1.0),
        'mix_norm': 1.0 + nrm(ks[19], (DEPTH, D_MIX), 0.01),
        'w_out': nrm(ks[20], (DEPTH, D_MIX, D), D_MIX ** -0.5),
        'norm2': 1.0 + nrm(ks[21], (DEPTH, D), 0.01),
        'peer_wq': nrm(ks[22], (DEPTH, D, PEER_HEADS * D_KEY), D ** -0.5),
        'peer_k1': nrm(ks[23], (DEPTH, PEER_HEADS, N_KEYS, D_KEY // 2), (D_KEY // 2) ** -0.5),
        'peer_k2': nrm(ks[24], (DEPTH, PEER_HEADS, N_KEYS, D_KEY // 2), (D_KEY // 2) ** -0.5),
        'peer_down': nrm(ks[25], (DEPTH, N_EXPERTS, D), D ** -0.5),
        'peer_up': nrm(ks[26], (DEPTH, N_EXPERTS, D), 0.5),
        'final_norm': 1.0 + nrm(ks[27], (D,), 0.01),
    }


def reference(x_prompt, x_sample, norm1, w_in, a_ln_g, a_ln_b, a_ws, a_bs, b_conv_w, b_conv_b,
              hf_w1, hf_b1, hf_w2, hf_b2, hf_w3, hf_b3, hf_freq, hf_w4, b_bias,
              mix_norm, w_out, norm2, peer_wq, peer_k1, peer_k2, peer_down, peer_up, final_norm):
    layer_params = (norm1, w_in, a_ln_g, a_ln_b, a_ws, a_bs, b_conv_w, b_conv_b,
                    hf_w1, hf_b1, hf_w2, hf_b2, hf_w3, hf_b3, hf_freq, hf_w4, b_bias,
                    mix_norm, w_out, norm2, peer_wq, peer_k1, peer_k2, peer_down, peer_up)
    y_prompt = trunk(x_prompt, layer_params, final_norm)
    y_sample = trunk(x_sample, layer_params, final_norm)
    return (y_prompt, y_sample)
```

```python
import functools
import math

import numpy as np
import jax
import jax.numpy as jnp
from jax import lax
from jax.experimental import pallas as pl
from jax.experimental.pallas import tpu as pltpu

F32 = jnp.float32
BF16 = jnp.bfloat16
I32 = jnp.int32

EPS = 1e-6
V7X_LANES = 128
V7X_SUBLANES = 8
V7X_VMEM_BYTES = 64 * 1024 * 1024
VMEM_LIMIT = V7X_VMEM_BYTES - 8 * 1024 * 1024

CHUNK = 128
A_HEAD_DIM = 128
EMB_DIM = 33
BANDS = (EMB_DIM - 1) // 2
DECAY_TARGET = 1e-2
FAST_DECAY_PCT = 0.3
SLOW_DECAY_PCT = 1.5
FFT_N1 = 64
FFT_N2 = 128
FFT_N = FFT_N1 * FFT_N2
CONV_BLOCK = FFT_N // 2
N_KEYS = 128
KEY_SHIFT = 7
PEER_TOPK = 16
HALF_KEY = 128


def _cparams(sem):
    return pltpu.CompilerParams(dimension_semantics=sem, vmem_limit_bytes=VMEM_LIMIT)


def _rms(xf, g):
    return xf * lax.rsqrt(jnp.mean(xf * xf, axis=-1, keepdims=True) + EPS) * g


def _gelu(x):
    return 0.5 * x * (1.0 + lax.erf(x * np.float32(math.sqrt(0.5))))


def _inproj_kernel(xp_ref, x_ref, xn_ref, n1_ref, wu_ref, wv_ref, wb_ref, lng_ref, lnb_ref,
                   ws_ref, bsb_ref, cw_ref, cb_ref, mga_ref,
                   yan_ref, x0_ref, g_ref, ya_s, zs_s, *, tiles_per_seq, cblk):
    tm = x_ref.shape[0]
    d_a = wu_ref.shape[1]
    d_b = wb_ref.shape[2]
    i = pl.program_id(0)
    not_first = (i % tiles_per_seq != 0).astype(F32)
    not_last = (i % tiles_per_seq != tiles_per_seq - 1).astype(F32)

    xcat = jnp.concatenate([xp_ref[...], x_ref[...], xn_ref[...]], axis=0)
    hcat = _rms(xcat, n1_ref[...])
    h_all = hcat.astype(BF16)
    h = hcat[V7X_SUBLANES:V7X_SUBLANES + tm].astype(BF16)

    u = _gelu(jnp.dot(h, wu_ref[...], preferred_element_type=F32))
    v = _gelu(jnp.dot(h, wv_ref[...], preferred_element_type=F32))
    mu = jnp.mean(v, axis=-1, keepdims=True)
    vc = v - mu
    var = jnp.mean(vc * vc, axis=-1, keepdims=True)
    vb = (vc * lax.rsqrt(var + EPS) * lng_ref[...] + lnb_ref[...]).astype(BF16)
    for c in range(tm // CHUNK):
        rows = slice(c * CHUNK, (c + 1) * CHUNK)
        for hd in range(d_a // A_HEAD_DIM):
            cols = slice(hd * A_HEAD_DIM, (hd + 1) * A_HEAD_DIM)
            mixed = jnp.dot(ws_ref[hd], vb[rows, cols], preferred_element_type=F32) + bsb_ref[hd]
            ya_s[rows, cols] = u[rows, cols] * mixed
    yan_ref[...] = _rms(ya_s[...], mga_ref[...]).astype(yan_ref.dtype)

    halo = V7X_SUBLANES
    for cb in range(d_b // cblk):
        cols = slice(cb * cblk, (cb + 1) * cblk)
        parts = []
        for p in range(3):
            z = jnp.dot(h_all, wb_ref[p, :, cols], preferred_element_type=F32)
            zs_s[...] = z
            zs_s[0:halo, :] = z[0:halo] * not_first
            zs_s[tm + halo:tm + 2 * halo, :] = z[tm + halo:tm + 2 * halo] * not_last
            w = cw_ref[p]
            zc = (cb_ref[p:p + 1, cols]
                  + zs_s[halo - 1:halo - 1 + tm, :] * w[0:1, cols]
                  + zs_s[halo:halo + tm, :] * w[1:2, cols]
                  + zs_s[halo + 1:halo + 1 + tm, :] * w[2:3, cols])
            parts.append(zc)
        x0_ref[:, cols] = parts[0]
        g_ref[:, cols] = parts[1] * parts[2]


def _inproj(x, seq_len, norm1, w_u, w_v, w_b, ln_g, ln_b, ws, bsb, conv_w, conv_b, mix_g_a, tm=512, cblk=256):
    t, d = x.shape
    d_a = w_u.shape[1]
    d_b = w_b.shape[2]
    nblk = tm // V7X_SUBLANES
    last8 = t // V7X_SUBLANES - 1
    const = lambda *shape: pl.BlockSpec(shape, lambda i: (0,) * len(shape), pipeline_mode=pl.Buffered(1))
    kern = functools.partial(_inproj_kernel, tiles_per_seq=seq_len // tm, cblk=cblk)
    return pl.pallas_call(
        kern,
        grid=(t // tm,),
        in_specs=[
            pl.BlockSpec((V7X_SUBLANES, d), lambda i: (jnp.maximum(i * nblk - 1, 0), 0)),
            pl.BlockSpec((tm, d), lambda i: (i, 0)),
            pl.BlockSpec((V7X_SUBLANES, d), lambda i: (jnp.minimum((i + 1) * nblk, last8), 0)),
            const(1, d), const(d, d_a), const(d, d_a), const(3, d, d_b), const(1, d_a), const(1, d_a),
            const(*ws.shape), const(*bsb.shape), const(3, 3, d_b), const(3, d_b), const(1, d_a),
        ],
        out_specs=[
            pl.BlockSpec((tm, d_a), lambda i: (i, 0)),
            pl.BlockSpec((tm, d_b), lambda i: (i, 0)),
            pl.BlockSpec((tm, d_b), lambda i: (i, 0)),
        ],
        out_shape=[
            jax.ShapeDtypeStruct((t, d_a), BF16),
            jax.ShapeDtypeStruct((t, d_b), F32),
            jax.ShapeDtypeStruct((t, d_b), F32),
        ],
        scratch_shapes=[pltpu.VMEM((tm, d_a), F32), pltpu.VMEM((tm + 2 * V7X_SUBLANES, cblk), F32)],
        compiler_params=_cparams(("parallel",)),
    )(x, x, x, norm1, w_u, w_v, w_b, ln_g, ln_b, ws, bsb, conv_w, conv_b, mix_g_a)


def _filt_kernel(fr_ref, dl_ref, w1_ref, b1_ref, w2_ref, b2_ref, w3_ref, b3_ref, fq_ref, w4_ref, o_ref,
                 *, seq_len, nblocks, tr):
    di = pl.program_id(0)
    s = pl.program_id(1)
    r = pl.program_id(2)
    e = di - (nblocks - 1) - s
    trow = r * tr + lax.broadcasted_iota(I32, (tr, 1), 0)
    lag = e * CONV_BLOCK + trow
    pos = jnp.abs(lag).astype(F32)
    valid = jnp.logical_and(jnp.abs(lag) <= seq_len - 1, jnp.logical_or(s == 0, trow != 0))
    t = pos / np.float32(max(seq_len - 1, 1))
    fw = np.float32(2.0 * math.pi / seq_len) * pos * fr_ref[...]
    hi = lax.Precision.HIGHEST
    w1 = w1_ref[...]
    z1 = (t * w1[0:1, :]
          + jnp.dot(jnp.cos(fw), w1[1:1 + BANDS, :], precision=hi, preferred_element_type=F32)
          + jnp.dot(-jnp.sin(fw), w1[1 + BANDS:, :], precision=hi, preferred_element_type=F32))
    fq = fq_ref[...]
    h = jnp.sin(fq[0:1, :] * (z1 + b1_ref[...]))
    h = jnp.sin(fq[1:2, :] * (jnp.dot(h, w2_ref[...], precision=hi, preferred_element_type=F32) + b2_ref[...]))
    h = jnp.sin(fq[2:3, :] * (jnp.dot(h, w3_ref[...], precision=hi, preferred_element_type=F32) + b3_ref[...]))
    h4 = jnp.dot(h, w4_ref[...], precision=hi, preferred_element_type=F32)
    window = jnp.exp(-t * dl_ref[...])
    o_ref[...] = jnp.where(valid, h4 * window, 0.0)


def _hyena_filter_blocks(seq_len, w1, b1, w2, b2, w3, b3, freq, w4, tr=512):
    nblocks = seq_len // CONV_BLOCK
    nd = 2 * nblocks - 1
    width = w1.shape[1]
    d_b = w4.shape[1] // 2
    fr = jnp.asarray(np.linspace(1e-4, BANDS - 1, BANDS, dtype=np.float32)[None, :])
    min_decay = math.log(DECAY_TARGET) / SLOW_DECAY_PCT
    max_decay = math.log(DECAY_TARGET) / FAST_DECAY_PCT
    deltas = jnp.asarray(np.abs(np.linspace(min_decay, max_decay, d_b, dtype=np.float32))[None, :])
    w4h = w4.reshape(width, 2, d_b).transpose(1, 0, 2)
    const = lambda *shape: pl.BlockSpec(shape, lambda a, b, c: (0,) * len(shape))
    kern = functools.partial(_filt_kernel, seq_len=seq_len, nblocks=nblocks, tr=tr)
    half_of = lambda di, s, r: jnp.where(di - (nblocks - 1) - s >= 0, 0, 1)
    return pl.pallas_call(
        kern,
        grid=(nd, 2, CONV_BLOCK // tr),
        in_specs=[
            const(1, BANDS), const(1, d_b), const(EMB_DIM, width), const(1, width), const(width, width),
            const(1, width), const(width, width), const(1, width), const(3, width),
            pl.BlockSpec((None, width, d_b), lambda di, s, r: (half_of(di, s, r), 0, 0)),
        ],
        out_specs=pl.BlockSpec((None, None, tr, d_b), lambda di, s, r: (di, s, r, 0)),
        out_shape=jax.ShapeDtypeStruct((nd, 2, CONV_BLOCK, d_b), F32),
        compiler_params=_cparams(("parallel", "parallel", "parallel")),
    )(fr, deltas, w1, b1, w2, b2, w3, b3, freq, w4h)


def _dft_tables():
    n1 = np.arange(FFT_N1)
    f64 = np.exp(-2j * np.pi * np.outer(n1, n1) / FFT_N1)
    half = FFT_N1 // 2
    fr, fi = f64.real, f64.imag
    lhs_data = np.block([[fr[:, :half], -fi[:, :half]], [fi[:, :half], fr[:, :half]]])
    lhs_real = np.concatenate([fr, fi], axis=0)
    gr, gi = fr[:half, :] / FFT_N, -fi[:half, :] / FFT_N
    lhs_inv = np.block([[gr, -gi], [gi, gr]])
    k1 = np.arange(FFT_N1)[:, None, None]
    k2 = np.arange(FFT_N2)[None, :, None]
    n2 = np.arange(FFT_N2)[None, None, :]
    g = np.exp(-2j * np.pi * (n2 * (k1 + FFT_N1 * k2) % FFT_N) / FFT_N)
    gfwd = np.concatenate([np.concatenate([g.real, -g.imag], axis=2),
                           np.concatenate([g.imag, g.real], axis=2)], axis=1)
    ht = np.conj(np.transpose(g, (0, 2, 1)))
    ginv = np.concatenate([np.concatenate([ht.real, -ht.imag], axis=2),
                           np.concatenate([ht.imag, ht.real], axis=2)], axis=1)
    as32 = lambda a: np.asarray(a, dtype=np.float32)
    return as32(lhs_data), as32(lhs_real), as32(lhs_inv), as32(gfwd), as32(ginv)


def _split(x):
    hi = x.astype(BF16)
    lo = (x - hi.astype(F32)).astype(BF16)
    return hi, lo


def _dot3(a_hi, a_lo, b):
    b_hi, b_lo = _split(b)
    acc = jnp.dot(a_hi, b_hi, preferred_element_type=F32)
    acc = acc + jnp.dot(a_hi, b_lo, preferred_element_type=F32)
    acc = acc + jnp.dot(a_lo, b_hi, preferred_element_type=F32)
    return acc


def _fft1_kernel(lh_ref, ll_ref, a_ref, b_ref, o_ref):
    rhs = jnp.concatenate([a_ref[...], b_ref[...]], axis=0)
    o_ref[...] = _dot3(lh_ref[...], ll_ref[...], rhs)


def _fft1(lhs, x5, tn=8192):
    g, _, p, rows, nl = x5.shape
    lh, ll = _split(jnp.asarray(lhs))
    return pl.pallas_call(
        _fft1_kernel,
        grid=(g, p, nl // tn),
        in_specs=[
            pl.BlockSpec(lhs.shape, lambda q, j, t: (0, 0)),
            pl.BlockSpec(lhs.shape, lambda q, j, t: (0, 0)),
            pl.BlockSpec((None, None, None, rows, tn), lambda q, j, t: (q, 0, j, 0, t)),
            pl.BlockSpec((None, None, None, rows, tn), lambda q, j, t: (q, 1, j, 0, t)),
        ],
        out_specs=pl.BlockSpec((None, None, 2 * FFT_N1, tn), lambda q, j, t: (q, j, 0, t)),
        out_shape=jax.ShapeDtypeStruct((g, p, 2 * FFT_N1, nl), F32),
        compiler_params=_cparams(("parallel", "parallel", "parallel")),
    )(lh, ll, x5, x5)


def _fft3_kernel(lh_ref, ll_ref, w_ref, o_ref):
    half = FFT_N1 // 2
    res = _dot3(lh_ref[...], ll_ref[...], w_ref[...])
    o_ref[0] = res[:half]
    o_ref[1] = res[half:]


def _fft3(lhs, w4d, tn=8192):
    g, p, rows, nl = w4d.shape
    half = FFT_N1 // 2
    lh, ll = _split(jnp.asarray(lhs))
    return pl.pallas_call(
        _fft3_kernel,
        grid=(g, p, nl // tn),
        in_specs=[
            pl.BlockSpec(lhs.shape, lambda q, j, t: (0, 0)),
            pl.BlockSpec(lhs.shape, lambda q, j, t: (0, 0)),
            pl.BlockSpec((None, None, rows, tn), lambda q, j, t: (q, j, 0, t)),
        ],
        out_specs=pl.BlockSpec((None, 2, None, half, tn), lambda q, j, t: (q, 0, j, 0, t)),
        out_shape=jax.ShapeDtypeStruct((g, 2, p, half, nl), F32),
        compiler_params=_cparams(("parallel", "parallel", "parallel")),
    )(lh, ll, w4d)


def _fft2_filter_kernel(gh_ref, gl_ref, x_ref, o_ref):
    ct = x_ref.shape[-1]
    z = _dot3(gh_ref[...], gl_ref[...], x_ref[...].reshape(2 * FFT_N2, ct))
    o_ref[...] = z.reshape(2, FFT_N2, ct)


def _fft2_filter(gfwd_hl, x1f, ct=256):
    nd, _, _, _, d_b = x1f.shape
    gh, gl = gfwd_hl
    gspec = pl.BlockSpec((None, 2 * FFT_N2, 2 * FFT_N2), lambda d, c, k: (k, 0, 0))
    xspec = pl.BlockSpec((None, 2, None, FFT_N2, ct), lambda d, c, k: (d, 0, k, 0, c))
    return pl.pallas_call(
        _fft2_filter_kernel,
        grid=(nd, d_b // ct, FFT_N1),
        in_specs=[gspec, gspec, xspec],
        out_specs=xspec,
        out_shape=jax.ShapeDtypeStruct(x1f.shape, F32),
        compiler_params=_cparams(("parallel", "parallel", "parallel")),
    )(gh, gl, x1f)


def _fft2_mix_kernel(gh_ref, gl_ref, ih_ref, il_ref, x_ref, k_ref, o_ref, *, nblocks):
    ct = x_ref.shape[-1]
    zs = []
    for j in range(nblocks):
        z = _dot3(gh_ref[...], gl_ref[...], x_ref[j].reshape(2 * FFT_N2, ct))
        zs.append((z[:FFT_N2], z[FFT_N2:]))
    for i in range(nblocks):
        yr = jnp.zeros((FFT_N2, ct), F32)
        yi = jnp.zeros((FFT_N2, ct), F32)
        for j in range(nblocks):
            d = i - j + nblocks - 1
            kr, ki = k_ref[d, 0], k_ref[d, 1]
            zr, zi = zs[j]
            yr = yr + kr * zr - ki * zi
            yi = yi + kr * zi + ki * zr
        w = _dot3(ih_ref[...], il_ref[...], jnp.concatenate([yr, yi], axis=0))
        o_ref[i] = w.reshape(2, FFT_N2, ct)


def _fft2_mix(gfwd_hl, ginv_hl, x1, kspec, ct=256):
    g, p, _, _, _, d_b = x1.shape
    nd = kspec.shape[0]
    gspec = pl.BlockSpec((None, 2 * FFT_N2, 2 * FFT_N2), lambda q, c, k: (k, 0, 0))
    xspec = pl.BlockSpec((None, p, 2, None, FFT_N2, ct), lambda q, c, k: (q, 0, 0, k, 0, c))
    kern = functools.partial(_fft2_mix_kernel, nblocks=p)
    return pl.pallas_call(
        kern,
        grid=(g, d_b // ct, FFT_N1),
        in_specs=[gspec, gspec, gspec, gspec, xspec,
                  pl.BlockSpec((nd, 2, None, FFT_N2, ct), lambda q, c, k: (0, 0, k, 0, c))],
        out_specs=xspec,
        out_shape=jax.ShapeDtypeStruct(x1.shape, F32),
        compiler_params=_cparams(("parallel", "parallel", "parallel")),
    )(gfwd_hl[0], gfwd_hl[1], ginv_hl[0], ginv_hl[1], x1, kspec)


def _hyena_long_conv(g3, w1, b1, w2, b2, w3, b3, freq, w4):
    bsz, seq_len, d_b = g3.shape
    nblocks = seq_len // CONV_BLOCK
    nl = FFT_N2 * d_b
    half = FFT_N1 // 2
    lhs_data, lhs_real, lhs_inv, gfwd, ginv = _dft_tables()
    gfwd_hl = _split(jnp.asarray(gfwd))
    ginv_hl = _split(jnp.asarray(ginv))
    cblocks = _hyena_filter_blocks(seq_len, w1, b1, w2, b2, w3, b3, freq, w4)
    nd = cblocks.shape[0]
    k1f = _fft1(lhs_real, cblocks.reshape(nd, 2, 1, half, nl))
    kspec = _fft2_filter(gfwd_hl, k1f.reshape(nd, 2, FFT_N1, FFT_N2, d_b))
    x1 = _fft1(lhs_data, g3.reshape(bsz // 2, 2, nblocks, half, nl))
    wmix = _fft2_mix(gfwd_hl, ginv_hl, x1.reshape(bsz // 2, nblocks, 2, FFT_N1, FFT_N2, d_b), kspec)
    y5 = _fft3(lhs_inv, wmix.reshape(bsz // 2, nblocks, 2 * FFT_N1, nl))
    return y5.reshape(bsz, seq_len, d_b)


def _postmix_kernel(x_ref, yan_ref, x0_ref, g_ref, yc_ref, bias_ref, mgb_ref, wo_ref, n2_ref, x1_ref, xn_ref):
    d_a = yan_ref.shape[1]
    gg = g_ref[...]
    yb = x0_ref[...] * (yc_ref[...] + gg * bias_ref[...])
    ybn = _rms(yb, mgb_ref[...]).astype(BF16)
    y = jnp.dot(yan_ref[...], wo_ref[0:d_a, :], preferred_element_type=F32)
    y = y + jnp.dot(ybn, wo_ref[d_a:, :], preferred_element_type=F32)
    x1 = x_ref[...] + y
    x1_ref[...] = x1
    xn_ref[...] = _rms(x1, n2_ref[...]).astype(xn_ref.dtype)


def _postmix(x, yan, x0, g, yconv, bias, mix_g_b, w_out, norm2, tm=512):
    t, d = x.shape
    d_a = yan.shape[1]
    d_b = x0.shape[1]
    row = lambda w: pl.BlockSpec((tm, w), lambda i: (i, 0))
    const = lambda *shape: pl.BlockSpec(shape, lambda i: (0,) * len(shape), pipeline_mode=pl.Buffered(1))
    return pl.pallas_call(
        _postmix_kernel,
        grid=(t // tm,),
        in_specs=[row(d), row(d_a), row(d_b), row(d_b), row(d_b), const(1, d_b), const(1, d_b),
                  const(d_a + d_b, d), const(1, d)],
        out_specs=[row(d), row(d)],
        out_shape=[jax.ShapeDtypeStruct((t, d), F32), jax.ShapeDtypeStruct((t, d), BF16)],
        compiler_params=_cparams(("parallel",)),
    )(x, yan, x0, g, yconv, bias, mix_g_b, w_out, norm2)


def _staircase():
    return [(i, j) for i in range(PEER_TOPK) for j in range(PEER_TOPK) if (i + 1) * (j + 1) <= PEER_TOPK]


def _topk_kernel(xn_ref, wqt_ref, kbd_ref, e_ref, gt_ref, s_s, v_s, i_s, c_s, ce_s, b_s, es_s, *, heads):
    tm = xn_ref.shape[0]
    hk = heads * HALF_KEY
    neg = np.float32(-np.inf)
    q_t = lax.dot_general(wqt_ref[...], xn_ref[...], (((1,), (1,)), ((), ())),
                          preferred_element_type=F32).astype(BF16)
    iota_n = lax.broadcasted_iota(I32, (N_KEYS, heads, tm), 0)
    for half in range(2):
        s_t = jnp.dot(kbd_ref[half], q_t[half * hk:(half + 1) * hk], preferred_element_type=F32)
        s_s[...] = s_t.reshape(N_KEYS, heads, tm)

        def level1(r, carry, half=half):
            s = s_s[...]
            m = jnp.max(s, axis=0)
            idx = jnp.min(jnp.where(s == m[None], iota_n, N_KEYS), axis=0)
            s_s[...] = jnp.where(iota_n == idx[None], neg, s)
            v_s[half, r] = m
            i_s[half, r] = idx
            return carry

        lax.fori_loop(0, PEER_TOPK, level1, 0)

    cands = _staircase()
    for p, (i, j) in enumerate(cands):
        c_s[p] = v_s[0, i] + v_s[1, j]
        ce_s[p] = i_s[0, i] * N_KEYS + i_s[1, j]
    ncand = len(cands)
    iota_p = lax.broadcasted_iota(I32, (ncand, heads, tm), 0)

    def level2(r, carry):
        c = c_s[...]
        m = jnp.max(c, axis=0)
        pid = jnp.min(jnp.where(c == m[None], iota_p, ncand), axis=0)
        knock = iota_p == pid[None]
        es_s[r] = jnp.max(jnp.where(knock, ce_s[...], -1), axis=0)
        c_s[...] = jnp.where(knock, neg, c)
        b_s[r] = m
        return carry

    lax.fori_loop(0, PEER_TOPK, level2, 0)
    best = b_s[...]
    ex = jnp.exp(best - jnp.max(best, axis=0, keepdims=True))
    gate = ex / jnp.sum(ex, axis=0, keepdims=True)
    gt_ref[...] = gate.reshape(PEER_TOPK * heads, tm).T
    e_ref[...] = es_s[...].reshape(PEER_TOPK * heads, tm).T


def _peer_topk(xn, wqt, kbd, heads, tm=256):
    t, d = xn.shape
    nsel = PEER_TOPK * heads
    ncand = len(_staircase())
    const = lambda *shape: pl.BlockSpec(shape, lambda i: (0,) * len(shape), pipeline_mode=pl.Buffered(1))
    kern = functools.partial(_topk_kernel, heads=heads)
    return pl.pallas_call(
        kern,
        grid=(t // tm,),
        in_specs=[pl.BlockSpec((tm, d), lambda i: (i, 0)), const(*wqt.shape), const(*kbd.shape)],
        out_specs=[pl.BlockSpec((tm, nsel), lambda i: (i, 0)), pl.BlockSpec((tm, nsel), lambda i: (i, 0))],
        out_shape=[jax.ShapeDtypeStruct((t, nsel), I32), jax.ShapeDtypeStruct((t, nsel), F32)],
        scratch_shapes=[
            pltpu.VMEM((N_KEYS, heads, tm), F32),
            pltpu.VMEM((2, PEER_TOPK, heads, tm), F32),
            pltpu.VMEM((2, PEER_TOPK, heads, tm), I32),
            pltpu.VMEM((ncand, heads, tm), F32),
            pltpu.VMEM((ncand, heads, tm), I32),
            pltpu.VMEM((PEER_TOPK, heads, tm), F32),
            pltpu.VMEM((PEER_TOPK, heads, tm), I32),
        ],
        compiler_params=_cparams(("parallel",)),
    )(xn, wqt, kbd)


def _matmul_kernel(a_ref, b_ref, o_ref):
    o_ref[...] = jnp.dot(a_ref[...], b_ref[...], preferred_element_type=F32).astype(o_ref.dtype)


def _peer_a(xn, down_t, tm=512, tn=2048):
    t, d = xn.shape
    n = down_t.shape[1]
    return pl.pallas_call(
        _matmul_kernel,
        grid=(n // tn, t // tm),
        in_specs=[pl.BlockSpec((tm, d), lambda j, i: (i, 0)), pl.BlockSpec((d, tn), lambda j, i: (0, j))],
        out_specs=pl.BlockSpec((tm, tn), lambda j, i: (i, j)),
        out_shape=jax.ShapeDtypeStruct((t, n), F32),
        compiler_params=_cparams(("parallel", "parallel")),
    )(xn, down_t)


def _peer_b_kernel(a_ref, e_ref, gt_ref, m_ref, w_s):
    c_tok = a_ref.shape[0]
    nsel = e_ref.shape[1]

    def group(gi, carry):
        r0 = pl.multiple_of(gi * V7X_SUBLANES, V7X_SUBLANES)
        rows = pl.ds(r0, V7X_SUBLANES)
        e = e_ref[rows, :]
        ai = lax.shift_right_logical(e, KEY_SHIFT)
        bi = e & (N_KEYS - 1)
        acc = jnp.zeros((V7X_SUBLANES, nsel), F32)
        for a in range(N_KEYS):
            blk = a_ref[rows, a * N_KEYS:(a + 1) * N_KEYS]
            acc = jnp.where(ai == a, jnp.take_along_axis(blk, bi, axis=1), acc)
        w_s[rows, :] = gt_ref[rows, :] * _gelu(acc)
        return carry

    lax.fori_loop(0, c_tok // V7X_SUBLANES, group, 0)
    sub = lax.broadcasted_iota(I32, (N_KEYS, nsel), 0)

    def token(c, carry):
        e = e_ref[pl.ds(c, 1), :]
        w = w_s[pl.ds(c, 1), :]
        w1t = jnp.where(sub == lax.shift_right_logical(e, KEY_SHIFT), w, 0.0).astype(BF16)
        e2t = jnp.where(sub == (e & (N_KEYS - 1)), 1.0, 0.0).astype(BF16)
        m = lax.dot_general(w1t, e2t, (((1,), (1,)), ((), ())), preferred_element_type=F32)
        m_ref[c] = m.astype(m_ref.dtype)
        return carry

    lax.fori_loop(0, c_tok, token, 0)


def _peer_b(a, eidx, gate, tc=128):
    t, n = a.shape
    nsel = eidx.shape[1]
    return pl.pallas_call(
        _peer_b_kernel,
        grid=(t // tc,),
        in_specs=[pl.BlockSpec((tc, n), lambda i: (i, 0)), pl.BlockSpec((tc, nsel), lambda i: (i, 0)),
                  pl.BlockSpec((tc, nsel), lambda i: (i, 0))],
        out_specs=pl.BlockSpec((tc, N_KEYS, N_KEYS), lambda i: (i, 0, 0)),
        out_shape=jax.ShapeDtypeStruct((t, N_KEYS, N_KEYS), BF16),
        scratch_shapes=[pltpu.VMEM((tc, nsel), F32)],
        compiler_params=_cparams(("parallel",)),
    )(a, eidx, gate)


def _peer_c_kernel(m_ref, up_ref, x1_ref, fn_ref, o_ref, acc_s):
    k = pl.program_id(1)

    @pl.when(k == 0)
    def _():
        acc_s[...] = jnp.zeros_like(acc_s)

    acc_s[...] += jnp.dot(m_ref[...], up_ref[...], preferred_element_type=F32)

    @pl.when(k == pl.num_programs(1) - 1)
    def _():
        o_ref[...] = _rms(x1_ref[...] + acc_s[...], fn_ref[...])


def _peer_c(m, up, x1, final_norm, tm=512, tk=1024):
    t, n = m.shape
    d = up.shape[1]
    return pl.pallas_call(
        _peer_c_kernel,
        grid=(t // tm, n // tk),
        in_specs=[pl.BlockSpec((tm, tk), lambda i, k: (i, k)), pl.BlockSpec((tk, d), lambda i, k: (k, 0)),
                  pl.BlockSpec((tm, d), lambda i, k: (i, 0)), pl.BlockSpec((1, d), lambda i, k: (0, 0))],
        out_specs=pl.BlockSpec((tm, d), lambda i, k: (i, 0)),
        out_shape=jax.ShapeDtypeStruct((t, d), F32),
        scratch_shapes=[pltpu.VMEM((tm, d), F32)],
        compiler_params=_cparams(("parallel", "arbitrary")),
    )(m, up, x1, final_norm)


def _prep_weights(norm1, w_in, a_ln_g, a_ln_b, a_ws, a_bs, b_conv_w, b_conv_b, b_bias, mix_norm, w_out, norm2,
                  peer_wq, peer_k1, peer_k2, peer_down, peer_up, final_norm):
    d = w_in.shape[0]
    d_a = a_ln_g.shape[0]
    d_b = b_bias.shape[0]
    heads, n_keys, half_key = peer_k1.shape
    row = lambda v: v.reshape(1, -1).astype(F32)
    w = dict(
        norm1=row(norm1), norm2=row(norm2), final_norm=row(final_norm),
        w_u=w_in[:, :d_a].astype(BF16), w_v=w_in[:, d_a:2 * d_a].astype(BF16),
        w_b=w_in[:, 2 * d_a:].reshape(d, 3, d_b).transpose(1, 0, 2).astype(BF16),
        ln_g=row(a_ln_g), ln_b=row(a_ln_b),
        ws=a_ws.astype(BF16),
        bsb=jnp.broadcast_to(a_bs[:, :, None], a_bs.shape + (A_HEAD_DIM,)).astype(F32),
        conv_w=b_conv_w.reshape(3, 3, d_b).transpose(1, 0, 2).astype(F32),
        conv_b=b_conv_b.reshape(3, d_b).astype(F32),
        bias=row(b_bias), mix_g_a=row(mix_norm[:d_a]), mix_g_b=row(mix_norm[d_a:]),
        w_out=w_out.astype(BF16),
        down_t=peer_down.astype(BF16).T, up=peer_up.astype(BF16),
    )
    wq_t = peer_wq.T.reshape(heads, 2, half_key, d).transpose(1, 0, 2, 3).reshape(2 * heads * half_key, d)
    w["wq_t"] = wq_t.astype(BF16)
    eye = jnp.eye(heads, dtype=F32)
    kbd = [jnp.einsum("hnd,hg->nhgd", k, eye).reshape(n_keys * heads, heads * half_key) for k in (peer_k1, peer_k2)]
    w["kbd"] = jnp.stack(kbd).astype(BF16)
    w["heads"] = heads
    return w


def _trunk(x3, w, hf):
    bsz, seq_len, d = x3.shape
    t = bsz * seq_len
    x = x3.reshape(t, d)
    yan, x0, g = _inproj(x, seq_len, w["norm1"], w["w_u"], w["w_v"], w["w_b"], w["ln_g"], w["ln_b"],
                         w["ws"], w["bsb"], w["conv_w"], w["conv_b"], w["mix_g_a"])
    d_b = x0.shape[1]
    yconv = _hyena_long_conv(g.reshape(bsz, seq_len, d_b), *hf).reshape(t, d_b)
    x1, xn = _postmix(x, yan, x0, g, yconv, w["bias"], w["mix_g_b"], w["w_out"], w["norm2"])
    eidx, gate = _peer_topk(xn, w["wq_t"], w["kbd"], w["heads"])
    a = _peer_a(xn, w["down_t"])
    m = _peer_b(a, eidx, gate).reshape(t, a.shape[1])
    out = _peer_c(m, w["up"], x1, w["final_norm"])
    return out.reshape(bsz, seq_len, d)


def kernel(x_prompt, x_sample, norm1, w_in, a_ln_g, a_ln_b, a_ws, a_bs, b_conv_w, b_conv_b, hf_w1, hf_b1, hf_w2, hf_b2, hf_w3, hf_b3, hf_freq, hf_w4, b_bias, mix_norm, w_out, norm2, peer_wq, peer_k1, peer_k2, peer_down, peer_up, final_norm):
    assert norm1.shape[0] == 1, "single-layer trunk"
    w = _prep_weights(norm1[0], w_in[0], a_ln_g[0], a_ln_b[0], a_ws[0], a_bs[0], b_conv_w[0], b_conv_b[0],
                      b_bias[0], mix_norm[0], w_out[0], norm2[0], peer_wq[0], peer_k1[0], peer_k2[0],
                      peer_down[0], peer_up[0], final_norm)
    hf = (hf_w1[0], hf_b1[0].reshape(1, -1), hf_w2[0], hf_b2[0].reshape(1, -1), hf_w3[0], hf_b3[0].reshape(1, -1),
          hf_freq[0], hf_w4[0])
    return (_trunk(x_prompt, w, hf), _trunk(x_sample, w, hf))
```

```python
import functools
import math

import numpy as np
import jax
import jax.numpy as jnp
from jax import lax
from jax.experimental import pallas as pl
from jax.experimental.pallas import tpu as pltpu

F32 = jnp.float32
BF16 = jnp.bfloat16
I32 = jnp.int32

EPS = 1e-6
V7X_LANES = 128
V7X_SUBLANES = 8
V7X_VMEM_BYTES = 64 * 1024 * 1024
VMEM_LIMIT = V7X_VMEM_BYTES - 8 * 1024 * 1024

CHUNK = 128
A_HEAD_DIM = 128
EMB_DIM = 33
BANDS = (EMB_DIM - 1) // 2
DECAY_TARGET = 1e-2
FAST_DECAY_PCT = 0.3
SLOW_DECAY_PCT = 1.5
FFT_N1 = 64
FFT_N2 = 128
FFT_N = FFT_N1 * FFT_N2
CONV_BLOCK = FFT_N // 2
N_KEYS = 128
KEY_SHIFT = 7
PEER_TOPK = 16
HALF_KEY = 128


def _cparams(sem):
    return pltpu.CompilerParams(dimension_semantics=sem, vmem_limit_bytes=VMEM_LIMIT)


def _rms(xf, g):
    return xf * lax.rsqrt(jnp.mean(xf * xf, axis=-1, keepdims=True) + EPS) * g


def _gelu(x):
    return 0.5 * x * (1.0 + lax.erf(x * np.float32(math.sqrt(0.5))))


def _inproj_kernel(xp_ref, x_ref, xn_ref, n1_ref, wu_ref, wv_ref, wb_ref, lng_ref, lnb_ref,
                   ws_ref, bsb_ref, cw_ref, cb_ref, mga_ref,
                   yan_ref, x0_ref, g_ref, ya_s, zs_s, *, tiles_per_seq, cblk):
    tm = x_ref.shape[0]
    d_a = wu_ref.shape[1]
    d_b = wb_ref.shape[2]
    i = pl.program_id(0)
    not_first = (i % tiles_per_seq != 0).astype(F32)
    not_last = (i % tiles_per_seq != tiles_per_seq - 1).astype(F32)

    xcat = jnp.concatenate([xp_ref[...], x_ref[...], xn_ref[...]], axis=0)
    hcat = _rms(xcat, n1_ref[...])
    h_all = hcat.astype(BF16)
    h = hcat[V7X_SUBLANES:V7X_SUBLANES + tm].astype(BF16)

    u = _gelu(jnp.dot(h, wu_ref[...], preferred_element_type=F32))
    v = _gelu(jnp.dot(h, wv_ref[...], preferred_element_type=F32))
    mu = jnp.mean(v, axis=-1, keepdims=True)
    vc = v - mu
    var = jnp.mean(vc * vc, axis=-1, keepdims=True)
    vb = (vc * lax.rsqrt(var + EPS) * lng_ref[...] + lnb_ref[...]).astype(BF16)
    for c in range(tm // CHUNK):
        rows = slice(c * CHUNK, (c + 1) * CHUNK)
        for hd in range(d_a // A_HEAD_DIM):
            cols = slice(hd * A_HEAD_DIM, (hd + 1) * A_HEAD_DIM)
            mixed = jnp.dot(ws_ref[hd], vb[rows, cols], preferred_element_type=F32) + bsb_ref[hd]
            ya_s[rows, cols] = u[rows, cols] * mixed
    yan_ref[...] = _rms(ya_s[...], mga_ref[...]).astype(yan_ref.dtype)

    halo = V7X_SUBLANES
    for cb in range(d_b // cblk):
        cols = slice(cb * cblk, (cb + 1) * cblk)
        parts = []
        for p in range(3):
            z = jnp.dot(h_all, wb_ref[p, :, cols], preferred_element_type=F32)
            zs_s[...] = z
            zs_s[0:halo, :] = z[0:halo] * not_first
            zs_s[tm + halo:tm + 2 * halo, :] = z[tm + halo:tm + 2 * halo] * not_last
            w = cw_ref[p]
            zc = (cb_ref[p:p + 1, cols]
                  + zs_s[halo - 1:halo - 1 + tm, :] * w[0:1, cols]
                  + zs_s[halo:halo + tm, :] * w[1:2, cols]
                  + zs_s[halo + 1:halo + 1 + tm, :] * w[2:3, cols])
            parts.append(zc)
        x0_ref[:, cols] = parts[0]
        g_ref[:, cols] = parts[1] * parts[2]


def _inproj(x, seq_len, norm1, w_u, w_v, w_b, ln_g, ln_b, ws, bsb, conv_w, conv_b, mix_g_a, tm=512, cblk=256):
    t, d = x.shape
    d_a = w_u.shape[1]
    d_b = w_b.shape[2]
    assert seq_len % tm == 0 and t % seq_len == 0 and tm % CHUNK == 0 and d_b % cblk == 0, (t, seq_len)
    nblk = tm // V7X_SUBLANES
    last8 = t // V7X_SUBLANES - 1
    const = lambda *shape: pl.BlockSpec(shape, lambda i: (0,) * len(shape), pipeline_mode=pl.Buffered(1))
    kern = functools.partial(_inproj_kernel, tiles_per_seq=seq_len // tm, cblk=cblk)
    return pl.pallas_call(
        kern,
        grid=(t // tm,),
        in_specs=[
            pl.BlockSpec((V7X_SUBLANES, d), lambda i: (jnp.maximum(i * nblk - 1, 0), 0)),
            pl.BlockSpec((tm, d), lambda i: (i, 0)),
            pl.BlockSpec((V7X_SUBLANES, d), lambda i: (jnp.minimum((i + 1) * nblk, last8), 0)),
            const(1, d), const(d, d_a), const(d, d_a), const(3, d, d_b), const(1, d_a), const(1, d_a),
            const(*ws.shape), const(*bsb.shape), const(3, 3, d_b), const(3, d_b), const(1, d_a),
        ],
        out_specs=[
            pl.BlockSpec((tm, d_a), lambda i: (i, 0)),
            pl.BlockSpec((tm, d_b), lambda i: (i, 0)),
            pl.BlockSpec((tm, d_b), lambda i: (i, 0)),
        ],
        out_shape=[
            jax.ShapeDtypeStruct((t, d_a), BF16),
            jax.ShapeDtypeStruct((t, d_b), F32),
            jax.ShapeDtypeStruct((t, d_b), F32),
        ],
        scratch_shapes=[pltpu.VMEM((tm, d_a), F32), pltpu.VMEM((tm + 2 * V7X_SUBLANES, cblk), F32)],
        compiler_params=_cparams(("parallel",)),
        name="inproj",
    )(x, x, x, norm1, w_u, w_v, w_b, ln_g, ln_b, ws, bsb, conv_w, conv_b, mix_g_a)


def _filt_kernel(fr_ref, dl_ref, w1_ref, b1_ref, w2_ref, b2_ref, w3_ref, b3_ref, fq_ref, w4_ref, o_ref,
                 *, seq_len, nblocks, tr):
    di = pl.program_id(0)
    s = pl.program_id(1)
    r = pl.program_id(2)
    e = di - (nblocks - 1) - s
    trow = r * tr + lax.broadcasted_iota(I32, (tr, 1), 0)
    lag = e * CONV_BLOCK + trow
    pos = jnp.abs(lag).astype(F32)
    valid = jnp.logical_and(jnp.abs(lag) <= seq_len - 1, jnp.logical_or(s == 0, trow != 0))
    t = pos / np.float32(max(seq_len - 1, 1))
    fw = np.float32(2.0 * math.pi / seq_len) * pos * fr_ref[...]
    hi = lax.Precision.HIGHEST
    w1 = w1_ref[...]
    z1 = (t * w1[0:1, :]
          + jnp.dot(jnp.cos(fw), w1[1:1 + BANDS, :], precision=hi, preferred_element_type=F32)
          + jnp.dot(-jnp.sin(fw), w1[1 + BANDS:, :], precision=hi, preferred_element_type=F32))
    fq = fq_ref[...]
    h = jnp.sin(fq[0:1, :] * (z1 + b1_ref[...]))
    h = jnp.sin(fq[1:2, :] * (jnp.dot(h, w2_ref[...], precision=hi, preferred_element_type=F32) + b2_ref[...]))
    h = jnp.sin(fq[2:3, :] * (jnp.dot(h, w3_ref[...], precision=hi, preferred_element_type=F32) + b3_ref[...]))
    h4 = jnp.dot(h, w4_ref[...], precision=hi, preferred_element_type=F32)
    window = jnp.exp(-t * dl_ref[...])
    o_ref[...] = jnp.where(valid, h4 * window, 0.0)


def _hyena_filter_blocks(seq_len, w1, b1, w2, b2, w3, b3, freq, w4, tr=512):
    nblocks = seq_len // CONV_BLOCK
    nd = 2 * nblocks - 1
    width = w1.shape[1]
    d_b = w4.shape[1] // 2
    fr = jnp.asarray(np.linspace(1e-4, BANDS - 1, BANDS, dtype=np.float32)[None, :])
    min_decay = math.log(DECAY_TARGET) / SLOW_DECAY_PCT
    max_decay = math.log(DECAY_TARGET) / FAST_DECAY_PCT
    deltas = jnp.asarray(np.abs(np.linspace(min_decay, max_decay, d_b, dtype=np.float32))[None, :])
    w4h = w4.reshape(width, 2, d_b).transpose(1, 0, 2)
    const = lambda *shape: pl.BlockSpec(shape, lambda a, b, c: (0,) * len(shape))
    kern = functools.partial(_filt_kernel, seq_len=seq_len, nblocks=nblocks, tr=tr)
    half_of = lambda di, s, r: jnp.where(di - (nblocks - 1) - s >= 0, 0, 1)
    return pl.pallas_call(
        kern,
        grid=(nd, 2, CONV_BLOCK // tr),
        in_specs=[
            const(1, BANDS), const(1, d_b), const(EMB_DIM, width), const(1, width), const(width, width),
            const(1, width), const(width, width), const(1, width), const(3, width),
            pl.BlockSpec((None, width, d_b), lambda di, s, r: (half_of(di, s, r), 0, 0)),
        ],
        out_specs=pl.BlockSpec((None, None, tr, d_b), lambda di, s, r: (di, s, r, 0)),
        out_shape=jax.ShapeDtypeStruct((nd, 2, CONV_BLOCK, d_b), F32),
        compiler_params=_cparams(("parallel", "parallel", "parallel")),
        name="hyena_filter",
    )(fr, deltas, w1, b1, w2, b2, w3, b3, freq, w4h)


def _dft_tables():
    n1 = np.arange(FFT_N1)
    f64 = np.exp(-2j * np.pi * np.outer(n1, n1) / FFT_N1)
    half = FFT_N1 // 2
    fr, fi = f64.real, f64.imag
    lhs_data = np.block([[fr[:, :half], -fi[:, :half]], [fi[:, :half], fr[:, :half]]])
    lhs_real = np.concatenate([fr, fi], axis=0)
    gr, gi = fr[:half, :] / FFT_N, -fi[:half, :] / FFT_N
    lhs_inv = np.block([[gr, -gi], [gi, gr]])
    k1 = np.arange(FFT_N1)[:, None, None]
    k2 = np.arange(FFT_N2)[None, :, None]
    n2 = np.arange(FFT_N2)[None, None, :]
    g = np.exp(-2j * np.pi * (n2 * (k1 + FFT_N1 * k2) % FFT_N) / FFT_N)
    gfwd = np.concatenate([np.concatenate([g.real, -g.imag], axis=2),
                           np.concatenate([g.imag, g.real], axis=2)], axis=1)
    ht = np.conj(np.transpose(g, (0, 2, 1)))
    ginv = np.concatenate([np.concatenate([ht.real, -ht.imag], axis=2),
                           np.concatenate([ht.imag, ht.real], axis=2)], axis=1)
    as32 = lambda a: np.asarray(a, dtype=np.float32)
    return as32(lhs_data), as32(lhs_real), as32(lhs_inv), as32(gfwd), as32(ginv)


def _split(x):
    hi = x.astype(BF16)
    lo = (x - hi.astype(F32)).astype(BF16)
    return hi, lo


def _dot3(a_hi, a_lo, b):
    b_hi, b_lo = _split(b)
    acc = jnp.dot(a_hi, b_hi, preferred_element_type=F32)
    acc = acc + jnp.dot(a_hi, b_lo, preferred_element_type=F32)
    acc = acc + jnp.dot(a_lo, b_hi, preferred_element_type=F32)
    return acc


def _fft1_kernel(lh_ref, ll_ref, a_ref, b_ref, o_ref):
    rhs = jnp.concatenate([a_ref[...], b_ref[...]], axis=0)
    o_ref[...] = _dot3(lh_ref[...], ll_ref[...], rhs)


def _fft1(lhs, x5, tn=8192):
    g, _, p, rows, nl = x5.shape
    lh, ll = _split(jnp.asarray(lhs))
    return pl.pallas_call(
        _fft1_kernel,
        grid=(g, p, nl // tn),
        in_specs=[
            pl.BlockSpec(lhs.shape, lambda q, j, t: (0, 0)),
            pl.BlockSpec(lhs.shape, lambda q, j, t: (0, 0)),
            pl.BlockSpec((None, None, None, rows, tn), lambda q, j, t: (q, 0, j, 0, t)),
            pl.BlockSpec((None, None, None, rows, tn), lambda q, j, t: (q, 1, j, 0, t)),
        ],
        out_specs=pl.BlockSpec((None, None, 2 * FFT_N1, tn), lambda q, j, t: (q, j, 0, t)),
        out_shape=jax.ShapeDtypeStruct((g, p, 2 * FFT_N1, nl), F32),
        compiler_params=_cparams(("parallel", "parallel", "parallel")),
        name="fft1",
    )(lh, ll, x5, x5)


def _fft3_kernel(lh_ref, ll_ref, w_ref, o_ref):
    half = FFT_N1 // 2
    res = _dot3(lh_ref[...], ll_ref[...], w_ref[...])
    o_ref[0] = res[:half]
    o_ref[1] = res[half:]


def _fft3(lhs, w4d, tn=8192):
    g, p, rows, nl = w4d.shape
    half = FFT_N1 // 2
    lh, ll = _split(jnp.asarray(lhs))
    return pl.pallas_call(
        _fft3_kernel,
        grid=(g, p, nl // tn),
        in_specs=[
            pl.BlockSpec(lhs.shape, lambda q, j, t: (0, 0)),
            pl.BlockSpec(lhs.shape, lambda q, j, t: (0, 0)),
            pl.BlockSpec((None, None, rows, tn), lambda q, j, t: (q, j, 0, t)),
        ],
        out_specs=pl.BlockSpec((None, 2, None, half, tn), lambda q, j, t: (q, 0, j, 0, t)),
        out_shape=jax.ShapeDtypeStruct((g, 2, p, half, nl), F32),
        compiler_params=_cparams(("parallel", "parallel", "parallel")),
        name="fft3",
    )(lh, ll, w4d)


def _fft2_filter_kernel(gh_ref, gl_ref, x_ref, o_ref):
    ct = x_ref.shape[-1]
    for k in range(x_ref.shape[1]):
        z = _dot3(gh_ref[k], gl_ref[k], x_ref[:, k].reshape(2 * FFT_N2, ct))
        o_ref[:, k] = z.reshape(2, FFT_N2, ct)


def _fft2_filter(gfwd_hl, x1f, ct=256, kb=8):
    nd, _, _, _, d_b = x1f.shape
    gh, gl = gfwd_hl
    gspec = pl.BlockSpec((kb, 2 * FFT_N2, 2 * FFT_N2), lambda k, d, c: (k, 0, 0))
    xspec = pl.BlockSpec((None, 2, kb, FFT_N2, ct), lambda k, d, c: (d, 0, k, 0, c))
    return pl.pallas_call(
        _fft2_filter_kernel,
        grid=(FFT_N1 // kb, nd, d_b // ct),
        in_specs=[gspec, gspec, xspec],
        out_specs=xspec,
        out_shape=jax.ShapeDtypeStruct(x1f.shape, F32),
        compiler_params=_cparams(("parallel", "parallel", "parallel")),
        name="fft2_filter",
    )(gh, gl, x1f)


def _fft2_mix_kernel(gh_ref, gl_ref, ih_ref, il_ref, x_ref, k_ref, o_ref, *, nblocks):
    ct = x_ref.shape[-1]
    for k in range(x_ref.shape[2]):
        zs = []
        for j in range(nblocks):
            z = _dot3(gh_ref[k], gl_ref[k], x_ref[j, :, k].reshape(2 * FFT_N2, ct))
            zs.append((z[:FFT_N2], z[FFT_N2:]))
        for i in range(nblocks):
            yr = jnp.zeros((FFT_N2, ct), F32)
            yi = jnp.zeros((FFT_N2, ct), F32)
            for j in range(nblocks):
                d = i - j + nblocks - 1
                kr, ki = k_ref[d, 0, k], k_ref[d, 1, k]
                zr, zi = zs[j]
                yr = yr + kr * zr - ki * zi
                yi = yi + kr * zi + ki * zr
            w = _dot3(ih_ref[k], il_ref[k], jnp.concatenate([yr, yi], axis=0))
            o_ref[i, :, k] = w.reshape(2, FFT_N2, ct)


def _fft2_mix(gfwd_hl, ginv_hl, x1, kspec, ct=256):
    g, p, _, _, _, d_b = x1.shape
    nd = kspec.shape[0]
    kb = max(1, 8 // p)
    gspec = pl.BlockSpec((kb, 2 * FFT_N2, 2 * FFT_N2), lambda k, q, c: (k, 0, 0))
    xspec = pl.BlockSpec((None, p, 2, kb, FFT_N2, ct), lambda k, q, c: (q, 0, 0, k, 0, c))
    kern = functools.partial(_fft2_mix_kernel, nblocks=p)
    return pl.pallas_call(
        kern,
        grid=(FFT_N1 // kb, g, d_b // ct),
        in_specs=[gspec, gspec, gspec, gspec, xspec,
                  pl.BlockSpec((nd, 2, kb, FFT_N2, ct), lambda k, q, c: (0, 0, k, 0, c))],
        out_specs=xspec,
        out_shape=jax.ShapeDtypeStruct(x1.shape, F32),
        compiler_params=_cparams(("parallel", "parallel", "parallel")),
        name="fft2_mix",
    )(gfwd_hl[0], gfwd_hl[1], ginv_hl[0], ginv_hl[1], x1, kspec)


def _hyena_long_conv(g3, w1, b1, w2, b2, w3, b3, freq, w4):
    bsz, seq_len, d_b = g3.shape
    nblocks = seq_len // CONV_BLOCK
    nl = FFT_N2 * d_b
    half = FFT_N1 // 2
    lhs_data, lhs_real, lhs_inv, gfwd, ginv = _dft_tables()
    gfwd_hl = _split(jnp.asarray(gfwd))
    ginv_hl = _split(jnp.asarray(ginv))
    cblocks = _hyena_filter_blocks(seq_len, w1, b1, w2, b2, w3, b3, freq, w4)
    nd = cblocks.shape[0]
    k1f = _fft1(lhs_real, cblocks.reshape(nd, 2, 1, half, nl))
    kspec = _fft2_filter(gfwd_hl, k1f.reshape(nd, 2, FFT_N1, FFT_N2, d_b))
    x1 = _fft1(lhs_data, g3.reshape(bsz // 2, 2, nblocks, half, nl))
    wmix = _fft2_mix(gfwd_hl, ginv_hl, x1.reshape(bsz // 2, nblocks, 2, FFT_N1, FFT_N2, d_b), kspec)
    y5 = _fft3(lhs_inv, wmix.reshape(bsz // 2, nblocks, 2 * FFT_N1, nl))
    return y5.reshape(bsz, seq_len, d_b)


def _postmix_kernel(x_ref, yan_ref, x0_ref, g_ref, yc_ref, bias_ref, mgb_ref, wo_ref, n2_ref, x1_ref, xn_ref):
    d_a = yan_ref.shape[1]
    gg = g_ref[...]
    yb = x0_ref[...] * (yc_ref[...] + gg * bias_ref[...])
    ybn = _rms(yb, mgb_ref[...]).astype(BF16)
    y = jnp.dot(yan_ref[...], wo_ref[0:d_a, :], preferred_element_type=F32)
    y = y + jnp.dot(ybn, wo_ref[d_a:, :], preferred_element_type=F32)
    x1 = x_ref[...] + y
    x1_ref[...] = x1
    xn_ref[...] = _rms(x1, n2_ref[...]).astype(xn_ref.dtype)


def _postmix(x, yan, x0, g, yconv, bias, mix_g_b, w_out, norm2, tm=512):
    t, d = x.shape
    d_a = yan.shape[1]
    d_b = x0.shape[1]
    assert t % tm == 0, t
    row = lambda w: pl.BlockSpec((tm, w), lambda i: (i, 0))
    const = lambda *shape: pl.BlockSpec(shape, lambda i: (0,) * len(shape), pipeline_mode=pl.Buffered(1))
    return pl.pallas_call(
        _postmix_kernel,
        grid=(t // tm,),
        in_specs=[row(d), row(d_a), row(d_b), row(d_b), row(d_b), const(1, d_b), const(1, d_b),
                  const(d_a + d_b, d), const(1, d)],
        out_specs=[row(d), row(d)],
        out_shape=[jax.ShapeDtypeStruct((t, d), F32), jax.ShapeDtypeStruct((t, d), BF16)],
        compiler_params=_cparams(("parallel",)),
        name="postmix",
    )(x, yan, x0, g, yconv, bias, mix_g_b, w_out, norm2)


def _staircase():
    return [(i, j) for i in range(PEER_TOPK) for j in range(PEER_TOPK) if (i + 1) * (j + 1) <= PEER_TOPK]


def _topk_kernel(xn_ref, wqt_ref, kbd_ref, e_ref, gt_ref, s_s, v_s, i_s, c_s, ce_s, b_s, es_s, *, heads):
    tm = xn_ref.shape[0]
    hk = heads * HALF_KEY
    neg = np.float32(-np.inf)
    q_t = lax.dot_general(wqt_ref[...], xn_ref[...], (((1,), (1,)), ((), ())),
                          preferred_element_type=F32).astype(BF16)
    iota_n = lax.broadcasted_iota(I32, (N_KEYS, heads, tm), 0)
    for half in range(2):
        s_t = jnp.dot(kbd_ref[half], q_t[half * hk:(half + 1) * hk], preferred_element_type=F32)
        s_s[...] = s_t.reshape(N_KEYS, heads, tm)

        def level1(r, carry, half=half):
            s = s_s[...]
            m = jnp.max(s, axis=0)
            idx = jnp.min(jnp.where(s == m[None], iota_n, N_KEYS), axis=0)
            s_s[...] = jnp.where(iota_n == idx[None], neg, s)
            v_s[half, r] = m
            i_s[half, r] = idx
            return carry

        lax.fori_loop(0, PEER_TOPK, level1, 0)

    cands = _staircase()
    for p, (i, j) in enumerate(cands):
        c_s[p] = v_s[0, i] + v_s[1, j]
        ce_s[p] = i_s[0, i] * N_KEYS + i_s[1, j]
    ncand = len(cands)
    iota_p = lax.broadcasted_iota(I32, (ncand, heads, tm), 0)

    def level2(r, carry):
        c = c_s[...]
        m = jnp.max(c, axis=0)
        pid = jnp.min(jnp.where(c == m[None], iota_p, ncand), axis=0)
        knock = iota_p == pid[None]
        es_s[r] = jnp.max(jnp.where(knock, ce_s[...], -1), axis=0)
        c_s[...] = jnp.where(knock, neg, c)
        b_s[r] = m
        return carry

    lax.fori_loop(0, PEER_TOPK, level2, 0)
    best = b_s[...]
    ex = jnp.exp(best - jnp.max(best, axis=0, keepdims=True))
    gate = ex / jnp.sum(ex, axis=0, keepdims=True)
    gt_ref[...] = gate.reshape(PEER_TOPK * heads, tm).T
    e_ref[...] = es_s[...].reshape(PEER_TOPK * heads, tm).T


def _peer_topk(xn, wqt, kbd, heads, tm=256):
    t, d = xn.shape
    nsel = PEER_TOPK * heads
    ncand = len(_staircase())
    assert t % tm == 0 and heads == V7X_SUBLANES, (t, heads)
    const = lambda *shape: pl.BlockSpec(shape, lambda i: (0,) * len(shape), pipeline_mode=pl.Buffered(1))
    kern = functools.partial(_topk_kernel, heads=heads)
    return pl.pallas_call(
        kern,
        grid=(t // tm,),
        in_specs=[pl.BlockSpec((tm, d), lambda i: (i, 0)), const(*wqt.shape), const(*kbd.shape)],
        out_specs=[pl.BlockSpec((tm, nsel), lambda i: (i, 0)), pl.BlockSpec((tm, nsel), lambda i: (i, 0))],
        out_shape=[jax.ShapeDtypeStruct((t, nsel), I32), jax.ShapeDtypeStruct((t, nsel), F32)],
        scratch_shapes=[
            pltpu.VMEM((N_KEYS, heads, tm), F32),
            pltpu.VMEM((2, PEER_TOPK, heads, tm), F32),
            pltpu.VMEM((2, PEER_TOPK, heads, tm), I32),
            pltpu.VMEM((ncand, heads, tm), F32),
            pltpu.VMEM((ncand, heads, tm), I32),
            pltpu.VMEM((PEER_TOPK, heads, tm), F32),
            pltpu.VMEM((PEER_TOPK, heads, tm), I32),
        ],
        compiler_params=_cparams(("parallel",)),
        name="peer_topk",
    )(xn, wqt, kbd)


def _matmul_kernel(a_ref, b_ref, o_ref):
    o_ref[...] = jnp.dot(a_ref[...], b_ref[...], preferred_element_type=F32).astype(o_ref.dtype)


def _peer_a(xn, down_t, tm=1024, tn=2048):
    t, d = xn.shape
    n = down_t.shape[1]
    assert t % tm == 0 and n % tn == 0, (t, n)
    return pl.pallas_call(
        _matmul_kernel,
        grid=(n // tn, t // tm),
        in_specs=[pl.BlockSpec((tm, d), lambda j, i: (i, 0)), pl.BlockSpec((d, tn), lambda j, i: (0, j))],
        out_specs=pl.BlockSpec((tm, tn), lambda j, i: (i, j)),
        out_shape=jax.ShapeDtypeStruct((t, n), F32),
        compiler_params=_cparams(("parallel", "parallel")),
        name="peer_a",
    )(xn, down_t)


PEER_B_GROUP = 16


def _peer_b_kernel(a_ref, e_ref, gt_ref, m_ref):
    c_tok = a_ref.shape[0]
    nsel = e_ref.shape[1]
    grp = PEER_B_GROUP
    sub = lax.broadcasted_iota(I32, (grp, N_KEYS, nsel), 1)

    def group(gi, carry):
        rows = pl.ds(pl.multiple_of(gi * grp, grp), grp)
        e = e_ref[rows, :]
        ai = lax.shift_right_logical(e, KEY_SHIFT)
        bi = e & (N_KEYS - 1)
        accs = [jnp.zeros((grp, nsel), F32) for _ in range(4)]
        for a in range(N_KEYS):
            blk = a_ref[rows, a * N_KEYS:(a + 1) * N_KEYS]
            accs[a % 4] = jnp.where(ai == a, jnp.take_along_axis(blk, bi, axis=1), accs[a % 4])
        picked = (accs[0] + accs[1]) + (accs[2] + accs[3])
        w = gt_ref[rows, :] * _gelu(picked)
        w1t = jnp.where(sub == ai[:, None, :], w[:, None, :], 0.0).astype(BF16)
        e2t = jnp.where(sub == bi[:, None, :], 1.0, 0.0).astype(BF16)
        m3 = jnp.einsum("cas,cbs->cab", w1t, e2t, preferred_element_type=F32)
        mt = jnp.swapaxes(m3, 0, 1).astype(m_ref.dtype)
        for a in range(N_KEYS):
            m_ref[rows, a * N_KEYS:(a + 1) * N_KEYS] = mt[a]
        return carry

    lax.fori_loop(0, c_tok // grp, group, 0)


def _peer_b(a, eidx, gate, tc=128):
    t, n = a.shape
    nsel = eidx.shape[1]
    assert t % tc == 0 and tc % PEER_B_GROUP == 0 and n == N_KEYS * N_KEYS and nsel == N_KEYS, (t, n, nsel)
    return pl.pallas_call(
        _peer_b_kernel,
        grid=(t // tc,),
        in_specs=[pl.BlockSpec((tc, n), lambda i: (i, 0)), pl.BlockSpec((tc, nsel), lambda i: (i, 0)),
                  pl.BlockSpec((tc, nsel), lambda i: (i, 0))],
        out_specs=pl.BlockSpec((tc, n), lambda i: (i, 0)),
        out_shape=jax.ShapeDtypeStruct((t, n), BF16),
        compiler_params=_cparams(("parallel",)),
        name="peer_b",
    )(a, eidx, gate)


def _peer_c_kernel(m_ref, up_ref, x1_ref, fn_ref, o_ref, acc_s):
    k = pl.program_id(1)

    @pl.when(k == 0)
    def _():
        acc_s[...] = jnp.zeros_like(acc_s)

    acc_s[...] += jnp.dot(m_ref[...], up_ref[...], preferred_element_type=F32)

    @pl.when(k == pl.num_programs(1) - 1)
    def _():
        o_ref[...] = _rms(x1_ref[...] + acc_s[...], fn_ref[...])


def _peer_c(m, up, x1, final_norm, tm=1024, tk=1024):
    t, n = m.shape
    d = up.shape[1]
    assert t % tm == 0 and n % tk == 0, (t, n)
    return pl.pallas_call(
        _peer_c_kernel,
        grid=(t // tm, n // tk),
        in_specs=[pl.BlockSpec((tm, tk), lambda i, k: (i, k)), pl.BlockSpec((tk, d), lambda i, k: (k, 0)),
                  pl.BlockSpec((tm, d), lambda i, k: (i, 0), pipeline_mode=pl.Buffered(1)),
                  pl.BlockSpec((1, d), lambda i, k: (0, 0))],
        out_specs=pl.BlockSpec((tm, d), lambda i, k: (i, 0)),
        out_shape=jax.ShapeDtypeStruct((t, d), F32),
        scratch_shapes=[pltpu.VMEM((tm, d), F32)],
        compiler_params=_cparams(("parallel", "arbitrary")),
        name="peer_c",
    )(m, up, x1, final_norm)


def _prep_weights(norm1, w_in, a_ln_g, a_ln_b, a_ws, a_bs, b_conv_w, b_conv_b, b_bias, mix_norm, w_out, norm2,
                  peer_wq, peer_k1, peer_k2, peer_down, peer_up, final_norm):
    d = w_in.shape[0]
    d_a = a_ln_g.shape[0]
    d_b = b_bias.shape[0]
    heads, n_keys, half_key = peer_k1.shape
    row = lambda v: v.reshape(1, -1).astype(F32)
    w = dict(
        norm1=row(norm1), norm2=row(norm2), final_norm=row(final_norm),
        w_u=w_in[:, :d_a].astype(BF16), w_v=w_in[:, d_a:2 * d_a].astype(BF16),
        w_b=w_in[:, 2 * d_a:].reshape(d, 3, d_b).transpose(1, 0, 2).astype(BF16),
        ln_g=row(a_ln_g), ln_b=row(a_ln_b),
        ws=a_ws.astype(BF16),
        bsb=jnp.broadcast_to(a_bs[:, :, None], a_bs.shape + (A_HEAD_DIM,)).astype(F32),
        conv_w=b_conv_w.reshape(3, 3, d_b).transpose(1, 0, 2).astype(F32),
        conv_b=b_conv_b.reshape(3, d_b).astype(F32),
        bias=row(b_bias), mix_g_a=row(mix_norm[:d_a]), mix_g_b=row(mix_norm[d_a:]),
        w_out=w_out.astype(BF16),
        down_t=peer_down.astype(BF16).T, up=peer_up.astype(BF16),
    )
    wq_t = peer_wq.T.reshape(heads, 2, half_key, d).transpose(1, 0, 2, 3).reshape(2 * heads * half_key, d)
    w["wq_t"] = wq_t.astype(BF16)
    eye = jnp.eye(heads, dtype=F32)
    kbd = [jnp.einsum("hnd,hg->nhgd", k, eye).reshape(n_keys * heads, heads * half_key) for k in (peer_k1, peer_k2)]
    w["kbd"] = jnp.stack(kbd).astype(BF16)
    w["heads"] = heads
    return w


def _trunk(x3, w, hf):
    bsz, seq_len, d = x3.shape
    t = bsz * seq_len
    x = x3.reshape(t, d)
    yan, x0, g = _inproj(x, seq_len, w["norm1"], w["w_u"], w["w_v"], w["w_b"], w["ln_g"], w["ln_b"],
                         w["ws"], w["bsb"], w["conv_w"], w["conv_b"], w["mix_g_a"])
    d_b = x0.shape[1]
    yconv = _hyena_long_conv(g.reshape(bsz, seq_len, d_b), *hf).reshape(t, d_b)
    x1, xn = _postmix(x, yan, x0, g, yconv, w["bias"], w["mix_g_b"], w["w_out"], w["norm2"])
    eidx, gate = _peer_topk(xn, w["wq_t"], w["kbd"], w["heads"])
    a = _peer_a(xn, w["down_t"])
    m = _peer_b(a, eidx, gate)
    out = _peer_c(m, w["up"], x1, w["final_norm"])
    return out.reshape(bsz, seq_len, d)


def kernel(x_prompt, x_sample, norm1, w_in, a_ln_g, a_ln_b, a_ws, a_bs, b_conv_w, b_conv_b, hf_w1, hf_b1, hf_w2, hf_b2, hf_w3, hf_b3, hf_freq, hf_w4, b_bias, mix_norm, w_out, norm2, peer_wq, peer_k1, peer_k2, peer_down, peer_up, final_norm):
    assert norm1.shape[0] == 1, "single-layer trunk"
    w = _prep_weights(norm1[0], w_in[0], a_ln_g[0], a_ln_b[0], a_ws[0], a_bs[0], b_conv_w[0], b_conv_b[0],
                      b_bias[0], mix_norm[0], w_out[0], norm2[0], peer_wq[0], peer_k1[0], peer_k2[0],
                      peer_down[0], peer_up[0], final_norm)
    hf = (hf_w1[0], hf_b1[0].reshape(1, -1), hf_w2[0], hf_b2[0].reshape(1, -1), hf_w3[0], hf_b3[0].reshape(1, -1),
          hf_freq[0], hf_w4[0])
    return (_trunk(x_prompt, w, hf), _trunk(x_sample, w, hf))
```

```python
import functools
import math

import numpy as np
import jax
import jax.numpy as jnp
from jax import lax
from jax.experimental import pallas as pl
from jax.experimental.pallas import tpu as pltpu

F32 = jnp.float32
BF16 = jnp.bfloat16
I32 = jnp.int32

EPS = 1e-6
V7X_LANES = 128
V7X_SUBLANES = 8
V7X_VMEM_BYTES = 64 * 1024 * 1024
VMEM_LIMIT = V7X_VMEM_BYTES - 8 * 1024 * 1024

CHUNK = 128
A_HEAD_DIM = 128
EMB_DIM = 33
BANDS = (EMB_DIM - 1) // 2
DECAY_TARGET = 1e-2
FAST_DECAY_PCT = 0.3
SLOW_DECAY_PCT = 1.5
FFT_N1 = 64
FFT_N2 = 128
FFT_N = FFT_N1 * FFT_N2
CONV_BLOCK = FFT_N // 2
N_KEYS = 128
KEY_SHIFT = 7
PEER_TOPK = 16
HALF_KEY = 128


def _cparams(sem):
    return pltpu.CompilerParams(dimension_semantics=sem, vmem_limit_bytes=VMEM_LIMIT)


def _rms(xf, g):
    return xf * lax.rsqrt(jnp.mean(xf * xf, axis=-1, keepdims=True) + EPS) * g


def _gelu(x):
    return 0.5 * x * (1.0 + lax.erf(x * np.float32(math.sqrt(0.5))))


def _inproj_kernel(xp_ref, x_ref, xn_ref, n1_ref, wu_ref, wv_ref, wb_ref, lng_ref, lnb_ref,
                   ws_ref, bsb_ref, cw_ref, cb_ref, mga_ref,
                   yan_ref, x0_ref, g_ref, ya_s, zs_s, *, tiles_per_seq, cblk):
    tm = x_ref.shape[0]
    d_a = wu_ref.shape[1]
    d_b = wb_ref.shape[2]
    i = pl.program_id(0)
    not_first = (i % tiles_per_seq != 0).astype(F32)
    not_last = (i % tiles_per_seq != tiles_per_seq - 1).astype(F32)

    xcat = jnp.concatenate([xp_ref[...], x_ref[...], xn_ref[...]], axis=0)
    hcat = _rms(xcat, n1_ref[...])
    h_all = hcat.astype(BF16)
    h = hcat[V7X_SUBLANES:V7X_SUBLANES + tm].astype(BF16)

    u = _gelu(jnp.dot(h, wu_ref[...], preferred_element_type=F32))
    v = _gelu(jnp.dot(h, wv_ref[...], preferred_element_type=F32))
    mu = jnp.mean(v, axis=-1, keepdims=True)
    vc = v - mu
    var = jnp.mean(vc * vc, axis=-1, keepdims=True)
    vb = (vc * lax.rsqrt(var + EPS) * lng_ref[...] + lnb_ref[...]).astype(BF16)
    for c in range(tm // CHUNK):
        rows = slice(c * CHUNK, (c + 1) * CHUNK)
        for hd in range(d_a // A_HEAD_DIM):
            cols = slice(hd * A_HEAD_DIM, (hd + 1) * A_HEAD_DIM)
            mixed = jnp.dot(ws_ref[hd], vb[rows, cols], preferred_element_type=F32) + bsb_ref[hd]
            ya_s[rows, cols] = u[rows, cols] * mixed
    yan_ref[...] = _rms(ya_s[...], mga_ref[...]).astype(yan_ref.dtype)

    halo = V7X_SUBLANES
    for cb in range(d_b // cblk):
        cols = slice(cb * cblk, (cb + 1) * cblk)
        parts = []
        for p in range(3):
            z = jnp.dot(h_all, wb_ref[p, :, cols], preferred_element_type=F32)
            zs_s[...] = z
            zs_s[0:halo, :] = z[0:halo] * not_first
            zs_s[tm + halo:tm + 2 * halo, :] = z[tm + halo:tm + 2 * halo] * not_last
            w = cw_ref[p]
            zc = (cb_ref[p:p + 1, cols]
                  + zs_s[halo - 1:halo - 1 + tm, :] * w[0:1, cols]
                  + zs_s[halo:halo + tm, :] * w[1:2, cols]
                  + zs_s[halo + 1:halo + 1 + tm, :] * w[2:3, cols])
            parts.append(zc)
        x0_ref[:, cols] = parts[0]
        g_ref[:, cols] = parts[1] * parts[2]


def _inproj(x, seq_len, norm1, w_u, w_v, w_b, ln_g, ln_b, ws, bsb, conv_w, conv_b, mix_g_a, tm=512, cblk=256):
    t, d = x.shape
    d_a = w_u.shape[1]
    d_b = w_b.shape[2]
    assert seq_len % tm == 0 and t % seq_len == 0 and tm % CHUNK == 0 and d_b % cblk == 0, (t, seq_len)
    nblk = tm // V7X_SUBLANES
    last8 = t // V7X_SUBLANES - 1
    const = lambda *shape: pl.BlockSpec(shape, lambda i: (0,) * len(shape), pipeline_mode=pl.Buffered(1))
    kern = functools.partial(_inproj_kernel, tiles_per_seq=seq_len // tm, cblk=cblk)
    return pl.pallas_call(
        kern,
        grid=(t // tm,),
        in_specs=[
            pl.BlockSpec((V7X_SUBLANES, d), lambda i: (jnp.maximum(i * nblk - 1, 0), 0)),
            pl.BlockSpec((tm, d), lambda i: (i, 0)),
            pl.BlockSpec((V7X_SUBLANES, d), lambda i: (jnp.minimum((i + 1) * nblk, last8), 0)),
            const(1, d), const(d, d_a), const(d, d_a), const(3, d, d_b), const(1, d_a), const(1, d_a),
            const(*ws.shape), const(*bsb.shape), const(3, 3, d_b), const(3, d_b), const(1, d_a),
        ],
        out_specs=[
            pl.BlockSpec((tm, d_a), lambda i: (i, 0)),
            pl.BlockSpec((tm, d_b), lambda i: (i, 0)),
            pl.BlockSpec((tm, d_b), lambda i: (i, 0)),
        ],
        out_shape=[
            jax.ShapeDtypeStruct((t, d_a), BF16),
            jax.ShapeDtypeStruct((t, d_b), F32),
            jax.ShapeDtypeStruct((t, d_b), F32),
        ],
        scratch_shapes=[pltpu.VMEM((tm, d_a), F32), pltpu.VMEM((tm + 2 * V7X_SUBLANES, cblk), F32)],
        compiler_params=_cparams(("parallel",)),
        name="inproj",
    )(x, x, x, norm1, w_u, w_v, w_b, ln_g, ln_b, ws, bsb, conv_w, conv_b, mix_g_a)


def _filt_kernel(fr_ref, dl_ref, w1_ref, b1_ref, w2_ref, b2_ref, w3_ref, b3_ref, fq_ref, w4_ref, o_ref,
                 *, seq_len, nblocks, tr):
    e = pl.program_id(0) - nblocks
    r = pl.program_id(1)
    trow = r * tr + lax.broadcasted_iota(I32, (tr, 1), 0)
    lag = e * CONV_BLOCK + trow
    pos = jnp.abs(lag).astype(F32)
    valid = jnp.abs(lag) <= seq_len - 1
    t = pos / np.float32(max(seq_len - 1, 1))
    fw = np.float32(2.0 * math.pi / seq_len) * pos * fr_ref[...]
    hi = lax.Precision.HIGHEST
    w1 = w1_ref[...]
    z1 = (t * w1[0:1, :]
          + jnp.dot(jnp.cos(fw), w1[1:1 + BANDS, :], precision=hi, preferred_element_type=F32)
          + jnp.dot(-jnp.sin(fw), w1[1 + BANDS:, :], precision=hi, preferred_element_type=F32))
    fq = fq_ref[...]
    h = jnp.sin(fq[0:1, :] * (z1 + b1_ref[...]))
    h = jnp.sin(fq[1:2, :] * (jnp.dot(h, w2_ref[...], precision=hi, preferred_element_type=F32) + b2_ref[...]))
    h = jnp.sin(fq[2:3, :] * (jnp.dot(h, w3_ref[...], precision=hi, preferred_element_type=F32) + b3_ref[...]))
    h4 = jnp.dot(h, w4_ref[...], precision=hi, preferred_element_type=F32)
    window = jnp.exp(-t * dl_ref[...])
    o_ref[...] = jnp.where(valid, h4 * window, 0.0)


def _hyena_filter_blocks(seq_len, w1, b1, w2, b2, w3, b3, freq, w4, tr=512):
    nblocks = seq_len // CONV_BLOCK
    width = w1.shape[1]
    d_b = w4.shape[1] // 2
    fr = jnp.asarray(np.linspace(1e-4, BANDS - 1, BANDS, dtype=np.float32)[None, :])
    min_decay = math.log(DECAY_TARGET) / SLOW_DECAY_PCT
    max_decay = math.log(DECAY_TARGET) / FAST_DECAY_PCT
    deltas = jnp.asarray(np.abs(np.linspace(min_decay, max_decay, d_b, dtype=np.float32))[None, :])
    w4h = w4.reshape(width, 2, d_b).transpose(1, 0, 2)
    const = lambda *shape: pl.BlockSpec(shape, lambda a, b: (0,) * len(shape))
    kern = functools.partial(_filt_kernel, seq_len=seq_len, nblocks=nblocks, tr=tr)
    return pl.pallas_call(
        kern,
        grid=(2 * nblocks, CONV_BLOCK // tr),
        in_specs=[
            const(1, BANDS), const(1, d_b), const(EMB_DIM, width), const(1, width), const(width, width),
            const(1, width), const(width, width), const(1, width), const(3, width),
            pl.BlockSpec((None, width, d_b), lambda ei, r: (jnp.where(ei >= nblocks, 0, 1), 0, 0)),
        ],
        out_specs=pl.BlockSpec((None, tr, d_b), lambda ei, r: (ei, r, 0)),
        out_shape=jax.ShapeDtypeStruct((2 * nblocks, CONV_BLOCK, d_b), F32),
        compiler_params=_cparams(("parallel", "parallel")),
        name="hyena_filter",
    )(fr, deltas, w1, b1, w2, b2, w3, b3, freq, w4h)


def _dft_tables():
    n1 = np.arange(FFT_N1)
    f64 = np.exp(-2j * np.pi * np.outer(n1, n1) / FFT_N1)
    half = FFT_N1 // 2
    fr, fi = f64.real, f64.imag
    lhs_data = np.block([[fr[:, :half], -fi[:, :half]], [fi[:, :half], fr[:, :half]]])
    lhs_real = np.concatenate([fr, fi], axis=0)
    gr, gi = fr[:half, :] / FFT_N, -fi[:half, :] / FFT_N
    lhs_inv = np.block([[gr, -gi], [gi, gr]])
    k1 = np.arange(FFT_N1)[:, None, None]
    k2 = np.arange(FFT_N2)[None, :, None]
    n2 = np.arange(FFT_N2)[None, None, :]
    g = np.exp(-2j * np.pi * (n2 * (k1 + FFT_N1 * k2) % FFT_N) / FFT_N)
    gfwd = np.concatenate([np.concatenate([g.real, -g.imag], axis=2),
                           np.concatenate([g.imag, g.real], axis=2)], axis=1)
    ht = np.conj(np.transpose(g, (0, 2, 1)))
    ginv = np.concatenate([np.concatenate([ht.real, -ht.imag], axis=2),
                           np.concatenate([ht.imag, ht.real], axis=2)], axis=1)
    as32 = lambda a: np.asarray(a, dtype=np.float32)
    return as32(lhs_data), as32(lhs_real), as32(lhs_inv), as32(gfwd), as32(ginv)


def _split(x):
    hi = x.astype(BF16)
    lo = (x - hi.astype(F32)).astype(BF16)
    return hi, lo


def _dot3(a_hi, a_lo, b):
    b_hi, b_lo = _split(b)
    acc = jnp.dot(a_hi, b_hi, preferred_element_type=F32)
    acc = acc + jnp.dot(a_hi, b_lo, preferred_element_type=F32)
    acc = acc + jnp.dot(a_lo, b_hi, preferred_element_type=F32)
    return acc


def _fft1_kernel(lh_ref, ll_ref, a_ref, b_ref, o_ref, *, zero_lanes):
    b = b_ref[...]
    if zero_lanes:
        row = lax.broadcasted_iota(I32, b.shape, 0)
        lane = lax.broadcasted_iota(I32, b.shape, 1) + pl.program_id(2) * b.shape[1]
        b = jnp.where(jnp.logical_and(row == 0, lane < zero_lanes), 0.0, b)
    rhs = jnp.concatenate([a_ref[...], b], axis=0)
    o_ref[...] = _dot3(lh_ref[...], ll_ref[...], rhs)


def _fft1(lhs, x4, a_of, b_of, g, p, zero_lanes=0, tn=8192):
    rows, nl = x4.shape[2:]
    lh, ll = _split(jnp.asarray(lhs))
    kern = functools.partial(_fft1_kernel, zero_lanes=zero_lanes)
    return pl.pallas_call(
        kern,
        grid=(g, p, nl // tn),
        in_specs=[
            pl.BlockSpec(lhs.shape, lambda q, j, t: (0, 0)),
            pl.BlockSpec(lhs.shape, lambda q, j, t: (0, 0)),
            pl.BlockSpec((None, None, rows, tn), lambda q, j, t: a_of(q, j) + (0, t)),
            pl.BlockSpec((None, None, rows, tn), lambda q, j, t: b_of(q, j) + (0, t)),
        ],
        out_specs=pl.BlockSpec((None, None, 2 * FFT_N1, tn), lambda q, j, t: (q, j, 0, t)),
        out_shape=jax.ShapeDtypeStruct((g, p, 2 * FFT_N1, nl), F32),
        compiler_params=_cparams(("parallel", "parallel", "parallel")),
        name="fft1",
    )(lh, ll, x4, x4)


def _fft3_kernel(lh_ref, ll_ref, w_ref, o_ref):
    half = FFT_N1 // 2
    res = _dot3(lh_ref[...], ll_ref[...], w_ref[...])
    o_ref[0] = res[:half]
    o_ref[1] = res[half:]


def _fft3(lhs, w4d, tn=8192):
    g, p, rows, nl = w4d.shape
    half = FFT_N1 // 2
    lh, ll = _split(jnp.asarray(lhs))
    return pl.pallas_call(
        _fft3_kernel,
        grid=(g, p, nl // tn),
        in_specs=[
            pl.BlockSpec(lhs.shape, lambda q, j, t: (0, 0)),
            pl.BlockSpec(lhs.shape, lambda q, j, t: (0, 0)),
            pl.BlockSpec((None, None, rows, tn), lambda q, j, t: (q, j, 0, t)),
        ],
        out_specs=pl.BlockSpec((None, 2, None, half, tn), lambda q, j, t: (q, 0, j, 0, t)),
        out_shape=jax.ShapeDtypeStruct((g, 2, p, half, nl), F32),
        compiler_params=_cparams(("parallel", "parallel", "parallel")),
        name="fft3",
    )(lh, ll, w4d)


def _fft2_filter_kernel(gh_ref, gl_ref, x_ref, o_ref):
    ct = x_ref.shape[-1]
    for k in range(x_ref.shape[1]):
        z = _dot3(gh_ref[k], gl_ref[k], x_ref[:, k].reshape(2 * FFT_N2, ct))
        o_ref[:, k] = z.reshape(2, FFT_N2, ct)


def _fft2_filter(gfwd_hl, x1f, ct=256, kb=8):
    nd, _, _, _, d_b = x1f.shape
    gh, gl = gfwd_hl
    gspec = pl.BlockSpec((kb, 2 * FFT_N2, 2 * FFT_N2), lambda k, d, c: (k, 0, 0))
    xspec = pl.BlockSpec((None, 2, kb, FFT_N2, ct), lambda k, d, c: (d, 0, k, 0, c))
    return pl.pallas_call(
        _fft2_filter_kernel,
        grid=(FFT_N1 // kb, nd, d_b // ct),
        in_specs=[gspec, gspec, xspec],
        out_specs=xspec,
        out_shape=jax.ShapeDtypeStruct(x1f.shape, F32),
        compiler_params=_cparams(("parallel", "parallel", "parallel")),
        name="fft2_filter",
    )(gh, gl, x1f)


def _fft2_mix_kernel(gh_ref, gl_ref, ih_ref, il_ref, x_ref, k_ref, o_ref, *, nblocks):
    ct = x_ref.shape[-1]
    for k in range(x_ref.shape[2]):
        zs = []
        for j in range(nblocks):
            z = _dot3(gh_ref[k], gl_ref[k], x_ref[j, :, k].reshape(2 * FFT_N2, ct))
            zs.append((z[:FFT_N2], z[FFT_N2:]))
        for i in range(nblocks):
            yr = jnp.zeros((FFT_N2, ct), F32)
            yi = jnp.zeros((FFT_N2, ct), F32)
            for j in range(nblocks):
                d = i - j + nblocks - 1
                kr, ki = k_ref[d, 0, k], k_ref[d, 1, k]
                zr, zi = zs[j]
                yr = yr + kr * zr - ki * zi
                yi = yi + kr * zi + ki * zr
            w = _dot3(ih_ref[k], il_ref[k], jnp.concatenate([yr, yi], axis=0))
            o_ref[i, :, k] = w.reshape(2, FFT_N2, ct)


def _fft2_mix(gfwd_hl, ginv_hl, x1, kspec, ct=256):
    g, p, _, _, _, d_b = x1.shape
    nd = kspec.shape[0]
    kb = max(1, 8 // p)
    gspec = pl.BlockSpec((kb, 2 * FFT_N2, 2 * FFT_N2), lambda k, q, c: (k, 0, 0))
    xspec = pl.BlockSpec((None, p, 2, kb, FFT_N2, ct), lambda k, q, c: (q, 0, 0, k, 0, c))
    kern = functools.partial(_fft2_mix_kernel, nblocks=p)
    return pl.pallas_call(
        kern,
        grid=(FFT_N1 // kb, g, d_b // ct),
        in_specs=[gspec, gspec, gspec, gspec, xspec,
                  pl.BlockSpec((nd, 2, kb, FFT_N2, ct), lambda k, q, c: (0, 0, k, 0, c))],
        out_specs=xspec,
        out_shape=jax.ShapeDtypeStruct(x1.shape, F32),
        compiler_params=_cparams(("parallel", "parallel", "parallel")),
        name="fft2_mix",
    )(gfwd_hl[0], gfwd_hl[1], ginv_hl[0], ginv_hl[1], x1, kspec)


def _hyena_long_conv(g3, w1, b1, w2, b2, w3, b3, freq, w4):
    bsz, seq_len, d_b = g3.shape
    nblocks = seq_len // CONV_BLOCK
    nl = FFT_N2 * d_b
    half = FFT_N1 // 2
    lhs_data, lhs_real, lhs_inv, gfwd, ginv = _dft_tables()
    gfwd_hl = _split(jnp.asarray(gfwd))
    ginv_hl = _split(jnp.asarray(ginv))
    kext = _hyena_filter_blocks(seq_len, w1, b1, w2, b2, w3, b3, freq, w4)
    nd = 2 * nblocks - 1
    k1f = _fft1(lhs_real, kext.reshape(1, 2 * nblocks, half, nl), lambda q, j: (0, q + 1), lambda q, j: (0, q),
                nd, 1, zero_lanes=d_b)
    kspec = _fft2_filter(gfwd_hl, k1f.reshape(nd, 2, FFT_N1, FFT_N2, d_b))
    x1 = _fft1(lhs_data, g3.reshape(bsz, nblocks, half, nl), lambda q, j: (2 * q, j), lambda q, j: (2 * q + 1, j),
               bsz // 2, nblocks)
    wmix = _fft2_mix(gfwd_hl, ginv_hl, x1.reshape(bsz // 2, nblocks, 2, FFT_N1, FFT_N2, d_b), kspec)
    y5 = _fft3(lhs_inv, wmix.reshape(bsz // 2, nblocks, 2 * FFT_N1, nl))
    return y5.reshape(bsz, seq_len, d_b)


def _postmix_kernel(x_ref, yan_ref, x0_ref, g_ref, yc_ref, bias_ref, mgb_ref, wo_ref, n2_ref, x1_ref, xn_ref):
    d_a = yan_ref.shape[1]
    gg = g_ref[...]
    yb = x0_ref[...] * (yc_ref[...] + gg * bias_ref[...])
    ybn = _rms(yb, mgb_ref[...]).astype(BF16)
    y = jnp.dot(yan_ref[...], wo_ref[0:d_a, :], preferred_element_type=F32)
    y = y + jnp.dot(ybn, wo_ref[d_a:, :], preferred_element_type=F32)
    x1 = x_ref[...] + y
    x1_ref[...] = x1
    xn_ref[...] = _rms(x1, n2_ref[...]).astype(xn_ref.dtype)


def _postmix(x, yan, x0, g, yconv, bias, mix_g_b, w_out, norm2, tm=512):
    t, d = x.shape
    d_a = yan.shape[1]
    d_b = x0.shape[1]
    assert t % tm == 0, t
    row = lambda w: pl.BlockSpec((tm, w), lambda i: (i, 0))
    const = lambda *shape: pl.BlockSpec(shape, lambda i: (0,) * len(shape), pipeline_mode=pl.Buffered(1))
    return pl.pallas_call(
        _postmix_kernel,
        grid=(t // tm,),
        in_specs=[row(d), row(d_a), row(d_b), row(d_b), row(d_b), const(1, d_b), const(1, d_b),
                  const(d_a + d_b, d), const(1, d)],
        out_specs=[row(d), row(d)],
        out_shape=[jax.ShapeDtypeStruct((t, d), F32), jax.ShapeDtypeStruct((t, d), BF16)],
        compiler_params=_cparams(("parallel",)),
        name="postmix",
    )(x, yan, x0, g, yconv, bias, mix_g_b, w_out, norm2)


def _staircase():
    return [(i, j) for i in range(PEER_TOPK) for j in range(PEER_TOPK) if (i + 1) * (j + 1) <= PEER_TOPK]


ARGMAX_LANES = 4


def _stream_argmax(val_ref, tag_ref, nrows, prev, slab):
    neg = np.float32(-np.inf)
    best_v = [jnp.full(slab, neg, F32) for _ in range(ARGMAX_LANES)]
    best_i = [jnp.zeros(slab, I32) for _ in range(ARGMAX_LANES)]
    best_t = [jnp.zeros(slab, I32) for _ in range(ARGMAX_LANES)]
    for n in range(nrows):
        v = jnp.where(prev == n, neg, val_ref[n])
        val_ref[n] = v
        k = n % ARGMAX_LANES
        better = v > best_v[k]
        best_v[k] = jnp.where(better, v, best_v[k])
        best_i[k] = jnp.where(better, n, best_i[k])
        if tag_ref is not None:
            best_t[k] = jnp.where(better, tag_ref[n], best_t[k])
    width = ARGMAX_LANES
    while width > 1:
        width //= 2
        for k in range(width):
            va, vb = best_v[k], best_v[k + width]
            ia, ib = best_i[k], best_i[k + width]
            take_b = jnp.logical_or(vb > va, jnp.logical_and(vb == va, ib < ia))
            best_v[k] = jnp.where(take_b, vb, va)
            best_i[k] = jnp.where(take_b, ib, ia)
            best_t[k] = jnp.where(take_b, best_t[k + width], best_t[k])
    return best_v[0], best_i[0], best_t[0]


def _topk_kernel(xn_ref, wqt_ref, kbd_ref, e_ref, gt_ref, s_s, v_s, i_s, c_s, ce_s, b_s, es_s, *, heads):
    tm = xn_ref.shape[0]
    hk = heads * HALF_KEY
    neg = np.float32(-np.inf)
    q_t = lax.dot_general(wqt_ref[...], xn_ref[...], (((1,), (1,)), ((), ())),
                          preferred_element_type=F32).astype(BF16)
    slab = (heads, tm)
    no_pick = jnp.full(slab, -1, I32)
    for half in range(2):
        s_t = jnp.dot(kbd_ref[half], q_t[half * hk:(half + 1) * hk], preferred_element_type=F32)
        s_s[...] = s_t.reshape(N_KEYS, heads, tm)

        def level1(r, prev, half=half):
            m, idx, _ = _stream_argmax(s_s, None, N_KEYS, prev, slab)
            v_s[half, r] = m
            i_s[half, r] = idx
            return idx

        lax.fori_loop(0, PEER_TOPK, level1, no_pick)

    cands = _staircase()
    for p, (i, j) in enumerate(cands):
        c_s[p] = v_s[0, i] + v_s[1, j]
        ce_s[p] = i_s[0, i] * N_KEYS + i_s[1, j]

    def level2(r, prev):
        m, pid, e = _stream_argmax(c_s, ce_s, len(cands), prev, slab)
        es_s[r] = e
        b_s[r] = m
        return pid

    lax.fori_loop(0, PEER_TOPK, level2, no_pick)
    best = b_s[...]
    ex = jnp.exp(best - jnp.max(best, axis=0, keepdims=True))
    gate = ex / jnp.sum(ex, axis=0, keepdims=True)
    gt_ref[...] = gate.reshape(PEER_TOPK * heads, tm).T
    e_ref[...] = es_s[...].reshape(PEER_TOPK * heads, tm).T


def _peer_topk(xn, wqt, kbd, heads, tm=512):
    t, d = xn.shape
    nsel = PEER_TOPK * heads
    ncand = len(_staircase())
    assert t % tm == 0 and heads == V7X_SUBLANES, (t, heads)
    const = lambda *shape: pl.BlockSpec(shape, lambda i: (0,) * len(shape), pipeline_mode=pl.Buffered(1))
    kern = functools.partial(_topk_kernel, heads=heads)
    return pl.pallas_call(
        kern,
        grid=(t // tm,),
        in_specs=[pl.BlockSpec((tm, d), lambda i: (i, 0)), const(*wqt.shape), const(*kbd.shape)],
        out_specs=[pl.BlockSpec((tm, nsel), lambda i: (i, 0)), pl.BlockSpec((tm, nsel), lambda i: (i, 0))],
        out_shape=[jax.ShapeDtypeStruct((t, nsel), I32), jax.ShapeDtypeStruct((t, nsel), F32)],
        scratch_shapes=[
            pltpu.VMEM((N_KEYS, heads, tm), F32),
            pltpu.VMEM((2, PEER_TOPK, heads, tm), F32),
            pltpu.VMEM((2, PEER_TOPK, heads, tm), I32),
            pltpu.VMEM((ncand, heads, tm), F32),
            pltpu.VMEM((ncand, heads, tm), I32),
            pltpu.VMEM((PEER_TOPK, heads, tm), F32),
            pltpu.VMEM((PEER_TOPK, heads, tm), I32),
        ],
        compiler_params=_cparams(("parallel",)),
        name="peer_topk",
    )(xn, wqt, kbd)


def _matmul_kernel(a_ref, b_ref, o_ref):
    o_ref[...] = jnp.dot(a_ref[...], b_ref[...], preferred_element_type=F32).astype(o_ref.dtype)


def _peer_a(xn, down_t, tm=1024, tn=2048):
    t, d = xn.shape
    n = down_t.shape[1]
    assert t % tm == 0 and n % tn == 0, (t, n)
    return pl.pallas_call(
        _matmul_kernel,
        grid=(n // tn, t // tm),
        in_specs=[pl.BlockSpec((tm, d), lambda j, i: (i, 0)), pl.BlockSpec((d, tn), lambda j, i: (0, j))],
        out_specs=pl.BlockSpec((tm, tn), lambda j, i: (i, j)),
        out_shape=jax.ShapeDtypeStruct((t, n), F32),
        compiler_params=_cparams(("parallel", "parallel")),
        name="peer_a",
    )(xn, down_t)


PEER_B_GROUP = 16


def _peer_b_kernel(a_ref, e_ref, gt_ref, m_ref):
    c_tok = a_ref.shape[0]
    nsel = e_ref.shape[1]
    grp = PEER_B_GROUP
    ngrp = c_tok // grp
    sub = lax.broadcasted_iota(I32, (grp, N_KEYS, nsel), 1)

    def rows_of(gi):
        return pl.ds(pl.multiple_of(gi * grp, grp), grp)

    def pick(gi):
        rows = rows_of(gi)
        e = e_ref[rows, :]
        ai = lax.shift_right_logical(e, KEY_SHIFT)
        bi = e & (N_KEYS - 1)
        accs = [jnp.zeros((grp, nsel), F32) for _ in range(4)]
        for a in range(N_KEYS):
            blk = a_ref[rows, a * N_KEYS:(a + 1) * N_KEYS]
            accs[a % 4] = jnp.where(ai == a, jnp.take_along_axis(blk, bi, axis=1), accs[a % 4])
        picked = (accs[0] + accs[1]) + (accs[2] + accs[3])
        return gt_ref[rows, :] * _gelu(picked), e

    def scatter(gi, w, e):
        ai = lax.shift_right_logical(e, KEY_SHIFT)
        bi = e & (N_KEYS - 1)
        w1t = jnp.where(sub == ai[:, None, :], w[:, None, :], 0.0).astype(BF16)
        e2t = jnp.where(sub == bi[:, None, :], 1.0, 0.0).astype(BF16)
        m3 = jnp.einsum("cas,cbs->cab", w1t, e2t, preferred_element_type=F32)
        mt = jnp.swapaxes(m3, 0, 1).astype(m_ref.dtype)
        rows = rows_of(gi)
        for a in range(N_KEYS):
            m_ref[rows, a * N_KEYS:(a + 1) * N_KEYS] = mt[a]

    def step(gi, carry):
        nxt = pick(gi)
        scatter(gi - 1, *carry)
        return nxt

    last = lax.fori_loop(1, ngrp, step, pick(0))
    scatter(ngrp - 1, *last)


def _peer_b(a, eidx, gate, tc=256):
    t, n = a.shape
    nsel = eidx.shape[1]
    assert t % tc == 0 and tc % PEER_B_GROUP == 0 and n == N_KEYS * N_KEYS and nsel == N_KEYS, (t, n, nsel)
    return pl.pallas_call(
        _peer_b_kernel,
        grid=(t // tc,),
        in_specs=[pl.BlockSpec((tc, n), lambda i: (i, 0)), pl.BlockSpec((tc, nsel), lambda i: (i, 0)),
                  pl.BlockSpec((tc, nsel), lambda i: (i, 0))],
        out_specs=pl.BlockSpec((tc, n), lambda i: (i, 0)),
        out_shape=jax.ShapeDtypeStruct((t, n), BF16),
        compiler_params=_cparams(("parallel",)),
        name="peer_b",
    )(a, eidx, gate)


def _peer_c_kernel(m_ref, up_ref, x1_ref, fn_ref, o_ref, acc_s):
    k = pl.program_id(1)

    @pl.when(k == 0)
    def _():
        acc_s[...] = jnp.zeros_like(acc_s)

    acc_s[...] += jnp.dot(m_ref[...], up_ref[...], preferred_element_type=F32)

    @pl.when(k == pl.num_programs(1) - 1)
    def _():
        o_ref[...] = _rms(x1_ref[...] + acc_s[...], fn_ref[...])


def _peer_c(m, up, x1, final_norm, tm=1024, tk=1024):
    t, n = m.shape
    d = up.shape[1]
    assert t % tm == 0 and n % tk == 0, (t, n)
    return pl.pallas_call(
        _peer_c_kernel,
        grid=(t // tm, n // tk),
        in_specs=[pl.BlockSpec((tm, tk), lambda i, k: (i, k)), pl.BlockSpec((tk, d), lambda i, k: (k, 0)),
                  pl.BlockSpec((tm, d), lambda i, k: (i, 0), pipeline_mode=pl.Buffered(1)),
                  pl.BlockSpec((1, d), lambda i, k: (0, 0))],
        out_specs=pl.BlockSpec((tm, d), lambda i, k: (i, 0)),
        out_shape=jax.ShapeDtypeStruct((t, d), F32),
        scratch_shapes=[pltpu.VMEM((tm, d), F32)],
        compiler_params=_cparams(("parallel", "arbitrary")),
        name="peer_c",
    )(m, up, x1, final_norm)


def _prep_weights(norm1, w_in, a_ln_g, a_ln_b, a_ws, a_bs, b_conv_w, b_conv_b, b_bias, mix_norm, w_out, norm2,
                  peer_wq, peer_k1, peer_k2, peer_down, peer_up, final_norm):
    d = w_in.shape[0]
    d_a = a_ln_g.shape[0]
    d_b = b_bias.shape[0]
    heads, n_keys, half_key = peer_k1.shape
    row = lambda v: v.reshape(1, -1).astype(F32)
    w = dict(
        norm1=row(norm1), norm2=row(norm2), final_norm=row(final_norm),
        w_u=w_in[:, :d_a].astype(BF16), w_v=w_in[:, d_a:2 * d_a].astype(BF16),
        w_b=w_in[:, 2 * d_a:].reshape(d, 3, d_b).transpose(1, 0, 2).astype(BF16),
        ln_g=row(a_ln_g), ln_b=row(a_ln_b),
        ws=a_ws.astype(BF16),
        bsb=jnp.broadcast_to(a_bs[:, :, None], a_bs.shape + (A_HEAD_DIM,)).astype(F32),
        conv_w=b_conv_w.reshape(3, 3, d_b).transpose(1, 0, 2).astype(F32),
        conv_b=b_conv_b.reshape(3, d_b).astype(F32),
        bias=row(b_bias), mix_g_a=row(mix_norm[:d_a]), mix_g_b=row(mix_norm[d_a:]),
        w_out=w_out.astype(BF16),
        down_t=peer_down.astype(BF16).T, up=peer_up.astype(BF16),
    )
    wq_t = peer_wq.T.reshape(heads, 2, half_key, d).transpose(1, 0, 2, 3).reshape(2 * heads * half_key, d)
    w["wq_t"] = wq_t.astype(BF16)
    eye = jnp.eye(heads, dtype=F32)
    kbd = [jnp.einsum("hnd,hg->nhgd", k, eye).reshape(n_keys * heads, heads * half_key) for k in (peer_k1, peer_k2)]
    w["kbd"] = jnp.stack(kbd).astype(BF16)
    w["heads"] = heads
    return w


def _trunk(x3, w, hf):
    bsz, seq_len, d = x3.shape
    t = bsz * seq_len
    x = x3.reshape(t, d)
    yan, x0, g = _inproj(x, seq_len, w["norm1"], w["w_u"], w["w_v"], w["w_b"], w["ln_g"], w["ln_b"],
                         w["ws"], w["bsb"], w["conv_w"], w["conv_b"], w["mix_g_a"])
    d_b = x0.shape[1]
    yconv = _hyena_long_conv(g.reshape(bsz, seq_len, d_b), *hf).reshape(t, d_b)
    x1, xn = _postmix(x, yan, x0, g, yconv, w["bias"], w["mix_g_b"], w["w_out"], w["norm2"])
    eidx, gate = _peer_topk(xn, w["wq_t"], w["kbd"], w["heads"])
    a = _peer_a(xn, w["down_t"])
    m = _peer_b(a, eidx, gate)
    out = _peer_c(m, w["up"], x1, w["final_norm"])
    return out.reshape(bsz, seq_len, d)


def kernel(x_prompt, x_sample, norm1, w_in, a_ln_g, a_ln_b, a_ws, a_bs, b_conv_w, b_conv_b, hf_w1, hf_b1, hf_w2, hf_b2, hf_w3, hf_b3, hf_freq, hf_w4, b_bias, mix_norm, w_out, norm2, peer_wq, peer_k1, peer_k2, peer_down, peer_up, final_norm):
    assert norm1.shape[0] == 1, "single-layer trunk"
    w = _prep_weights(norm1[0], w_in[0], a_ln_g[0], a_ln_b[0], a_ws[0], a_bs[0], b_conv_w[0], b_conv_b[0],
                      b_bias[0], mix_norm[0], w_out[0], norm2[0], peer_wq[0], peer_k1[0], peer_k2[0],
                      peer_down[0], peer_up[0], final_norm)
    hf = (hf_w1[0], hf_b1[0].reshape(1, -1), hf_w2[0], hf_b2[0].reshape(1, -1), hf_w3[0], hf_b3[0].reshape(1, -1),
          hf_freq[0], hf_w4[0])
    return (_trunk(x_prompt, w, hf), _trunk(x_sample, w, hf))
```

```python
import functools
import math

import numpy as np
import jax
import jax.numpy as jnp
from jax import lax
from jax.experimental import pallas as pl
from jax.experimental.pallas import tpu as pltpu

F32 = jnp.float32
BF16 = jnp.bfloat16
I32 = jnp.int32

EPS = 1e-6
V7X_LANES = 128
V7X_SUBLANES = 8
V7X_VMEM_BYTES = 64 * 1024 * 1024
VMEM_LIMIT = V7X_VMEM_BYTES - 8 * 1024 * 1024

CHUNK = 128
A_HEAD_DIM = 128
EMB_DIM = 33
BANDS = (EMB_DIM - 1) // 2
DECAY_TARGET = 1e-2
FAST_DECAY_PCT = 0.3
SLOW_DECAY_PCT = 1.5
FFT_N1 = 64
FFT_N2 = 128
FFT_N = FFT_N1 * FFT_N2
CONV_BLOCK = FFT_N // 2
N_KEYS = 128
KEY_SHIFT = 7
PEER_TOPK = 16
HALF_KEY = 128


def _cparams(sem):
    return pltpu.CompilerParams(dimension_semantics=sem, vmem_limit_bytes=VMEM_LIMIT)


def _rms(xf, g):
    return xf * lax.rsqrt(jnp.mean(xf * xf, axis=-1, keepdims=True) + EPS) * g


def _gelu(x):
    return 0.5 * x * (1.0 + lax.erf(x * np.float32(math.sqrt(0.5))))


def _inproj_kernel(xp_ref, x_ref, xn_ref, n1_ref, wu_ref, wv_ref, wb_ref, lng_ref, lnb_ref,
                   ws_ref, bsb_ref, cw_ref, cb_ref, mga_ref,
                   yan_ref, x0_ref, g_ref, ya_s, zs_s, *, tiles_per_seq, cblk):
    tm = x_ref.shape[0]
    d_a = wu_ref.shape[1]
    d_b = wb_ref.shape[2]
    i = pl.program_id(0)
    not_first = (i % tiles_per_seq != 0).astype(F32)
    not_last = (i % tiles_per_seq != tiles_per_seq - 1).astype(F32)

    xcat = jnp.concatenate([xp_ref[...], x_ref[...], xn_ref[...]], axis=0)
    hcat = _rms(xcat, n1_ref[...])
    h_all = hcat.astype(BF16)
    h = hcat[V7X_SUBLANES:V7X_SUBLANES + tm].astype(BF16)

    u = _gelu(jnp.dot(h, wu_ref[...], preferred_element_type=F32))
    v = _gelu(jnp.dot(h, wv_ref[...], preferred_element_type=F32))
    mu = jnp.mean(v, axis=-1, keepdims=True)
    vc = v - mu
    var = jnp.mean(vc * vc, axis=-1, keepdims=True)
    vb = (vc * lax.rsqrt(var + EPS) * lng_ref[...] + lnb_ref[...]).astype(BF16)
    for c in range(tm // CHUNK):
        rows = slice(c * CHUNK, (c + 1) * CHUNK)
        for hd in range(d_a // A_HEAD_DIM):
            cols = slice(hd * A_HEAD_DIM, (hd + 1) * A_HEAD_DIM)
            mixed = jnp.dot(ws_ref[hd], vb[rows, cols], preferred_element_type=F32) + bsb_ref[hd]
            ya_s[rows, cols] = u[rows, cols] * mixed
    yan_ref[...] = _rms(ya_s[...], mga_ref[...]).astype(yan_ref.dtype)

    halo = V7X_SUBLANES
    for cb in range(d_b // cblk):
        cols = slice(cb * cblk, (cb + 1) * cblk)
        parts = []
        for p in range(3):
            z = jnp.dot(h_all, wb_ref[p, :, cols], preferred_element_type=F32)
            zs_s[...] = z
            zs_s[0:halo, :] = z[0:halo] * not_first
            zs_s[tm + halo:tm + 2 * halo, :] = z[tm + halo:tm + 2 * halo] * not_last
            w = cw_ref[p]
            zc = (cb_ref[p:p + 1, cols]
                  + zs_s[halo - 1:halo - 1 + tm, :] * w[0:1, cols]
                  + zs_s[halo:halo + tm, :] * w[1:2, cols]
                  + zs_s[halo + 1:halo + 1 + tm, :] * w[2:3, cols])
            parts.append(zc)
        x0_ref[:, cols] = parts[0]
        g_ref[:, cols] = parts[1] * parts[2]


def _inproj(x, seq_len, norm1, w_u, w_v, w_b, ln_g, ln_b, ws, bsb, conv_w, conv_b, mix_g_a, tm=512, cblk=256):
    t, d = x.shape
    d_a = w_u.shape[1]
    d_b = w_b.shape[2]
    assert seq_len % tm == 0 and t % seq_len == 0 and tm % CHUNK == 0 and d_b % cblk == 0, (t, seq_len)
    nblk = tm // V7X_SUBLANES
    last8 = t // V7X_SUBLANES - 1
    const = lambda *shape: pl.BlockSpec(shape, lambda i: (0,) * len(shape), pipeline_mode=pl.Buffered(1))
    kern = functools.partial(_inproj_kernel, tiles_per_seq=seq_len // tm, cblk=cblk)
    return pl.pallas_call(
        kern,
        grid=(t // tm,),
        in_specs=[
            pl.BlockSpec((V7X_SUBLANES, d), lambda i: (jnp.maximum(i * nblk - 1, 0), 0)),
            pl.BlockSpec((tm, d), lambda i: (i, 0)),
            pl.BlockSpec((V7X_SUBLANES, d), lambda i: (jnp.minimum((i + 1) * nblk, last8), 0)),
            const(1, d), const(d, d_a), const(d, d_a), const(3, d, d_b), const(1, d_a), const(1, d_a),
            const(*ws.shape), const(*bsb.shape), const(3, 3, d_b), const(3, d_b), const(1, d_a),
        ],
        out_specs=[
            pl.BlockSpec((tm, d_a), lambda i: (i, 0)),
            pl.BlockSpec((tm, d_b), lambda i: (i, 0)),
            pl.BlockSpec((tm, d_b), lambda i: (i, 0)),
        ],
        out_shape=[
            jax.ShapeDtypeStruct((t, d_a), BF16),
            jax.ShapeDtypeStruct((t, d_b), F32),
            jax.ShapeDtypeStruct((t, d_b), F32),
        ],
        scratch_shapes=[pltpu.VMEM((tm, d_a), F32), pltpu.VMEM((tm + 2 * V7X_SUBLANES, cblk), F32)],
        compiler_params=_cparams(("parallel",)),
        name="inproj",
    )(x, x, x, norm1, w_u, w_v, w_b, ln_g, ln_b, ws, bsb, conv_w, conv_b, mix_g_a)


def _filt_kernel(fr_ref, dl_ref, w1_ref, b1_ref, w2_ref, b2_ref, w3_ref, b3_ref, fq_ref, w4_ref, o_ref,
                 *, seq_len, nblocks, n2_per_step):
    half = FFT_N1 // 2
    tr = half * n2_per_step
    e = pl.program_id(0) - nblocks
    r = lax.broadcasted_iota(I32, (tr, 1), 0)
    n2 = pl.program_id(1) * n2_per_step + r // half
    trow = (r % half) * FFT_N2 + n2
    lag = e * CONV_BLOCK + trow
    pos = jnp.abs(lag).astype(F32)
    valid = jnp.abs(lag) <= seq_len - 1
    t = pos / np.float32(max(seq_len - 1, 1))
    fw = np.float32(2.0 * math.pi / seq_len) * pos * fr_ref[...]
    hi = lax.Precision.HIGHEST
    w1 = w1_ref[...]
    z1 = (t * w1[0:1, :]
          + jnp.dot(jnp.cos(fw), w1[1:1 + BANDS, :], precision=hi, preferred_element_type=F32)
          + jnp.dot(-jnp.sin(fw), w1[1 + BANDS:, :], precision=hi, preferred_element_type=F32))
    fq = fq_ref[...]
    h = jnp.sin(fq[0:1, :] * (z1 + b1_ref[...]))
    h = jnp.sin(fq[1:2, :] * (jnp.dot(h, w2_ref[...], precision=hi, preferred_element_type=F32) + b2_ref[...]))
    h = jnp.sin(fq[2:3, :] * (jnp.dot(h, w3_ref[...], precision=hi, preferred_element_type=F32) + b3_ref[...]))
    h4 = jnp.dot(h, w4_ref[...], precision=hi, preferred_element_type=F32)
    window = jnp.exp(-t * dl_ref[...])
    res = jnp.where(valid, h4 * window, 0.0)
    d_b = res.shape[1]
    for i in range(n2_per_step):
        o_ref[:, i * d_b:(i + 1) * d_b] = res[i * half:(i + 1) * half]


def _hyena_filter_blocks(seq_len, w1, b1, w2, b2, w3, b3, freq, w4, n2_per_step=16):
    nblocks = seq_len // CONV_BLOCK
    width = w1.shape[1]
    d_b = w4.shape[1] // 2
    fr = jnp.asarray(np.linspace(1e-4, BANDS - 1, BANDS, dtype=np.float32)[None, :])
    min_decay = math.log(DECAY_TARGET) / SLOW_DECAY_PCT
    max_decay = math.log(DECAY_TARGET) / FAST_DECAY_PCT
    deltas = jnp.asarray(np.abs(np.linspace(min_decay, max_decay, d_b, dtype=np.float32))[None, :])
    w4h = w4.reshape(width, 2, d_b).transpose(1, 0, 2)
    const = lambda *shape: pl.BlockSpec(shape, lambda a, b: (0,) * len(shape))
    half = FFT_N1 // 2
    kern = functools.partial(_filt_kernel, seq_len=seq_len, nblocks=nblocks, n2_per_step=n2_per_step)
    return pl.pallas_call(
        kern,
        grid=(2 * nblocks, FFT_N2 // n2_per_step),
        in_specs=[
            const(1, BANDS), const(1, d_b), const(EMB_DIM, width), const(1, width), const(width, width),
            const(1, width), const(width, width), const(1, width), const(3, width),
            pl.BlockSpec((None, width, d_b), lambda ei, r: (jnp.where(ei >= nblocks, 0, 1), 0, 0)),
        ],
        out_specs=pl.BlockSpec((None, half, n2_per_step * d_b), lambda ei, r: (ei, 0, r)),
        out_shape=jax.ShapeDtypeStruct((2 * nblocks, half, FFT_N2 * d_b), F32),
        compiler_params=_cparams(("parallel", "parallel")),
        name="hyena_filter",
    )(fr, deltas, w1, b1, w2, b2, w3, b3, freq, w4h)


def _dft_tables():
    n1 = np.arange(FFT_N1)
    f64 = np.exp(-2j * np.pi * np.outer(n1, n1) / FFT_N1)
    half = FFT_N1 // 2
    fr, fi = f64.real, f64.imag
    lhs_data = np.block([[fr[:, :half], -fi[:, :half]], [fi[:, :half], fr[:, :half]]])
    lhs_real = np.concatenate([fr, fi], axis=0)
    gr, gi = fr[:half, :] / FFT_N, -fi[:half, :] / FFT_N
    lhs_inv = np.block([[gr, -gi], [gi, gr]])
    k1 = np.arange(FFT_N1)[:, None, None]
    k2 = np.arange(FFT_N2)[None, :, None]
    n2 = np.arange(FFT_N2)[None, None, :]
    g = np.exp(-2j * np.pi * (n2 * (k1 + FFT_N1 * k2) % FFT_N) / FFT_N)
    gfwd = np.concatenate([np.concatenate([g.real, -g.imag], axis=2),
                           np.concatenate([g.imag, g.real], axis=2)], axis=1)
    ht = np.conj(np.transpose(g, (0, 2, 1)))
    ginv = np.concatenate([np.concatenate([ht.real, -ht.imag], axis=2),
                           np.concatenate([ht.imag, ht.real], axis=2)], axis=1)
    as32 = lambda a: np.asarray(a, dtype=np.float32)
    return as32(lhs_data), as32(lhs_real), as32(lhs_inv), as32(gfwd), as32(ginv)


def _split(x):
    hi = x.astype(BF16)
    lo = (x - hi.astype(F32)).astype(BF16)
    return hi, lo


def _dot3(a_hi, a_lo, b):
    b_hi, b_lo = _split(b)
    acc = jnp.dot(a_hi, b_hi, preferred_element_type=F32)
    acc = acc + jnp.dot(a_hi, b_lo, preferred_element_type=F32)
    acc = acc + jnp.dot(a_lo, b_hi, preferred_element_type=F32)
    return acc


FFT_TN2 = 8


def _rows_to_lanes(x):
    xt = jnp.swapaxes(x, 0, 1)
    return jnp.concatenate([xt[i] for i in range(xt.shape[0])], axis=1)


def _lanes_to_rows(x, s):
    c = x.shape[1] // s
    return jnp.swapaxes(jnp.stack([x[:, i * c:(i + 1) * c] for i in range(s)], axis=0), 0, 1)


def _fft1_kernel(lh_ref, ll_ref, a_ref, b_ref, o_ref, *, zero_lanes, natural):
    if natural:
        a = _rows_to_lanes(a_ref[...])
        b = _rows_to_lanes(b_ref[...])
    else:
        a = a_ref[...]
        b = b_ref[...]
    if zero_lanes:
        row = lax.broadcasted_iota(I32, b.shape, 0)
        lane = lax.broadcasted_iota(I32, b.shape, 1) + pl.program_id(2) * b.shape[1]
        b = jnp.where(jnp.logical_and(row == 0, lane < zero_lanes), 0.0, b)
    res = _dot3(lh_ref[...], ll_ref[...], jnp.concatenate([a, b], axis=0))
    o_ref[...] = _lanes_to_rows(res, FFT_TN2).reshape(o_ref.shape)


def _fft1(lhs, x, a_of, b_of, g, p, d_b, natural, zero_lanes=0):
    half = FFT_N1 // 2
    lh, ll = _split(jnp.asarray(lhs))
    kern = functools.partial(_fft1_kernel, zero_lanes=zero_lanes, natural=natural)
    if natural:
        blk = lambda of: pl.BlockSpec((None, None, half, FFT_TN2, d_b), lambda q, j, t: of(q, j) + (0, t, 0))
    else:
        blk = lambda of: pl.BlockSpec((None, None, half, FFT_TN2 * d_b), lambda q, j, t: of(q, j) + (0, t))
    return pl.pallas_call(
        kern,
        grid=(g, p, FFT_N2 // FFT_TN2),
        in_specs=[
            pl.BlockSpec(lhs.shape, lambda q, j, t: (0, 0)),
            pl.BlockSpec(lhs.shape, lambda q, j, t: (0, 0)),
            blk(a_of), blk(b_of),
        ],
        out_specs=pl.BlockSpec((None, None, 2, FFT_N1, FFT_TN2, d_b), lambda q, j, t: (q, j, 0, 0, t, 0)),
        out_shape=jax.ShapeDtypeStruct((g, p, 2, FFT_N1, FFT_N2, d_b), F32),
        compiler_params=_cparams(("parallel", "parallel", "parallel")),
        name="fft1",
    )(lh, ll, x, x)


def _fft3_kernel(lh_ref, ll_ref, w_ref, o_ref):
    half = FFT_N1 // 2
    c = w_ref.shape[-1]
    w = _rows_to_lanes(w_ref[...].reshape(2 * FFT_N1, FFT_TN2, c))
    res = _dot3(lh_ref[...], ll_ref[...], w)
    o_ref[0] = _lanes_to_rows(res[:half], FFT_TN2)
    o_ref[1] = _lanes_to_rows(res[half:], FFT_TN2)


def _fft3(lhs, w6):
    g, p, _, _, _, d_b = w6.shape
    half = FFT_N1 // 2
    lh, ll = _split(jnp.asarray(lhs))
    return pl.pallas_call(
        _fft3_kernel,
        grid=(g, p, FFT_N2 // FFT_TN2),
        in_specs=[
            pl.BlockSpec(lhs.shape, lambda q, j, t: (0, 0)),
            pl.BlockSpec(lhs.shape, lambda q, j, t: (0, 0)),
            pl.BlockSpec((None, None, 2, FFT_N1, FFT_TN2, d_b), lambda q, j, t: (q, j, 0, 0, t, 0)),
        ],
        out_specs=pl.BlockSpec((None, 2, None, half, FFT_TN2, d_b), lambda q, j, t: (q, 0, j, 0, t, 0)),
        out_shape=jax.ShapeDtypeStruct((g, 2, p, half, FFT_N2, d_b), F32),
        compiler_params=_cparams(("parallel", "parallel", "parallel")),
        name="fft3",
    )(lh, ll, w6)


def _fft2_filter_kernel(gh_ref, gl_ref, x_ref, o_ref):
    ct = x_ref.shape[-1]
    for k in range(x_ref.shape[1]):
        z = _dot3(gh_ref[k], gl_ref[k], x_ref[:, k].reshape(2 * FFT_N2, ct))
        o_ref[:, k] = z.reshape(2, FFT_N2, ct)


def _fft2_filter(gfwd_hl, x1f, ct=256, kb=8):
    nd, _, _, _, d_b = x1f.shape
    gh, gl = gfwd_hl
    gspec = pl.BlockSpec((kb, 2 * FFT_N2, 2 * FFT_N2), lambda k, d, c: (k, 0, 0))
    xspec = pl.BlockSpec((None, 2, kb, FFT_N2, ct), lambda k, d, c: (d, 0, k, 0, c))
    return pl.pallas_call(
        _fft2_filter_kernel,
        grid=(FFT_N1 // kb, nd, d_b // ct),
        in_specs=[gspec, gspec, xspec],
        out_specs=xspec,
        out_shape=jax.ShapeDtypeStruct(x1f.shape, F32),
        compiler_params=_cparams(("parallel", "parallel", "parallel")),
        name="fft2_filter",
    )(gh, gl, x1f)


def _fft2_mix_kernel(gh_ref, gl_ref, ih_ref, il_ref, x_ref, k_ref, o_ref, *, nblocks):
    ct = x_ref.shape[-1]
    for k in range(x_ref.shape[2]):
        zs = []
        for j in range(nblocks):
            z = _dot3(gh_ref[k], gl_ref[k], x_ref[j, :, k].reshape(2 * FFT_N2, ct))
            zs.append((z[:FFT_N2], z[FFT_N2:]))
        for i in range(nblocks):
            yr = jnp.zeros((FFT_N2, ct), F32)
            yi = jnp.zeros((FFT_N2, ct), F32)
            for j in range(nblocks):
                d = i - j + nblocks - 1
                kr, ki = k_ref[d, 0, k], k_ref[d, 1, k]
                zr, zi = zs[j]
                yr = yr + kr * zr - ki * zi
                yi = yi + kr * zi + ki * zr
            w = _dot3(ih_ref[k], il_ref[k], jnp.concatenate([yr, yi], axis=0))
            o_ref[i, :, k] = w.reshape(2, FFT_N2, ct)


def _fft2_mix(gfwd_hl, ginv_hl, x1, kspec, ct=256):
    g, p, _, _, _, d_b = x1.shape
    nd = kspec.shape[0]
    kb = max(1, 8 // p)
    gspec = pl.BlockSpec((kb, 2 * FFT_N2, 2 * FFT_N2), lambda k, q, c: (k, 0, 0))
    xspec = pl.BlockSpec((None, p, 2, kb, FFT_N2, ct), lambda k, q, c: (q, 0, 0, k, 0, c))
    kern = functools.partial(_fft2_mix_kernel, nblocks=p)
    return pl.pallas_call(
        kern,
        grid=(FFT_N1 // kb, g, d_b // ct),
        in_specs=[gspec, gspec, gspec, gspec, xspec,
                  pl.BlockSpec((nd, 2, kb, FFT_N2, ct), lambda k, q, c: (0, 0, k, 0, c))],
        out_specs=xspec,
        out_shape=jax.ShapeDtypeStruct(x1.shape, F32),
        compiler_params=_cparams(("parallel", "parallel", "parallel")),
        name="fft2_mix",
    )(gfwd_hl[0], gfwd_hl[1], ginv_hl[0], ginv_hl[1], x1, kspec)


def _hyena_long_conv(g3, w1, b1, w2, b2, w3, b3, freq, w4):
    bsz, seq_len, d_b = g3.shape
    nblocks = seq_len // CONV_BLOCK
    half = FFT_N1 // 2
    lhs_data, lhs_real, lhs_inv, gfwd, ginv = _dft_tables()
    gfwd_hl = _split(jnp.asarray(gfwd))
    ginv_hl = _split(jnp.asarray(ginv))
    kext = _hyena_filter_blocks(seq_len, w1, b1, w2, b2, w3, b3, freq, w4)
    nd = 2 * nblocks - 1
    k1f = _fft1(lhs_real, kext[None], lambda q, j: (0, q + 1), lambda q, j: (0, q), nd, 1, d_b,
                natural=False, zero_lanes=d_b)
    kspec = _fft2_filter(gfwd_hl, k1f.reshape(nd, 2, FFT_N1, FFT_N2, d_b))
    g5 = g3.reshape(bsz, nblocks, half, FFT_N2, d_b)
    x1 = _fft1(lhs_data, g5, lambda q, j: (2 * q, j), lambda q, j: (2 * q + 1, j), bsz // 2, nblocks, d_b,
               natural=True)
    wmix = _fft2_mix(gfwd_hl, ginv_hl, x1, kspec)
    y6 = _fft3(lhs_inv, wmix)
    return y6.reshape(bsz, seq_len, d_b)


def _postmix_kernel(x_ref, yan_ref, x0_ref, g_ref, yc_ref, bias_ref, mgb_ref, wo_ref, n2_ref, x1_ref, xn_ref):
    d_a = yan_ref.shape[1]
    gg = g_ref[...]
    yb = x0_ref[...] * (yc_ref[...] + gg * bias_ref[...])
    ybn = _rms(yb, mgb_ref[...]).astype(BF16)
    y = jnp.dot(yan_ref[...], wo_ref[0:d_a, :], preferred_element_type=F32)
    y = y + jnp.dot(ybn, wo_ref[d_a:, :], preferred_element_type=F32)
    x1 = x_ref[...] + y
    x1_ref[...] = x1
    xn_ref[...] = _rms(x1, n2_ref[...]).astype(xn_ref.dtype)


def _postmix(x, yan, x0, g, yconv, bias, mix_g_b, w_out, norm2, tm=512):
    t, d = x.shape
    d_a = yan.shape[1]
    d_b = x0.shape[1]
    assert t % tm == 0, t
    row = lambda w: pl.BlockSpec((tm, w), lambda i: (i, 0))
    const = lambda *shape: pl.BlockSpec(shape, lambda i: (0,) * len(shape), pipeline_mode=pl.Buffered(1))
    return pl.pallas_call(
        _postmix_kernel,
        grid=(t // tm,),
        in_specs=[row(d), row(d_a), row(d_b), row(d_b), row(d_b), const(1, d_b), const(1, d_b),
                  const(d_a + d_b, d), const(1, d)],
        out_specs=[row(d), row(d)],
        out_shape=[jax.ShapeDtypeStruct((t, d), F32), jax.ShapeDtypeStruct((t, d), BF16)],
        compiler_params=_cparams(("parallel",)),
        name="postmix",
    )(x, yan, x0, g, yconv, bias, mix_g_b, w_out, norm2)


def _staircase():
    return [(i, j) for i in range(PEER_TOPK) for j in range(PEER_TOPK) if (i + 1) * (j + 1) <= PEER_TOPK]


ARGMAX_LANES = 4


def _stream_argmax(val_ref, tag_ref, nrows, prev, slab):
    neg = np.float32(-np.inf)
    best_v = [jnp.full(slab, neg, F32) for _ in range(ARGMAX_LANES)]
    best_i = [jnp.zeros(slab, I32) for _ in range(ARGMAX_LANES)]
    best_t = [jnp.zeros(slab, I32) for _ in range(ARGMAX_LANES)]
    for n in range(nrows):
        v = jnp.where(prev == n, neg, val_ref[n])
        val_ref[n] = v
        k = n % ARGMAX_LANES
        better = v > best_v[k]
        best_v[k] = jnp.where(better, v, best_v[k])
        best_i[k] = jnp.where(better, n, best_i[k])
        if tag_ref is not None:
            best_t[k] = jnp.where(better, tag_ref[n], best_t[k])
    width = ARGMAX_LANES
    while width > 1:
        width //= 2
        for k in range(width):
            va, vb = best_v[k], best_v[k + width]
            ia, ib = best_i[k], best_i[k + width]
            take_b = jnp.logical_or(vb > va, jnp.logical_and(vb == va, ib < ia))
            best_v[k] = jnp.where(take_b, vb, va)
            best_i[k] = jnp.where(take_b, ib, ia)
            best_t[k] = jnp.where(take_b, best_t[k + width], best_t[k])
    return best_v[0], best_i[0], best_t[0]


def _topk_kernel(xn_ref, wqt_ref, kbd_ref, e_ref, gt_ref, s_s, v_s, i_s, c_s, ce_s, b_s, es_s, *, heads):
    tm = xn_ref.shape[0]
    hk = heads * HALF_KEY
    neg = np.float32(-np.inf)
    q_t = lax.dot_general(wqt_ref[...], xn_ref[...], (((1,), (1,)), ((), ())),
                          preferred_element_type=F32).astype(BF16)
    slab = (heads, tm)
    no_pick = jnp.full(slab, -1, I32)
    for half in range(2):
        s_t = jnp.dot(kbd_ref[half], q_t[half * hk:(half + 1) * hk], preferred_element_type=F32)
        s_s[...] = s_t.reshape(N_KEYS, heads, tm)

        def level1(r, prev, half=half):
            m, idx, _ = _stream_argmax(s_s, None, N_KEYS, prev, slab)
            v_s[half, r] = m
            i_s[half, r] = idx
            return idx

        lax.fori_loop(0, PEER_TOPK, level1, no_pick)

    cands = _staircase()
    for p, (i, j) in enumerate(cands):
        c_s[p] = v_s[0, i] + v_s[1, j]
        ce_s[p] = i_s[0, i] * N_KEYS + i_s[1, j]

    def level2(r, prev):
        m, pid, e = _stream_argmax(c_s, ce_s, len(cands), prev, slab)
        es_s[r] = e
        b_s[r] = m
        return pid

    lax.fori_loop(0, PEER_TOPK, level2, no_pick)
    best = b_s[...]
    ex = jnp.exp(best - jnp.max(best, axis=0, keepdims=True))
    gate = ex / jnp.sum(ex, axis=0, keepdims=True)
    gt_ref[...] = gate.reshape(PEER_TOPK * heads, tm).T
    e_ref[...] = es_s[...].reshape(PEER_TOPK * heads, tm).T


def _peer_topk(xn, wqt, kbd, heads, tm=512):
    t, d = xn.shape
    nsel = PEER_TOPK * heads
    ncand = len(_staircase())
    assert t % tm == 0 and heads == V7X_SUBLANES, (t, heads)
    const = lambda *shape: pl.BlockSpec(shape, lambda i: (0,) * len(shape), pipeline_mode=pl.Buffered(1))
    kern = functools.partial(_topk_kernel, heads=heads)
    return pl.pallas_call(
        kern,
        grid=(t // tm,),
        in_specs=[pl.BlockSpec((tm, d), lambda i: (i, 0)), const(*wqt.shape), const(*kbd.shape)],
        out_specs=[pl.BlockSpec((tm, nsel), lambda i: (i, 0)), pl.BlockSpec((tm, nsel), lambda i: (i, 0))],
        out_shape=[jax.ShapeDtypeStruct((t, nsel), I32), jax.ShapeDtypeStruct((t, nsel), F32)],
        scratch_shapes=[
            pltpu.VMEM((N_KEYS, heads, tm), F32),
            pltpu.VMEM((2, PEER_TOPK, heads, tm), F32),
            pltpu.VMEM((2, PEER_TOPK, heads, tm), I32),
            pltpu.VMEM((ncand, heads, tm), F32),
            pltpu.VMEM((ncand, heads, tm), I32),
            pltpu.VMEM((PEER_TOPK, heads, tm), F32),
            pltpu.VMEM((PEER_TOPK, heads, tm), I32),
        ],
        compiler_params=_cparams(("parallel",)),
        name="peer_topk",
    )(xn, wqt, kbd)


def _matmul_kernel(a_ref, b_ref, o_ref):
    o_ref[...] = jnp.dot(a_ref[...], b_ref[...], preferred_element_type=F32).astype(o_ref.dtype)


def _peer_a(xn, down_t, tm=1024, tn=2048):
    t, d = xn.shape
    n = down_t.shape[1]
    assert t % tm == 0 and n % tn == 0, (t, n)
    return pl.pallas_call(
        _matmul_kernel,
        grid=(n // tn, t // tm),
        in_specs=[pl.BlockSpec((tm, d), lambda j, i: (i, 0)), pl.BlockSpec((d, tn), lambda j, i: (0, j))],
        out_specs=pl.BlockSpec((tm, tn), lambda j, i: (i, j)),
        out_shape=jax.ShapeDtypeStruct((t, n), F32),
        compiler_params=_cparams(("parallel", "parallel")),
        name="peer_a",
    )(xn, down_t)


PEER_B_GROUP = 16


def _peer_b_kernel(a_ref, e_ref, gt_ref, m_ref):
    c_tok = a_ref.shape[0]
    nsel = e_ref.shape[1]
    grp = PEER_B_GROUP
    ngrp = c_tok // grp
    sub = lax.broadcasted_iota(I32, (grp, N_KEYS, nsel), 1)

    def rows_of(gi):
        return pl.ds(pl.multiple_of(gi * grp, grp), grp)

    def pick(gi):
        rows = rows_of(gi)
        e = e_ref[rows, :]
        ai = lax.shift_right_logical(e, KEY_SHIFT)
        bi = e & (N_KEYS - 1)
        accs = [jnp.zeros((grp, nsel), F32) for _ in range(4)]
        for a in range(N_KEYS):
            blk = a_ref[rows, a * N_KEYS:(a + 1) * N_KEYS]
            accs[a % 4] = jnp.where(ai == a, jnp.take_along_axis(blk, bi, axis=1), accs[a % 4])
        picked = (accs[0] + accs[1]) + (accs[2] + accs[3])
        return gt_ref[rows, :] * _gelu(picked), e

    def scatter(gi, w, e):
        ai = lax.shift_right_logical(e, KEY_SHIFT)
        bi = e & (N_KEYS - 1)
        w1t = jnp.where(sub == ai[:, None, :], w[:, None, :], 0.0).astype(BF16)
        e2t = jnp.where(sub == bi[:, None, :], 1.0, 0.0).astype(BF16)
        m3 = jnp.einsum("cas,cbs->cab", w1t, e2t, preferred_element_type=F32)
        mt = jnp.swapaxes(m3, 0, 1).astype(m_ref.dtype)
        rows = rows_of(gi)
        for a in range(N_KEYS):
            m_ref[rows, a * N_KEYS:(a + 1) * N_KEYS] = mt[a]

    def step(gi, carry):
        nxt = pick(gi)
        scatter(gi - 1, *carry)
        return nxt

    last = lax.fori_loop(1, ngrp, step, pick(0))
    scatter(ngrp - 1, *last)


def _peer_b(a, eidx, gate, tc=256):
    t, n = a.shape
    nsel = eidx.shape[1]
    assert t % tc == 0 and tc % PEER_B_GROUP == 0 and n == N_KEYS * N_KEYS and nsel == N_KEYS, (t, n, nsel)
    return pl.pallas_call(
        _peer_b_kernel,
        grid=(t // tc,),
        in_specs=[pl.BlockSpec((tc, n), lambda i: (i, 0)), pl.BlockSpec((tc, nsel), lambda i: (i, 0)),
                  pl.BlockSpec((tc, nsel), lambda i: (i, 0))],
        out_specs=pl.BlockSpec((tc, n), lambda i: (i, 0)),
        out_shape=jax.ShapeDtypeStruct((t, n), BF16),
        compiler_params=_cparams(("parallel",)),
        name="peer_b",
    )(a, eidx, gate)


def _peer_c_kernel(m_ref, up_ref, x1_ref, fn_ref, o_ref, acc_s):
    k = pl.program_id(1)

    @pl.when(k == 0)
    def _():
        acc_s[...] = jnp.zeros_like(acc_s)

    acc_s[...] += jnp.dot(m_ref[...], up_ref[...], preferred_element_type=F32)

    @pl.when(k == pl.num_programs(1) - 1)
    def _():
        o_ref[...] = _rms(x1_ref[...] + acc_s[...], fn_ref[...])


def _peer_c(m, up, x1, final_norm, tm=1024, tk=1024):
    t, n = m.shape
    d = up.shape[1]
    assert t % tm == 0 and n % tk == 0, (t, n)
    return pl.pallas_call(
        _peer_c_kernel,
        grid=(t // tm, n // tk),
        in_specs=[pl.BlockSpec((tm, tk), lambda i, k: (i, k)), pl.BlockSpec((tk, d), lambda i, k: (k, 0)),
                  pl.BlockSpec((tm, d), lambda i, k: (i, 0), pipeline_mode=pl.Buffered(1)),
                  pl.BlockSpec((1, d), lambda i, k: (0, 0))],
        out_specs=pl.BlockSpec((tm, d), lambda i, k: (i, 0)),
        out_shape=jax.ShapeDtypeStruct((t, d), F32),
        scratch_shapes=[pltpu.VMEM((tm, d), F32)],
        compiler_params=_cparams(("parallel", "arbitrary")),
        name="peer_c",
    )(m, up, x1, final_norm)


def _prep_weights(norm1, w_in, a_ln_g, a_ln_b, a_ws, a_bs, b_conv_w, b_conv_b, b_bias, mix_norm, w_out, norm2,
                  peer_wq, peer_k1, peer_k2, peer_down, peer_up, final_norm):
    d = w_in.shape[0]
    d_a = a_ln_g.shape[0]
    d_b = b_bias.shape[0]
    heads, n_keys, half_key = peer_k1.shape
    row = lambda v: v.reshape(1, -1).astype(F32)
    w = dict(
        norm1=row(norm1), norm2=row(norm2), final_norm=row(final_norm),
        w_u=w_in[:, :d_a].astype(BF16), w_v=w_in[:, d_a:2 * d_a].astype(BF16),
        w_b=w_in[:, 2 * d_a:].reshape(d, 3, d_b).transpose(1, 0, 2).astype(BF16),
        ln_g=row(a_ln_g), ln_b=row(a_ln_b),
        ws=a_ws.astype(BF16),
        bsb=jnp.broadcast_to(a_bs[:, :, None], a_bs.shape + (A_HEAD_DIM,)).astype(F32),
        conv_w=b_conv_w.reshape(3, 3, d_b).transpose(1, 0, 2).astype(F32),
        conv_b=b_conv_b.reshape(3, d_b).astype(F32),
        bias=row(b_bias), mix_g_a=row(mix_norm[:d_a]), mix_g_b=row(mix_norm[d_a:]),
        w_out=w_out.astype(BF16),
        down_t=peer_down.astype(BF16).T, up=peer_up.astype(BF16),
    )
    wq_t = peer_wq.T.reshape(heads, 2, half_key, d).transpose(1, 0, 2, 3).reshape(2 * heads * half_key, d)
    w["wq_t"] = wq_t.astype(BF16)
    eye = jnp.eye(heads, dtype=F32)
    kbd = [jnp.einsum("hnd,hg->nhgd", k, eye).reshape(n_keys * heads, heads * half_key) for k in (peer_k1, peer_k2)]
    w["kbd"] = jnp.stack(kbd).astype(BF16)
    w["heads"] = heads
    return w


def _trunk(x3, w, hf):
    bsz, seq_len, d = x3.shape
    t = bsz * seq_len
    x = x3.reshape(t, d)
    yan, x0, g = _inproj(x, seq_len, w["norm1"], w["w_u"], w["w_v"], w["w_b"], w["ln_g"], w["ln_b"],
                         w["ws"], w["bsb"], w["conv_w"], w["conv_b"], w["mix_g_a"])
    d_b = x0.shape[1]
    yconv = _hyena_long_conv(g.reshape(bsz, seq_len, d_b), *hf).reshape(t, d_b)
    x1, xn = _postmix(x, yan, x0, g, yconv, w["bias"], w["mix_g_b"], w["w_out"], w["norm2"])
    eidx, gate = _peer_topk(xn, w["wq_t"], w["kbd"], w["heads"])
    a = _peer_a(xn, w["down_t"])
    m = _peer_b(a, eidx, gate)
    out = _peer_c(m, w["up"], x1, w["final_norm"])
    return out.reshape(bsz, seq_len, d)


def kernel(x_prompt, x_sample, norm1, w_in, a_ln_g, a_ln_b, a_ws, a_bs, b_conv_w, b_conv_b, hf_w1, hf_b1, hf_w2, hf_b2, hf_w3, hf_b3, hf_freq, hf_w4, b_bias, mix_norm, w_out, norm2, peer_wq, peer_k1, peer_k2, peer_down, peer_up, final_norm):
    assert norm1.shape[0] == 1, "single-layer trunk"
    w = _prep_weights(norm1[0], w_in[0], a_ln_g[0], a_ln_b[0], a_ws[0], a_bs[0], b_conv_w[0], b_conv_b[0],
                      b_bias[0], mix_norm[0], w_out[0], norm2[0], peer_wq[0], peer_k1[0], peer_k2[0],
                      peer_down[0], peer_up[0], final_norm)
    hf = (hf_w1[0], hf_b1[0].reshape(1, -1), hf_w2[0], hf_b2[0].reshape(1, -1), hf_w3[0], hf_b3[0].reshape(1, -1),
          hf_freq[0], hf_w4[0])
    return (_trunk(x_prompt, w, hf), _trunk(x_sample, w, hf))
```

```python
import functools
import math

import numpy as np
import jax
import jax.numpy as jnp
from jax import lax
from jax.experimental import pallas as pl
from jax.experimental.pallas import tpu as pltpu

F32 = jnp.float32
BF16 = jnp.bfloat16
I32 = jnp.int32

EPS = 1e-6
V7X_LANES = 128
V7X_SUBLANES = 8
V7X_MXU_WIDTH = 256
V7X_VMEM_BYTES = 64 * 1024 * 1024
VMEM_LIMIT = V7X_VMEM_BYTES - 8 * 1024 * 1024

CHUNK = 128
A_HEAD_DIM = 128
EMB_DIM = 33
BANDS = (EMB_DIM - 1) // 2
DECAY_TARGET = 1e-2
FAST_DECAY_PCT = 0.3
SLOW_DECAY_PCT = 1.5
FFT_N1 = 64
FFT_N2 = 128
FFT_N = FFT_N1 * FFT_N2
CONV_BLOCK = FFT_N // 2
N_KEYS = 128
KEY_SHIFT = 7
PEER_TOPK = 16
HALF_KEY = 128


def _cparams(sem):
    return pltpu.CompilerParams(dimension_semantics=sem, vmem_limit_bytes=VMEM_LIMIT)


def _rms(xf, g):
    return xf * lax.rsqrt(jnp.mean(xf * xf, axis=-1, keepdims=True) + EPS) * g


def _gelu(x):
    return 0.5 * x * (1.0 + lax.erf(x * np.float32(math.sqrt(0.5))))


def _inproj_kernel(xp_ref, x_ref, xn_ref, n1_ref, wu_ref, wv_ref, wb_ref, lng_ref, lnb_ref,
                   ws_ref, bsb_ref, cw_ref, cb_ref, mga_ref,
                   yan_ref, x0_ref, g_ref, ya_s, zs_s, *, tiles_per_seq, cblk):
    tm = x_ref.shape[0]
    d_a = wu_ref.shape[1]
    d_b = wb_ref.shape[2]
    i = pl.program_id(0)
    not_first = (i % tiles_per_seq != 0).astype(F32)
    not_last = (i % tiles_per_seq != tiles_per_seq - 1).astype(F32)

    xcat = jnp.concatenate([xp_ref[...], x_ref[...], xn_ref[...]], axis=0)
    hcat = _rms(xcat, n1_ref[...])
    h_all = hcat.astype(BF16)
    h = hcat[V7X_SUBLANES:V7X_SUBLANES + tm].astype(BF16)

    u = _gelu(jnp.dot(h, wu_ref[...], preferred_element_type=F32))
    v = _gelu(jnp.dot(h, wv_ref[...], preferred_element_type=F32))
    mu = jnp.mean(v, axis=-1, keepdims=True)
    vc = v - mu
    var = jnp.mean(vc * vc, axis=-1, keepdims=True)
    vb = (vc * lax.rsqrt(var + EPS) * lng_ref[...] + lnb_ref[...]).astype(BF16)
    for c in range(tm // CHUNK):
        rows = slice(c * CHUNK, (c + 1) * CHUNK)
        for hd in range(d_a // A_HEAD_DIM):
            cols = slice(hd * A_HEAD_DIM, (hd + 1) * A_HEAD_DIM)
            mixed = jnp.dot(ws_ref[hd], vb[rows, cols], preferred_element_type=F32) + bsb_ref[hd]
            ya_s[rows, cols] = u[rows, cols] * mixed
    yan_ref[...] = _rms(ya_s[...], mga_ref[...]).astype(yan_ref.dtype)

    halo = V7X_SUBLANES
    for cb in range(d_b // cblk):
        cols = slice(cb * cblk, (cb + 1) * cblk)
        parts = []
        for p in range(3):
            z = jnp.dot(h_all, wb_ref[p, :, cols], preferred_element_type=F32)
            zs_s[...] = z
            zs_s[0:halo, :] = z[0:halo] * not_first
            zs_s[tm + halo:tm + 2 * halo, :] = z[tm + halo:tm + 2 * halo] * not_last
            w = cw_ref[p]
            zc = (cb_ref[p:p + 1, cols]
                  + zs_s[halo - 1:halo - 1 + tm, :] * w[0:1, cols]
                  + zs_s[halo:halo + tm, :] * w[1:2, cols]
                  + zs_s[halo + 1:halo + 1 + tm, :] * w[2:3, cols])
            parts.append(zc)
        x0_ref[:, cols] = parts[0]
        g_ref[:, cols] = parts[1] * parts[2]


def _inproj(x, seq_len, norm1, w_u, w_v, w_b, ln_g, ln_b, ws, bsb, conv_w, conv_b, mix_g_a, tm=512, cblk=256):
    t, d = x.shape
    d_a = w_u.shape[1]
    d_b = w_b.shape[2]
    assert seq_len % tm == 0 and t % seq_len == 0 and tm % CHUNK == 0 and d_b % cblk == 0, (t, seq_len)
    nblk = tm // V7X_SUBLANES
    last8 = t // V7X_SUBLANES - 1
    const = lambda *shape: pl.BlockSpec(shape, lambda i: (0,) * len(shape), pipeline_mode=pl.Buffered(1))
    kern = functools.partial(_inproj_kernel, tiles_per_seq=seq_len // tm, cblk=cblk)
    return pl.pallas_call(
        kern,
        grid=(t // tm,),
        in_specs=[
            pl.BlockSpec((V7X_SUBLANES, d), lambda i: (jnp.maximum(i * nblk - 1, 0), 0)),
            pl.BlockSpec((tm, d), lambda i: (i, 0)),
            pl.BlockSpec((V7X_SUBLANES, d), lambda i: (jnp.minimum((i + 1) * nblk, last8), 0)),
            const(1, d), const(d, d_a), const(d, d_a), const(3, d, d_b), const(1, d_a), const(1, d_a),
            const(*ws.shape), const(*bsb.shape), const(3, 3, d_b), const(3, d_b), const(1, d_a),
        ],
        out_specs=[
            pl.BlockSpec((tm, d_a), lambda i: (i, 0)),
            pl.BlockSpec((tm, d_b), lambda i: (i, 0)),
            pl.BlockSpec((tm, d_b), lambda i: (i, 0)),
        ],
        out_shape=[
            jax.ShapeDtypeStruct((t, d_a), BF16),
            jax.ShapeDtypeStruct((t, d_b), F32),
            jax.ShapeDtypeStruct((t, d_b), F32),
        ],
        scratch_shapes=[pltpu.VMEM((tm, d_a), F32), pltpu.VMEM((tm + 2 * V7X_SUBLANES, cblk), F32)],
        compiler_params=_cparams(("parallel",)),
        name="inproj",
    )(x, x, x, norm1, w_u, w_v, w_b, ln_g, ln_b, ws, bsb, conv_w, conv_b, mix_g_a)


def _filt_kernel(fr_ref, dl_ref, w1_ref, b1_ref, w2_ref, b2_ref, w3_ref, b3_ref, fq_ref, w4_ref, o_ref,
                 *, seq_len, nblocks, n2_per_step):
    half = FFT_N1 // 2
    tr = half * n2_per_step
    e = pl.program_id(0) - nblocks
    r = lax.broadcasted_iota(I32, (tr, 1), 0)
    n2 = pl.program_id(1) * n2_per_step + r // half
    trow = (r % half) * FFT_N2 + n2
    lag = e * CONV_BLOCK + trow
    pos = jnp.abs(lag).astype(F32)
    valid = jnp.abs(lag) <= seq_len - 1
    t = pos / np.float32(max(seq_len - 1, 1))
    fw = np.float32(2.0 * math.pi / seq_len) * pos * fr_ref[...]
    hi = lax.Precision.HIGHEST
    w1 = w1_ref[...]
    z1 = (t * w1[0:1, :]
          + jnp.dot(jnp.cos(fw), w1[1:1 + BANDS, :], precision=hi, preferred_element_type=F32)
          + jnp.dot(-jnp.sin(fw), w1[1 + BANDS:, :], precision=hi, preferred_element_type=F32))
    fq = fq_ref[...]
    h = jnp.sin(fq[0:1, :] * (z1 + b1_ref[...]))
    h = jnp.sin(fq[1:2, :] * (jnp.dot(h, w2_ref[...], precision=hi, preferred_element_type=F32) + b2_ref[...]))
    h = jnp.sin(fq[2:3, :] * (jnp.dot(h, w3_ref[...], precision=hi, preferred_element_type=F32) + b3_ref[...]))
    h4 = jnp.dot(h, w4_ref[...], precision=hi, preferred_element_type=F32)
    window = jnp.exp(-t * dl_ref[...])
    res = jnp.where(valid, h4 * window, 0.0)
    d_b = res.shape[1]
    for i in range(n2_per_step):
        o_ref[:, i * d_b:(i + 1) * d_b] = res[i * half:(i + 1) * half]


def _hyena_filter_blocks(seq_len, w1, b1, w2, b2, w3, b3, freq, w4, n2_per_step=16):
    nblocks = seq_len // CONV_BLOCK
    width = w1.shape[1]
    d_b = w4.shape[1] // 2
    fr = jnp.asarray(np.linspace(1e-4, BANDS - 1, BANDS, dtype=np.float32)[None, :])
    min_decay = math.log(DECAY_TARGET) / SLOW_DECAY_PCT
    max_decay = math.log(DECAY_TARGET) / FAST_DECAY_PCT
    deltas = jnp.asarray(np.abs(np.linspace(min_decay, max_decay, d_b, dtype=np.float32))[None, :])
    w4h = w4.reshape(width, 2, d_b).transpose(1, 0, 2)
    const = lambda *shape: pl.BlockSpec(shape, lambda a, b: (0,) * len(shape))
    half = FFT_N1 // 2
    kern = functools.partial(_filt_kernel, seq_len=seq_len, nblocks=nblocks, n2_per_step=n2_per_step)
    return pl.pallas_call(
        kern,
        grid=(2 * nblocks, FFT_N2 // n2_per_step),
        in_specs=[
            const(1, BANDS), const(1, d_b), const(EMB_DIM, width), const(1, width), const(width, width),
            const(1, width), const(width, width), const(1, width), const(3, width),
            pl.BlockSpec((None, width, d_b), lambda ei, r: (jnp.where(ei >= nblocks, 0, 1), 0, 0)),
        ],
        out_specs=pl.BlockSpec((None, half, n2_per_step * d_b), lambda ei, r: (ei, 0, r)),
        out_shape=jax.ShapeDtypeStruct((2 * nblocks, half, FFT_N2 * d_b), F32),
        compiler_params=_cparams(("parallel", "parallel")),
        name="hyena_filter",
    )(fr, deltas, w1, b1, w2, b2, w3, b3, freq, w4h)


def _dft_tables():
    n1 = np.arange(FFT_N1)
    f64 = np.exp(-2j * np.pi * np.outer(n1, n1) / FFT_N1)
    half = FFT_N1 // 2
    fr, fi = f64.real, f64.imag
    lhs_data = np.block([[fr[:, :half], -fi[:, :half]], [fi[:, :half], fr[:, :half]]])
    lhs_real = np.concatenate([fr, fi], axis=0)
    gr, gi = fr[:half, :] / FFT_N, -fi[:half, :] / FFT_N
    lhs_inv = np.block([[gr, -gi], [gi, gr]])
    k1 = np.arange(FFT_N1)[:, None, None]
    k2 = np.arange(FFT_N2)[None, :, None]
    n2 = np.arange(FFT_N2)[None, None, :]
    g = np.exp(-2j * np.pi * (n2 * (k1 + FFT_N1 * k2) % FFT_N) / FFT_N)
    gfwd = np.concatenate([np.concatenate([g.real, -g.imag], axis=2),
                           np.concatenate([g.imag, g.real], axis=2)], axis=1)
    ht = np.conj(np.transpose(g, (0, 2, 1)))
    ginv = np.concatenate([np.concatenate([ht.real, -ht.imag], axis=2),
                           np.concatenate([ht.imag, ht.real], axis=2)], axis=1)
    as32 = lambda a: np.asarray(a, dtype=np.float32)
    return as32(lhs_data), as32(lhs_real), as32(lhs_inv), as32(gfwd), as32(ginv)


def _split(x):
    hi = x.astype(BF16)
    lo = (x - hi.astype(F32)).astype(BF16)
    return hi, lo


def _dot3(a_hi, a_lo, b):
    b_hi, b_lo = _split(b)
    acc = jnp.dot(a_hi, b_hi, preferred_element_type=F32)
    acc = acc + jnp.dot(a_hi, b_lo, preferred_element_type=F32)
    acc = acc + jnp.dot(a_lo, b_hi, preferred_element_type=F32)
    return acc


FFT_TN2 = 8


def _rows_to_lanes(x):
    xt = jnp.swapaxes(x, 0, 1)
    return jnp.concatenate([xt[i] for i in range(xt.shape[0])], axis=1)


def _lanes_to_rows(x, s):
    c = x.shape[1] // s
    return jnp.swapaxes(jnp.stack([x[:, i * c:(i + 1) * c] for i in range(s)], axis=0), 0, 1)


def _fft1_kernel(lh_ref, ll_ref, a_ref, b_ref, o_ref, *, zero_lanes, natural):
    if natural:
        a = _rows_to_lanes(a_ref[...])
        b = _rows_to_lanes(b_ref[...])
    else:
        a = a_ref[...]
        b = b_ref[...]
    if zero_lanes:
        row = lax.broadcasted_iota(I32, b.shape, 0)
        lane = lax.broadcasted_iota(I32, b.shape, 1) + pl.program_id(2) * b.shape[1]
        b = jnp.where(jnp.logical_and(row == 0, lane < zero_lanes), 0.0, b)
    res = _dot3(lh_ref[...], ll_ref[...], jnp.concatenate([a, b], axis=0))
    o_ref[...] = _lanes_to_rows(res, FFT_TN2).reshape(o_ref.shape)


def _fft1(lhs, x, a_of, b_of, g, p, d_b, natural, zero_lanes=0):
    half = FFT_N1 // 2
    lh, ll = _split(jnp.asarray(lhs))
    kern = functools.partial(_fft1_kernel, zero_lanes=zero_lanes, natural=natural)
    if natural:
        blk = lambda of: pl.BlockSpec((None, None, half, FFT_TN2, d_b), lambda q, j, t: of(q, j) + (0, t, 0))
    else:
        blk = lambda of: pl.BlockSpec((None, None, half, FFT_TN2 * d_b), lambda q, j, t: of(q, j) + (0, t))
    return pl.pallas_call(
        kern,
        grid=(g, p, FFT_N2 // FFT_TN2),
        in_specs=[
            pl.BlockSpec(lhs.shape, lambda q, j, t: (0, 0)),
            pl.BlockSpec(lhs.shape, lambda q, j, t: (0, 0)),
            blk(a_of), blk(b_of),
        ],
        out_specs=pl.BlockSpec((None, None, 2, FFT_N1, FFT_TN2, d_b), lambda q, j, t: (q, j, 0, 0, t, 0)),
        out_shape=jax.ShapeDtypeStruct((g, p, 2, FFT_N1, FFT_N2, d_b), F32),
        compiler_params=_cparams(("parallel", "parallel", "parallel")),
        name="fft1",
    )(lh, ll, x, x)


def _fft3_kernel(lh_ref, ll_ref, w_ref, o_ref):
    half = FFT_N1 // 2
    c = w_ref.shape[-1]
    w = _rows_to_lanes(w_ref[...].reshape(2 * FFT_N1, FFT_TN2, c))
    res = _dot3(lh_ref[...], ll_ref[...], w)
    o_ref[0] = _lanes_to_rows(res[:half], FFT_TN2)
    o_ref[1] = _lanes_to_rows(res[half:], FFT_TN2)


def _fft3(lhs, w6):
    g, p, _, _, _, d_b = w6.shape
    half = FFT_N1 // 2
    lh, ll = _split(jnp.asarray(lhs))
    return pl.pallas_call(
        _fft3_kernel,
        grid=(g, p, FFT_N2 // FFT_TN2),
        in_specs=[
            pl.BlockSpec(lhs.shape, lambda q, j, t: (0, 0)),
            pl.BlockSpec(lhs.shape, lambda q, j, t: (0, 0)),
            pl.BlockSpec((None, None, 2, FFT_N1, FFT_TN2, d_b), lambda q, j, t: (q, j, 0, 0, t, 0)),
        ],
        out_specs=pl.BlockSpec((None, 2, None, half, FFT_TN2, d_b), lambda q, j, t: (q, 0, j, 0, t, 0)),
        out_shape=jax.ShapeDtypeStruct((g, 2, p, half, FFT_N2, d_b), F32),
        compiler_params=_cparams(("parallel", "parallel", "parallel")),
        name="fft3",
    )(lh, ll, w6)


def _fft2_filter_kernel(gh_ref, gl_ref, x_ref, o_ref):
    ct = x_ref.shape[-1]
    for k in range(x_ref.shape[1]):
        z = _dot3(gh_ref[k], gl_ref[k], x_ref[:, k].reshape(2 * FFT_N2, ct))
        o_ref[:, k] = z.reshape(2, FFT_N2, ct)


def _fft2_filter(gfwd_hl, x1f, ct=256, kb=8):
    nd, _, _, _, d_b = x1f.shape
    gh, gl = gfwd_hl
    gspec = pl.BlockSpec((kb, 2 * FFT_N2, 2 * FFT_N2), lambda k, d, c: (k, 0, 0))
    xspec = pl.BlockSpec((None, 2, kb, FFT_N2, ct), lambda k, d, c: (d, 0, k, 0, c))
    return pl.pallas_call(
        _fft2_filter_kernel,
        grid=(FFT_N1 // kb, nd, d_b // ct),
        in_specs=[gspec, gspec, xspec],
        out_specs=xspec,
        out_shape=jax.ShapeDtypeStruct(x1f.shape, F32),
        compiler_params=_cparams(("parallel", "parallel", "parallel")),
        name="fft2_filter",
    )(gh, gl, x1f)


def _fft2_mix_kernel(gh_ref, gl_ref, ih_ref, il_ref, x_ref, k_ref, o_ref, *, nblocks):
    ct = x_ref.shape[-1]
    for k in range(x_ref.shape[2]):
        zs = []
        for j in range(nblocks):
            z = _dot3(gh_ref[k], gl_ref[k], x_ref[j, :, k].reshape(2 * FFT_N2, ct))
            zs.append((z[:FFT_N2], z[FFT_N2:]))
        for i in range(nblocks):
            yr = jnp.zeros((FFT_N2, ct), F32)
            yi = jnp.zeros((FFT_N2, ct), F32)
            for j in range(nblocks):
                d = i - j + nblocks - 1
                kr, ki = k_ref[d, 0, k], k_ref[d, 1, k]
                zr, zi = zs[j]
                yr = yr + kr * zr - ki * zi
                yi = yi + kr * zi + ki * zr
            w = _dot3(ih_ref[k], il_ref[k], jnp.concatenate([yr, yi], axis=0))
            o_ref[i, :, k] = w.reshape(2, FFT_N2, ct)


def _fft2_mix(gfwd_hl, ginv_hl, x1, kspec, ct=256):
    g, p, _, _, _, d_b = x1.shape
    nd = kspec.shape[0]
    kb = max(1, 8 // p)
    gspec = pl.BlockSpec((kb, 2 * FFT_N2, 2 * FFT_N2), lambda k, q, c: (k, 0, 0))
    xspec = pl.BlockSpec((None, p, 2, kb, FFT_N2, ct), lambda k, q, c: (q, 0, 0, k, 0, c))
    kern = functools.partial(_fft2_mix_kernel, nblocks=p)
    return pl.pallas_call(
        kern,
        grid=(FFT_N1 // kb, g, d_b // ct),
        in_specs=[gspec, gspec, gspec, gspec, xspec,
                  pl.BlockSpec((nd, 2, kb, FFT_N2, ct), lambda k, q, c: (0, 0, k, 0, c))],
        out_specs=xspec,
        out_shape=jax.ShapeDtypeStruct(x1.shape, F32),
        compiler_params=_cparams(("parallel", "parallel", "parallel")),
        name="fft2_mix",
    )(gfwd_hl[0], gfwd_hl[1], ginv_hl[0], ginv_hl[1], x1, kspec)


def _hyena_long_conv(g3, w1, b1, w2, b2, w3, b3, freq, w4):
    bsz, seq_len, d_b = g3.shape
    nblocks = seq_len // CONV_BLOCK
    half = FFT_N1 // 2
    lhs_data, lhs_real, lhs_inv, gfwd, ginv = _dft_tables()
    gfwd_hl = _split(jnp.asarray(gfwd))
    ginv_hl = _split(jnp.asarray(ginv))
    kext = _hyena_filter_blocks(seq_len, w1, b1, w2, b2, w3, b3, freq, w4)
    nd = 2 * nblocks - 1
    k1f = _fft1(lhs_real, kext[None], lambda q, j: (0, q + 1), lambda q, j: (0, q), nd, 1, d_b,
                natural=False, zero_lanes=d_b)
    kspec = _fft2_filter(gfwd_hl, k1f.reshape(nd, 2, FFT_N1, FFT_N2, d_b))
    g5 = g3.reshape(bsz, nblocks, half, FFT_N2, d_b)
    x1 = _fft1(lhs_data, g5, lambda q, j: (2 * q, j), lambda q, j: (2 * q + 1, j), bsz // 2, nblocks, d_b,
               natural=True)
    wmix = _fft2_mix(gfwd_hl, ginv_hl, x1, kspec)
    y6 = _fft3(lhs_inv, wmix)
    return y6.reshape(bsz, seq_len, d_b)


def _postmix_kernel(x_ref, yan_ref, x0_ref, g_ref, yc_ref, bias_ref, mgb_ref, wo_ref, n2_ref, x1_ref, xn_ref):
    d_a = yan_ref.shape[1]
    gg = g_ref[...]
    yb = x0_ref[...] * (yc_ref[...] + gg * bias_ref[...])
    ybn = _rms(yb, mgb_ref[...]).astype(BF16)
    y = jnp.dot(yan_ref[...], wo_ref[0:d_a, :], preferred_element_type=F32)
    y = y + jnp.dot(ybn, wo_ref[d_a:, :], preferred_element_type=F32)
    x1 = x_ref[...] + y
    x1_ref[...] = x1
    xn_ref[...] = _rms(x1, n2_ref[...]).astype(xn_ref.dtype)


def _postmix(x, yan, x0, g, yconv, bias, mix_g_b, w_out, norm2, tm=512):
    t, d = x.shape
    d_a = yan.shape[1]
    d_b = x0.shape[1]
    assert t % tm == 0, t
    row = lambda w: pl.BlockSpec((tm, w), lambda i: (i, 0))
    const = lambda *shape: pl.BlockSpec(shape, lambda i: (0,) * len(shape), pipeline_mode=pl.Buffered(1))
    return pl.pallas_call(
        _postmix_kernel,
        grid=(t // tm,),
        in_specs=[row(d), row(d_a), row(d_b), row(d_b), row(d_b), const(1, d_b), const(1, d_b),
                  const(d_a + d_b, d), const(1, d)],
        out_specs=[row(d), row(d)],
        out_shape=[jax.ShapeDtypeStruct((t, d), F32), jax.ShapeDtypeStruct((t, d), BF16)],
        compiler_params=_cparams(("parallel",)),
        name="postmix",
    )(x, yan, x0, g, yconv, bias, mix_g_b, w_out, norm2)


def _staircase():
    return [(i, j) for i in range(PEER_TOPK) for j in range(PEER_TOPK) if (i + 1) * (j + 1) <= PEER_TOPK]


ARGMAX_LANES = 2


def _stream_argmax(val_ref, tag_ref, nrows, prev, slab):
    neg = np.float32(-np.inf)
    best_v = [jnp.full(slab, neg, F32) for _ in range(ARGMAX_LANES)]
    best_i = [jnp.zeros(slab, I32) for _ in range(ARGMAX_LANES)]
    best_t = [jnp.zeros(slab, I32) for _ in range(ARGMAX_LANES)]
    for n in range(nrows):
        v = jnp.where(prev == n, neg, val_ref[n])
        val_ref[n] = v
        k = n % ARGMAX_LANES
        better = v > best_v[k]
        best_v[k] = jnp.maximum(v, best_v[k])
        best_i[k] = jnp.where(better, n, best_i[k])
        if tag_ref is not None:
            best_t[k] = jnp.where(better, tag_ref[n], best_t[k])
    width = ARGMAX_LANES
    while width > 1:
        width //= 2
        for k in range(width):
            va, vb = best_v[k], best_v[k + width]
            ia, ib = best_i[k], best_i[k + width]
            take_b = jnp.logical_or(vb > va, jnp.logical_and(vb == va, ib < ia))
            best_v[k] = jnp.where(take_b, vb, va)
            best_i[k] = jnp.where(take_b, ib, ia)
            best_t[k] = jnp.where(take_b, best_t[k + width], best_t[k])
    return best_v[0], best_i[0], best_t[0]


def _scores_kernel(xn_ref, wqt_ref, kbd_ref, s_ref, *, heads):
    tm = xn_ref.shape[0]
    hk = heads * HALF_KEY
    q_t = lax.dot_general(wqt_ref[...], xn_ref[...], (((1,), (1,)), ((), ())),
                          preferred_element_type=F32).astype(BF16)
    for half in range(2):
        s_t = jnp.dot(kbd_ref[half], q_t[half * hk:(half + 1) * hk], preferred_element_type=F32)
        s_ref[half] = s_t.reshape(N_KEYS, heads, tm)


def _peer_scores(xn, wqt, kbd, heads, tm=512):
    t, d = xn.shape
    assert t % tm == 0 and heads == V7X_SUBLANES, (t, heads)
    const = lambda *shape: pl.BlockSpec(shape, lambda i: (0,) * len(shape), pipeline_mode=pl.Buffered(1))
    kern = functools.partial(_scores_kernel, heads=heads)
    return pl.pallas_call(
        kern,
        grid=(t // tm,),
        in_specs=[pl.BlockSpec((tm, d), lambda i: (i, 0)), const(*wqt.shape), const(*kbd.shape)],
        out_specs=pl.BlockSpec((2, N_KEYS, heads, tm), lambda i: (0, 0, 0, i)),
        out_shape=jax.ShapeDtypeStruct((2, N_KEYS, heads, t), F32),
        compiler_params=_cparams(("parallel",)),
        name="peer_scores",
    )(xn, wqt, kbd)


SELECT_BLOCK = 128


def _peer_a_select_kernel(xn_ref, dt_ref, s_ref, a_ref, e_ref, gt_ref, s0_s, s1_s, c_s, ce_s, b_s, es_s,
                          *, heads):
    xn = xn_ref[...]
    ncol = V7X_MXU_WIDTH
    nchunk = dt_ref.shape[1] // ncol

    def matmul_chunk(c):
        cols = slice(c * ncol, (c + 1) * ncol)
        a_ref[:, cols] = jnp.dot(xn, dt_ref[:, cols], preferred_element_type=F32)

    slab = (heads, s_ref.shape[-1])
    no_pick = jnp.full(slab, -1, I32)
    half_s = (s0_s, s1_s)
    vals, idxs, prevs = ([], []), ([], []), [no_pick, no_pick]
    for half in range(2):
        half_s[half][...] = s_ref[half]
    l1_chunks = nchunk - nchunk // 3
    chunk = 0
    for r in range(PEER_TOPK):
        while chunk * PEER_TOPK < r * l1_chunks:
            matmul_chunk(chunk)
            chunk += 1
        for half in range(2):
            m, prevs[half], _ = _stream_argmax(half_s[half], None, N_KEYS, prevs[half], slab)
            vals[half].append(m)
            idxs[half].append(prevs[half])
    cands = _staircase()
    for p, (i, j) in enumerate(cands):
        c_s[p] = vals[0][i] + vals[1][j]
        ce_s[p] = idxs[0][i] * N_KEYS + idxs[1][j]
    prev = no_pick
    for r in range(PEER_TOPK):
        while (chunk - l1_chunks) * PEER_TOPK < r * (nchunk - l1_chunks) and chunk < nchunk:
            matmul_chunk(chunk)
            chunk += 1
        m, prev, e = _stream_argmax(c_s, ce_s, len(cands), prev, slab)
        es_s[r] = e
        b_s[r] = m
    while chunk < nchunk:
        matmul_chunk(chunk)
        chunk += 1
    best = b_s[...]
    ex = jnp.exp(best - jnp.max(best, axis=0, keepdims=True))
    gate = ex / jnp.sum(ex, axis=0, keepdims=True)
    gt_ref[...] = gate.reshape(PEER_TOPK * heads, slab[1]).T
    e_ref[...] = es_s[...].reshape(PEER_TOPK * heads, slab[1]).T


def _peer_a_select(xn, down_t, scores, heads, tm=1024, tn=2048):
    t, d = xn.shape
    n = down_t.shape[1]
    tb = SELECT_BLOCK
    nsel = PEER_TOPK * heads
    ncand = len(_staircase())
    ni = t // tm
    assert t % tm == 0 and n % tn == 0 and heads == V7X_SUBLANES, (t, n, heads)
    assert (n // tn) * ni * tb == t, "one selection block per grid step must cover all tokens"
    blk = lambda j, i: j * ni + i
    kern = functools.partial(_peer_a_select_kernel, heads=heads)
    return pl.pallas_call(
        kern,
        grid=(n // tn, ni),
        in_specs=[pl.BlockSpec((tm, d), lambda j, i: (i, 0)), pl.BlockSpec((d, tn), lambda j, i: (0, j)),
                  pl.BlockSpec((2, N_KEYS, heads, tb), lambda j, i: (0, 0, 0, blk(j, i)))],
        out_specs=[pl.BlockSpec((tm, tn), lambda j, i: (i, j)),
                   pl.BlockSpec((tb, nsel), lambda j, i: (blk(j, i), 0)),
                   pl.BlockSpec((tb, nsel), lambda j, i: (blk(j, i), 0))],
        out_shape=[jax.ShapeDtypeStruct((t, n), F32), jax.ShapeDtypeStruct((t, nsel), I32),
                   jax.ShapeDtypeStruct((t, nsel), F32)],
        scratch_shapes=[
            pltpu.VMEM((N_KEYS, heads, tb), F32),
            pltpu.VMEM((N_KEYS, heads, tb), F32),
            pltpu.VMEM((ncand, heads, tb), F32),
            pltpu.VMEM((ncand, heads, tb), I32),
            pltpu.VMEM((PEER_TOPK, heads, tb), F32),
            pltpu.VMEM((PEER_TOPK, heads, tb), I32),
        ],
        compiler_params=_cparams(("parallel", "parallel")),
        name="peer_a_select",
    )(xn, down_t, scores)


PEER_B_GROUP = 16


def _peer_b_kernel(a_ref, e_ref, gt_ref, m_ref):
    c_tok = a_ref.shape[0]
    nsel = e_ref.shape[1]
    grp = PEER_B_GROUP
    ngrp = c_tok // grp
    sub = lax.broadcasted_iota(I32, (grp, N_KEYS, nsel), 1)

    def rows_of(gi):
        return pl.ds(pl.multiple_of(gi * grp, grp), grp)

    def pick(gi):
        rows = rows_of(gi)
        e = e_ref[rows, :]
        ai = lax.shift_right_logical(e, KEY_SHIFT)
        bi = e & (N_KEYS - 1)
        accs = [jnp.zeros((grp, nsel), F32) for _ in range(4)]
        for a in range(N_KEYS):
            blk = a_ref[rows, a * N_KEYS:(a + 1) * N_KEYS]
            accs[a % 4] = jnp.where(ai == a, jnp.take_along_axis(blk, bi, axis=1), accs[a % 4])
        picked = (accs[0] + accs[1]) + (accs[2] + accs[3])
        return gt_ref[rows, :] * _gelu(picked), e

    def scatter(gi, w, e):
        ai = lax.shift_right_logical(e, KEY_SHIFT)
        bi = e & (N_KEYS - 1)
        w1t = jnp.where(sub == ai[:, None, :], w[:, None, :], 0.0).astype(BF16)
        e2t = jnp.where(sub == bi[:, None, :], 1.0, 0.0).astype(BF16)
        m3 = jnp.einsum("cas,cbs->cab", w1t, e2t, preferred_element_type=F32)
        mt = jnp.swapaxes(m3, 0, 1).astype(m_ref.dtype)
        rows = rows_of(gi)
        for a in range(N_KEYS):
            m_ref[rows, a * N_KEYS:(a + 1) * N_KEYS] = mt[a]

    def step(gi, carry):
        nxt = pick(gi)
        scatter(gi - 1, *carry)
        return nxt

    last = lax.fori_loop(1, ngrp, step, pick(0))
    scatter(ngrp - 1, *last)


def _peer_b(a, eidx, gate, tc=256):
    t, n = a.shape
    nsel = eidx.shape[1]
    assert t % tc == 0 and tc % PEER_B_GROUP == 0 and n == N_KEYS * N_KEYS and nsel == N_KEYS, (t, n, nsel)
    return pl.pallas_call(
        _peer_b_kernel,
        grid=(t // tc,),
        in_specs=[pl.BlockSpec((tc, n), lambda i: (i, 0)), pl.BlockSpec((tc, nsel), lambda i: (i, 0)),
                  pl.BlockSpec((tc, nsel), lambda i: (i, 0))],
        out_specs=pl.BlockSpec((tc, n), lambda i: (i, 0)),
        out_shape=jax.ShapeDtypeStruct((t, n), BF16),
        compiler_params=_cparams(("parallel",)),
        name="peer_b",
    )(a, eidx, gate)


def _peer_c_kernel(m_ref, up_ref, x1_ref, fn_ref, o_ref, acc_s):
    k = pl.program_id(1)

    @pl.when(k == 0)
    def _():
        acc_s[...] = jnp.zeros_like(acc_s)

    acc_s[...] += jnp.dot(m_ref[...], up_ref[...], preferred_element_type=F32)

    @pl.when(k == pl.num_programs(1) - 1)
    def _():
        o_ref[...] = _rms(x1_ref[...] + acc_s[...], fn_ref[...])


def _peer_c(m, up, x1, final_norm, tm=1024, tk=1024):
    t, n = m.shape
    d = up.shape[1]
    assert t % tm == 0 and n % tk == 0, (t, n)
    return pl.pallas_call(
        _peer_c_kernel,
        grid=(t // tm, n // tk),
        in_specs=[pl.BlockSpec((tm, tk), lambda i, k: (i, k)), pl.BlockSpec((tk, d), lambda i, k: (k, 0)),
                  pl.BlockSpec((tm, d), lambda i, k: (i, 0), pipeline_mode=pl.Buffered(1)),
                  pl.BlockSpec((1, d), lambda i, k: (0, 0))],
        out_specs=pl.BlockSpec((tm, d), lambda i, k: (i, 0)),
        out_shape=jax.ShapeDtypeStruct((t, d), F32),
        scratch_shapes=[pltpu.VMEM((tm, d), F32)],
        compiler_params=_cparams(("parallel", "arbitrary")),
        name="peer_c",
    )(m, up, x1, final_norm)


def _prep_weights(norm1, w_in, a_ln_g, a_ln_b, a_ws, a_bs, b_conv_w, b_conv_b, b_bias, mix_norm, w_out, norm2,
                  peer_wq, peer_k1, peer_k2, peer_down, peer_up, final_norm):
    d = w_in.shape[0]
    d_a = a_ln_g.shape[0]
    d_b = b_bias.shape[0]
    heads, n_keys, half_key = peer_k1.shape
    row = lambda v: v.reshape(1, -1).astype(F32)
    w = dict(
        norm1=row(norm1), norm2=row(norm2), final_norm=row(final_norm),
        w_u=w_in[:, :d_a].astype(BF16), w_v=w_in[:, d_a:2 * d_a].astype(BF16),
        w_b=w_in[:, 2 * d_a:].reshape(d, 3, d_b).transpose(1, 0, 2).astype(BF16),
        ln_g=row(a_ln_g), ln_b=row(a_ln_b),
        ws=a_ws.astype(BF16),
        bsb=jnp.broadcast_to(a_bs[:, :, None], a_bs.shape + (A_HEAD_DIM,)).astype(F32),
        conv_w=b_conv_w.reshape(3, 3, d_b).transpose(1, 0, 2).astype(F32),
        conv_b=b_conv_b.reshape(3, d_b).astype(F32),
        bias=row(b_bias), mix_g_a=row(mix_norm[:d_a]), mix_g_b=row(mix_norm[d_a:]),
        w_out=w_out.astype(BF16),
        down_t=peer_down.astype(BF16).T, up=peer_up.astype(BF16),
    )
    wq_t = peer_wq.T.reshape(heads, 2, half_key, d).transpose(1, 0, 2, 3).reshape(2 * heads * half_key, d)
    w["wq_t"] = wq_t.astype(BF16)
    eye = jnp.eye(heads, dtype=F32)
    kbd = [jnp.einsum("hnd,hg->nhgd", k, eye).reshape(n_keys * heads, heads * half_key) for k in (peer_k1, peer_k2)]
    w["kbd"] = jnp.stack(kbd).astype(BF16)
    w["heads"] = heads
    return w


def _trunk(x3, w, hf):
    bsz, seq_len, d = x3.shape
    t = bsz * seq_len
    x = x3.reshape(t, d)
    yan, x0, g = _inproj(x, seq_len, w["norm1"], w["w_u"], w["w_v"], w["w_b"], w["ln_g"], w["ln_b"],
                         w["ws"], w["bsb"], w["conv_w"], w["conv_b"], w["mix_g_a"])
    d_b = x0.shape[1]
    yconv = _hyena_long_conv(g.reshape(bsz, seq_len, d_b), *hf).reshape(t, d_b)
    x1, xn = _postmix(x, yan, x0, g, yconv, w["bias"], w["mix_g_b"], w["w_out"], w["norm2"])
    scores = _peer_scores(xn, w["wq_t"], w["kbd"], w["heads"])
    a, eidx, gate = _peer_a_select(xn, w["down_t"], scores, w["heads"])
    m = _peer_b(a, eidx, gate)
    out = _peer_c(m, w["up"], x1, w["final_norm"])
    return out.reshape(bsz, seq_len, d)


def kernel(x_prompt, x_sample, norm1, w_in, a_ln_g, a_ln_b, a_ws, a_bs, b_conv_w, b_conv_b, hf_w1, hf_b1, hf_w2, hf_b2, hf_w3, hf_b3, hf_freq, hf_w4, b_bias, mix_norm, w_out, norm2, peer_wq, peer_k1, peer_k2, peer_down, peer_up, final_norm):
    assert norm1.shape[0] == 1, "single-layer trunk"
    w = _prep_weights(norm1[0], w_in[0], a_ln_g[0], a_ln_b[0], a_ws[0], a_bs[0], b_conv_w[0], b_conv_b[0],
                      b_bias[0], mix_norm[0], w_out[0], norm2[0], peer_wq[0], peer_k1[0], peer_k2[0],
                      peer_down[0], peer_up[0], final_norm)
    hf = (hf_w1[0], hf_b1[0].reshape(1, -1), hf_w2[0], hf_b2[0].reshape(1, -1), hf_w3[0], hf_b3[0].reshape(1, -1),
          hf_freq[0], hf_w4[0])
    return (_trunk(x_prompt, w, hf), _trunk(x_sample, w, hf))
```

```python
import functools
import math

import numpy as np
import jax
import jax.numpy as jnp
from jax import lax
from jax.experimental import pallas as pl
from jax.experimental.pallas import tpu as pltpu

F32 = jnp.float32
BF16 = jnp.bfloat16
I32 = jnp.int32
U32 = jnp.uint32

EPS = 1e-6
V7X_LANES = 128
V7X_SUBLANES = 8
V7X_MXU_WIDTH = 256
V7X_VMEM_BYTES = 64 * 1024 * 1024
VMEM_LIMIT = V7X_VMEM_BYTES - 8 * 1024 * 1024

CHUNK = 128
A_HEAD_DIM = 128
EMB_DIM = 33
BANDS = (EMB_DIM - 1) // 2
DECAY_TARGET = 1e-2
FAST_DECAY_PCT = 0.3
SLOW_DECAY_PCT = 1.5
FFT_N1 = 64
FFT_N2 = 128
FFT_N = FFT_N1 * FFT_N2
CONV_BLOCK = FFT_N // 2
N_KEYS = 128
KEY_SHIFT = 7
PEER_TOPK = 16
HALF_KEY = 128


def _cparams(sem):
    return pltpu.CompilerParams(dimension_semantics=sem, vmem_limit_bytes=VMEM_LIMIT)


def _rms(xf, g):
    return xf * lax.rsqrt(jnp.mean(xf * xf, axis=-1, keepdims=True) + EPS) * g


def _gelu(x):
    return 0.5 * x * (1.0 + lax.erf(x * np.float32(math.sqrt(0.5))))


def _inproj_kernel(xp_ref, x_ref, xn_ref, n1_ref, wu_ref, wv_ref, wb_ref, lng_ref, lnb_ref,
                   ws_ref, bsb_ref, cw_ref, cb_ref, mga_ref,
                   yan_ref, x0_ref, g_ref, ya_s, zs_s, *, tiles_per_seq, cblk):
    tm = x_ref.shape[0]
    d_a = wu_ref.shape[1]
    d_b = wb_ref.shape[2]
    i = pl.program_id(0)
    not_first = (i % tiles_per_seq != 0).astype(F32)
    not_last = (i % tiles_per_seq != tiles_per_seq - 1).astype(F32)

    xcat = jnp.concatenate([xp_ref[...], x_ref[...], xn_ref[...]], axis=0)
    hcat = _rms(xcat, n1_ref[...])
    h_all = hcat.astype(BF16)
    h = hcat[V7X_SUBLANES:V7X_SUBLANES + tm].astype(BF16)

    u = _gelu(jnp.dot(h, wu_ref[...], preferred_element_type=F32))
    v = _gelu(jnp.dot(h, wv_ref[...], preferred_element_type=F32))
    mu = jnp.mean(v, axis=-1, keepdims=True)
    vc = v - mu
    var = jnp.mean(vc * vc, axis=-1, keepdims=True)
    vb = (vc * lax.rsqrt(var + EPS) * lng_ref[...] + lnb_ref[...]).astype(BF16)
    for c in range(tm // CHUNK):
        rows = slice(c * CHUNK, (c + 1) * CHUNK)
        for hd in range(d_a // A_HEAD_DIM):
            cols = slice(hd * A_HEAD_DIM, (hd + 1) * A_HEAD_DIM)
            mixed = jnp.dot(ws_ref[hd], vb[rows, cols], preferred_element_type=F32) + bsb_ref[hd]
            ya_s[rows, cols] = u[rows, cols] * mixed
    yan_ref[...] = _rms(ya_s[...], mga_ref[...]).astype(yan_ref.dtype)

    halo = V7X_SUBLANES
    for cb in range(d_b // cblk):
        cols = slice(cb * cblk, (cb + 1) * cblk)
        parts = []
        for p in range(3):
            z = jnp.dot(h_all, wb_ref[p, :, cols], preferred_element_type=F32)
            zs_s[...] = z
            zs_s[0:halo, :] = z[0:halo] * not_first
            zs_s[tm + halo:tm + 2 * halo, :] = z[tm + halo:tm + 2 * halo] * not_last
            w = cw_ref[p]
            zc = (cb_ref[p:p + 1, cols]
                  + zs_s[halo - 1:halo - 1 + tm, :] * w[0:1, cols]
                  + zs_s[halo:halo + tm, :] * w[1:2, cols]
                  + zs_s[halo + 1:halo + 1 + tm, :] * w[2:3, cols])
            parts.append(zc)
        x0_ref[:, cols] = parts[0]
        g_ref[:, cols] = parts[1] * parts[2]


def _inproj(x, seq_len, norm1, w_u, w_v, w_b, ln_g, ln_b, ws, bsb, conv_w, conv_b, mix_g_a, tm=512, cblk=256):
    t, d = x.shape
    d_a = w_u.shape[1]
    d_b = w_b.shape[2]
    assert seq_len % tm == 0 and t % seq_len == 0 and tm % CHUNK == 0 and d_b % cblk == 0, (t, seq_len)
    nblk = tm // V7X_SUBLANES
    last8 = t // V7X_SUBLANES - 1
    const = lambda *shape: pl.BlockSpec(shape, lambda i: (0,) * len(shape), pipeline_mode=pl.Buffered(1))
    kern = functools.partial(_inproj_kernel, tiles_per_seq=seq_len // tm, cblk=cblk)
    return pl.pallas_call(
        kern,
        grid=(t // tm,),
        in_specs=[
            pl.BlockSpec((V7X_SUBLANES, d), lambda i: (jnp.maximum(i * nblk - 1, 0), 0)),
            pl.BlockSpec((tm, d), lambda i: (i, 0)),
            pl.BlockSpec((V7X_SUBLANES, d), lambda i: (jnp.minimum((i + 1) * nblk, last8), 0)),
            const(1, d), const(d, d_a), const(d, d_a), const(3, d, d_b), const(1, d_a), const(1, d_a),
            const(*ws.shape), const(*bsb.shape), const(3, 3, d_b), const(3, d_b), const(1, d_a),
        ],
        out_specs=[
            pl.BlockSpec((tm, d_a), lambda i: (i, 0)),
            pl.BlockSpec((tm, d_b), lambda i: (i, 0)),
            pl.BlockSpec((tm, d_b), lambda i: (i, 0)),
        ],
        out_shape=[
            jax.ShapeDtypeStruct((t, d_a), BF16),
            jax.ShapeDtypeStruct((t, d_b), F32),
            jax.ShapeDtypeStruct((t, d_b), F32),
        ],
        scratch_shapes=[pltpu.VMEM((tm, d_a), F32), pltpu.VMEM((tm + 2 * V7X_SUBLANES, cblk), F32)],
        compiler_params=_cparams(("parallel",)),
        name="inproj",
    )(x, x, x, norm1, w_u, w_v, w_b, ln_g, ln_b, ws, bsb, conv_w, conv_b, mix_g_a)


def _filt_kernel(fr_ref, dl_ref, w1_ref, b1_ref, w2_ref, b2_ref, w3_ref, b3_ref, fq_ref, w4_ref, o_ref,
                 *, seq_len, nblocks, n2_per_step):
    half = FFT_N1 // 2
    tr = half * n2_per_step
    e = pl.program_id(0) - nblocks
    r = lax.broadcasted_iota(I32, (tr, 1), 0)
    n2 = pl.program_id(1) * n2_per_step + r // half
    trow = (r % half) * FFT_N2 + n2
    lag = e * CONV_BLOCK + trow
    pos = jnp.abs(lag).astype(F32)
    valid = jnp.abs(lag) <= seq_len - 1
    t = pos / np.float32(max(seq_len - 1, 1))
    fw = np.float32(2.0 * math.pi / seq_len) * pos * fr_ref[...]
    hi = lax.Precision.HIGHEST
    w1 = w1_ref[...]
    z1 = (t * w1[0:1, :]
          + jnp.dot(jnp.cos(fw), w1[1:1 + BANDS, :], precision=hi, preferred_element_type=F32)
          + jnp.dot(-jnp.sin(fw), w1[1 + BANDS:, :], precision=hi, preferred_element_type=F32))
    fq = fq_ref[...]
    h = jnp.sin(fq[0:1, :] * (z1 + b1_ref[...]))
    h = jnp.sin(fq[1:2, :] * (jnp.dot(h, w2_ref[...], precision=hi, preferred_element_type=F32) + b2_ref[...]))
    h = jnp.sin(fq[2:3, :] * (jnp.dot(h, w3_ref[...], precision=hi, preferred_element_type=F32) + b3_ref[...]))
    h4 = jnp.dot(h, w4_ref[...], precision=hi, preferred_element_type=F32)
    window = jnp.exp(-t * dl_ref[...])
    res = jnp.where(valid, h4 * window, 0.0)
    d_b = res.shape[1]
    for i in range(n2_per_step):
        o_ref[:, i * d_b:(i + 1) * d_b] = res[i * half:(i + 1) * half]


def _hyena_filter_blocks(seq_len, w1, b1, w2, b2, w3, b3, freq, w4, n2_per_step=16):
    nblocks = seq_len // CONV_BLOCK
    width = w1.shape[1]
    d_b = w4.shape[1] // 2
    fr = jnp.asarray(np.linspace(1e-4, BANDS - 1, BANDS, dtype=np.float32)[None, :])
    min_decay = math.log(DECAY_TARGET) / SLOW_DECAY_PCT
    max_decay = math.log(DECAY_TARGET) / FAST_DECAY_PCT
    deltas = jnp.asarray(np.abs(np.linspace(min_decay, max_decay, d_b, dtype=np.float32))[None, :])
    w4h = w4.reshape(width, 2, d_b).transpose(1, 0, 2)
    const = lambda *shape: pl.BlockSpec(shape, lambda a, b: (0,) * len(shape))
    half = FFT_N1 // 2
    kern = functools.partial(_filt_kernel, seq_len=seq_len, nblocks=nblocks, n2_per_step=n2_per_step)
    return pl.pallas_call(
        kern,
        grid=(2 * nblocks, FFT_N2 // n2_per_step),
        in_specs=[
            const(1, BANDS), const(1, d_b), const(EMB_DIM, width), const(1, width), const(width, width),
            const(1, width), const(width, width), const(1, width), const(3, width),
            pl.BlockSpec((None, width, d_b), lambda ei, r: (jnp.where(ei >= nblocks, 0, 1), 0, 0)),
        ],
        out_specs=pl.BlockSpec((None, half, n2_per_step * d_b), lambda ei, r: (ei, 0, r)),
        out_shape=jax.ShapeDtypeStruct((2 * nblocks, half, FFT_N2 * d_b), F32),
        compiler_params=_cparams(("parallel", "parallel")),
        name="hyena_filter",
    )(fr, deltas, w1, b1, w2, b2, w3, b3, freq, w4h)


def _dft_tables():
    n1 = np.arange(FFT_N1)
    f64 = np.exp(-2j * np.pi * np.outer(n1, n1) / FFT_N1)
    half = FFT_N1 // 2
    fr, fi = f64.real, f64.imag
    lhs_data = np.block([[fr[:, :half], -fi[:, :half]], [fi[:, :half], fr[:, :half]]])
    lhs_real = np.concatenate([fr, fi], axis=0)
    gr, gi = fr[:half, :] / FFT_N, -fi[:half, :] / FFT_N
    lhs_inv = np.block([[gr, -gi], [gi, gr]])
    k1 = np.arange(FFT_N1)[:, None, None]
    k2 = np.arange(FFT_N2)[None, :, None]
    n2 = np.arange(FFT_N2)[None, None, :]
    g = np.exp(-2j * np.pi * (n2 * (k1 + FFT_N1 * k2) % FFT_N) / FFT_N)
    gfwd = np.concatenate([np.concatenate([g.real, -g.imag], axis=2),
                           np.concatenate([g.imag, g.real], axis=2)], axis=1)
    ht = np.conj(np.transpose(g, (0, 2, 1)))
    ginv = np.concatenate([np.concatenate([ht.real, -ht.imag], axis=2),
                           np.concatenate([ht.imag, ht.real], axis=2)], axis=1)
    as32 = lambda a: np.asarray(a, dtype=np.float32)
    return as32(lhs_data), as32(lhs_real), as32(lhs_inv), as32(gfwd), as32(ginv)


def _split(x):
    hi = x.astype(BF16)
    lo = (x - hi.astype(F32)).astype(BF16)
    return hi, lo


def _dot3(a_hi, a_lo, b):
    b_hi, b_lo = _split(b)
    acc = jnp.dot(a_hi, b_hi, preferred_element_type=F32)
    acc = acc + jnp.dot(a_hi, b_lo, preferred_element_type=F32)
    acc = acc + jnp.dot(a_lo, b_hi, preferred_element_type=F32)
    return acc


FFT_TN2 = 8


def _rows_to_lanes(x):
    xt = jnp.swapaxes(x, 0, 1)
    return jnp.concatenate([xt[i] for i in range(xt.shape[0])], axis=1)


def _lanes_to_rows(x, s):
    c = x.shape[1] // s
    return jnp.swapaxes(jnp.stack([x[:, i * c:(i + 1) * c] for i in range(s)], axis=0), 0, 1)


def _fft1_kernel(lh_ref, ll_ref, a_ref, b_ref, o_ref, *, zero_lanes, natural):
    if natural:
        a = _rows_to_lanes(a_ref[...])
        b = _rows_to_lanes(b_ref[...])
    else:
        a = a_ref[...]
        b = b_ref[...]
    if zero_lanes:
        row = lax.broadcasted_iota(I32, b.shape, 0)
        lane = lax.broadcasted_iota(I32, b.shape, 1) + pl.program_id(2) * b.shape[1]
        b = jnp.where(jnp.logical_and(row == 0, lane < zero_lanes), 0.0, b)
    res = _dot3(lh_ref[...], ll_ref[...], jnp.concatenate([a, b], axis=0))
    o_ref[...] = _lanes_to_rows(res, FFT_TN2).reshape(o_ref.shape)


def _fft1(lhs, x, a_of, b_of, g, p, d_b, natural, zero_lanes=0):
    half = FFT_N1 // 2
    lh, ll = _split(jnp.asarray(lhs))
    kern = functools.partial(_fft1_kernel, zero_lanes=zero_lanes, natural=natural)
    if natural:
        blk = lambda of: pl.BlockSpec((None, None, half, FFT_TN2, d_b), lambda q, j, t: of(q, j) + (0, t, 0))
    else:
        blk = lambda of: pl.BlockSpec((None, None, half, FFT_TN2 * d_b), lambda q, j, t: of(q, j) + (0, t))
    return pl.pallas_call(
        kern,
        grid=(g, p, FFT_N2 // FFT_TN2),
        in_specs=[
            pl.BlockSpec(lhs.shape, lambda q, j, t: (0, 0)),
            pl.BlockSpec(lhs.shape, lambda q, j, t: (0, 0)),
            blk(a_of), blk(b_of),
        ],
        out_specs=pl.BlockSpec((None, None, 2, FFT_N1, FFT_TN2, d_b), lambda q, j, t: (q, j, 0, 0, t, 0)),
        out_shape=jax.ShapeDtypeStruct((g, p, 2, FFT_N1, FFT_N2, d_b), F32),
        compiler_params=_cparams(("parallel", "parallel", "parallel")),
        name="fft1",
    )(lh, ll, x, x)


def _fft3_kernel(lh_ref, ll_ref, w_ref, o_ref):
    half = FFT_N1 // 2
    c = w_ref.shape[-1]
    w = _rows_to_lanes(w_ref[...].reshape(2 * FFT_N1, FFT_TN2, c))
    res = _dot3(lh_ref[...], ll_ref[...], w)
    o_ref[0] = _lanes_to_rows(res[:half], FFT_TN2)
    o_ref[1] = _lanes_to_rows(res[half:], FFT_TN2)


def _fft3(lhs, w6):
    g, p, _, _, _, d_b = w6.shape
    half = FFT_N1 // 2
    lh, ll = _split(jnp.asarray(lhs))
    return pl.pallas_call(
        _fft3_kernel,
        grid=(g, p, FFT_N2 // FFT_TN2),
        in_specs=[
            pl.BlockSpec(lhs.shape, lambda q, j, t: (0, 0)),
            pl.BlockSpec(lhs.shape, lambda q, j, t: (0, 0)),
            pl.BlockSpec((None, None, 2, FFT_N1, FFT_TN2, d_b), lambda q, j, t: (q, j, 0, 0, t, 0)),
        ],
        out_specs=pl.BlockSpec((None, 2, None, half, FFT_TN2, d_b), lambda q, j, t: (q, 0, j, 0, t, 0)),
        out_shape=jax.ShapeDtypeStruct((g, 2, p, half, FFT_N2, d_b), F32),
        compiler_params=_cparams(("parallel", "parallel", "parallel")),
        name="fft3",
    )(lh, ll, w6)


def _fft2_filter_kernel(gh_ref, gl_ref, x_ref, o_ref):
    ct = x_ref.shape[-1]
    for k in range(x_ref.shape[1]):
        z = _dot3(gh_ref[k], gl_ref[k], x_ref[:, k].reshape(2 * FFT_N2, ct))
        o_ref[:, k] = z.reshape(2, FFT_N2, ct)


def _fft2_filter(gfwd_hl, x1f, ct=256, kb=8):
    nd, _, _, _, d_b = x1f.shape
    gh, gl = gfwd_hl
    gspec = pl.BlockSpec((kb, 2 * FFT_N2, 2 * FFT_N2), lambda k, d, c: (k, 0, 0))
    xspec = pl.BlockSpec((None, 2, kb, FFT_N2, ct), lambda k, d, c: (d, 0, k, 0, c))
    return pl.pallas_call(
        _fft2_filter_kernel,
        grid=(FFT_N1 // kb, nd, d_b // ct),
        in_specs=[gspec, gspec, xspec],
        out_specs=xspec,
        out_shape=jax.ShapeDtypeStruct(x1f.shape, F32),
        compiler_params=_cparams(("parallel", "parallel", "parallel")),
        name="fft2_filter",
    )(gh, gl, x1f)


def _fft2_mix_kernel(gh_ref, gl_ref, ih_ref, il_ref, x_ref, k_ref, o_ref, *, nblocks):
    ct = x_ref.shape[-1]
    for k in range(x_ref.shape[2]):
        zs = []
        for j in range(nblocks):
            z = _dot3(gh_ref[k], gl_ref[k], x_ref[j, :, k].reshape(2 * FFT_N2, ct))
            zs.append((z[:FFT_N2], z[FFT_N2:]))
        for i in range(nblocks):
            yr = jnp.zeros((FFT_N2, ct), F32)
            yi = jnp.zeros((FFT_N2, ct), F32)
            for j in range(nblocks):
                d = i - j + nblocks - 1
                kr, ki = k_ref[d, 0, k], k_ref[d, 1, k]
                zr, zi = zs[j]
                yr = yr + kr * zr - ki * zi
                yi = yi + kr * zi + ki * zr
            w = _dot3(ih_ref[k], il_ref[k], jnp.concatenate([yr, yi], axis=0))
            o_ref[i, :, k] = w.reshape(2, FFT_N2, ct)


def _fft2_mix(gfwd_hl, ginv_hl, x1, kspec, ct=256):
    g, p, _, _, _, d_b = x1.shape
    nd = kspec.shape[0]
    kb = max(1, 8 // p)
    gspec = pl.BlockSpec((kb, 2 * FFT_N2, 2 * FFT_N2), lambda k, q, c: (k, 0, 0))
    xspec = pl.BlockSpec((None, p, 2, kb, FFT_N2, ct), lambda k, q, c: (q, 0, 0, k, 0, c))
    kern = functools.partial(_fft2_mix_kernel, nblocks=p)
    return pl.pallas_call(
        kern,
        grid=(FFT_N1 // kb, g, d_b // ct),
        in_specs=[gspec, gspec, gspec, gspec, xspec,
                  pl.BlockSpec((nd, 2, kb, FFT_N2, ct), lambda k, q, c: (0, 0, k, 0, c))],
        out_specs=xspec,
        out_shape=jax.ShapeDtypeStruct(x1.shape, F32),
        compiler_params=_cparams(("parallel", "parallel", "parallel")),
        name="fft2_mix",
    )(gfwd_hl[0], gfwd_hl[1], ginv_hl[0], ginv_hl[1], x1, kspec)


def _hyena_long_conv(g3, w1, b1, w2, b2, w3, b3, freq, w4):
    bsz, seq_len, d_b = g3.shape
    nblocks = seq_len // CONV_BLOCK
    half = FFT_N1 // 2
    lhs_data, lhs_real, lhs_inv, gfwd, ginv = _dft_tables()
    gfwd_hl = _split(jnp.asarray(gfwd))
    ginv_hl = _split(jnp.asarray(ginv))
    kext = _hyena_filter_blocks(seq_len, w1, b1, w2, b2, w3, b3, freq, w4)
    nd = 2 * nblocks - 1
    k1f = _fft1(lhs_real, kext[None], lambda q, j: (0, q + 1), lambda q, j: (0, q), nd, 1, d_b,
                natural=False, zero_lanes=d_b)
    kspec = _fft2_filter(gfwd_hl, k1f.reshape(nd, 2, FFT_N1, FFT_N2, d_b))
    g5 = g3.reshape(bsz, nblocks, half, FFT_N2, d_b)
    x1 = _fft1(lhs_data, g5, lambda q, j: (2 * q, j), lambda q, j: (2 * q + 1, j), bsz // 2, nblocks, d_b,
               natural=True)
    wmix = _fft2_mix(gfwd_hl, ginv_hl, x1, kspec)
    y6 = _fft3(lhs_inv, wmix)
    return y6.reshape(bsz, seq_len, d_b)


def _postmix_kernel(x_ref, yan_ref, x0_ref, g_ref, yc_ref, bias_ref, mgb_ref, wo_ref, n2_ref, x1_ref, xn_ref):
    d_a = yan_ref.shape[1]
    gg = g_ref[...]
    yb = x0_ref[...] * (yc_ref[...] + gg * bias_ref[...])
    ybn = _rms(yb, mgb_ref[...]).astype(BF16)
    y = jnp.dot(yan_ref[...], wo_ref[0:d_a, :], preferred_element_type=F32)
    y = y + jnp.dot(ybn, wo_ref[d_a:, :], preferred_element_type=F32)
    x1 = x_ref[...] + y
    x1_ref[...] = x1
    xn_ref[...] = _rms(x1, n2_ref[...]).astype(xn_ref.dtype)


def _postmix(x, yan, x0, g, yconv, bias, mix_g_b, w_out, norm2, tm=512):
    t, d = x.shape
    d_a = yan.shape[1]
    d_b = x0.shape[1]
    assert t % tm == 0, t
    row = lambda w: pl.BlockSpec((tm, w), lambda i: (i, 0))
    const = lambda *shape: pl.BlockSpec(shape, lambda i: (0,) * len(shape), pipeline_mode=pl.Buffered(1))
    return pl.pallas_call(
        _postmix_kernel,
        grid=(t // tm,),
        in_specs=[row(d), row(d_a), row(d_b), row(d_b), row(d_b), const(1, d_b), const(1, d_b),
                  const(d_a + d_b, d), const(1, d)],
        out_specs=[row(d), row(d)],
        out_shape=[jax.ShapeDtypeStruct((t, d), F32), jax.ShapeDtypeStruct((t, d), BF16)],
        compiler_params=_cparams(("parallel",)),
        name="postmix",
    )(x, yan, x0, g, yconv, bias, mix_g_b, w_out, norm2)


def _staircase():
    return [(i, j) for i in range(PEER_TOPK) for j in range(PEER_TOPK) if (i + 1) * (j + 1) <= PEER_TOPK]


ARGMAX_LANES = 2


def _stream_argmax(val_ref, tag_ref, nrows, prev, slab):
    neg = np.float32(-np.inf)
    best_v = [jnp.full(slab, neg, F32) for _ in range(ARGMAX_LANES)]
    best_i = [jnp.zeros(slab, I32) for _ in range(ARGMAX_LANES)]
    best_t = [jnp.zeros(slab, I32) for _ in range(ARGMAX_LANES)]
    for n in range(nrows):
        v = jnp.where(prev == n, neg, val_ref[n])
        val_ref[n] = v
        k = n % ARGMAX_LANES
        better = v > best_v[k]
        best_v[k] = jnp.maximum(v, best_v[k])
        best_i[k] = jnp.where(better, n, best_i[k])
        if tag_ref is not None:
            best_t[k] = jnp.where(better, tag_ref[n], best_t[k])
    width = ARGMAX_LANES
    while width > 1:
        width //= 2
        for k in range(width):
            va, vb = best_v[k], best_v[k + width]
            ia, ib = best_i[k], best_i[k + width]
            take_b = jnp.logical_or(vb > va, jnp.logical_and(vb == va, ib < ia))
            best_v[k] = jnp.where(take_b, vb, va)
            best_i[k] = jnp.where(take_b, ib, ia)
            best_t[k] = jnp.where(take_b, best_t[k + width], best_t[k])
    return best_v[0], best_i[0], best_t[0]


def _scores_kernel(xn_ref, wqt_ref, kbd_ref, s_ref, *, heads):
    tm = xn_ref.shape[0]
    hk = heads * HALF_KEY
    q_t = lax.dot_general(wqt_ref[...], xn_ref[...], (((1,), (1,)), ((), ())),
                          preferred_element_type=F32).astype(BF16)
    for half in range(2):
        s_t = jnp.dot(kbd_ref[half], q_t[half * hk:(half + 1) * hk], preferred_element_type=F32)
        s_ref[half] = s_t.reshape(N_KEYS, heads, tm)


def _peer_scores(xn, wqt, kbd, heads, tm=512):
    t, d = xn.shape
    assert t % tm == 0 and heads == V7X_SUBLANES, (t, heads)
    const = lambda *shape: pl.BlockSpec(shape, lambda i: (0,) * len(shape), pipeline_mode=pl.Buffered(1))
    kern = functools.partial(_scores_kernel, heads=heads)
    return pl.pallas_call(
        kern,
        grid=(t // tm,),
        in_specs=[pl.BlockSpec((tm, d), lambda i: (i, 0)), const(*wqt.shape), const(*kbd.shape)],
        out_specs=pl.BlockSpec((2, N_KEYS, heads, tm), lambda i: (0, 0, 0, i)),
        out_shape=jax.ShapeDtypeStruct((2, N_KEYS, heads, t), F32),
        compiler_params=_cparams(("parallel",)),
        name="peer_scores",
    )(xn, wqt, kbd)


SELECT_BLOCK = 128


def _peer_a_select_kernel(xn_ref, dlo_ref, dhi_ref, s_ref, a_ref, e_ref, gt_ref, s0_s, s1_s, c_s, ce_s, b_s, es_s,
                          *, heads):
    xn = xn_ref[...]
    a_lo = jnp.dot(xn, dlo_ref[...], preferred_element_type=F32)
    a_hi = jnp.dot(xn, dhi_ref[...], preferred_element_type=F32)
    a_ref[...] = pltpu.pack_elementwise([a_lo, a_hi], packed_dtype=BF16)

    slab = (heads, s_ref.shape[-1])
    no_pick = jnp.full(slab, -1, I32)
    half_s = (s0_s, s1_s)
    vals, idxs, prevs = ([], []), ([], []), [no_pick, no_pick]
    for half in range(2):
        half_s[half][...] = s_ref[half]
    for r in range(PEER_TOPK):
        for half in range(2):
            m, prevs[half], _ = _stream_argmax(half_s[half], None, N_KEYS, prevs[half], slab)
            vals[half].append(m)
            idxs[half].append(prevs[half])
    cands = _staircase()
    for p, (i, j) in enumerate(cands):
        c_s[p] = vals[0][i] + vals[1][j]
        ce_s[p] = idxs[0][i] * N_KEYS + idxs[1][j]
    prev = no_pick
    for r in range(PEER_TOPK):
        m, prev, e = _stream_argmax(c_s, ce_s, len(cands), prev, slab)
        es_s[r] = e
        b_s[r] = m
    best = b_s[...]
    ex = jnp.exp(best - jnp.max(best, axis=0, keepdims=True))
    gate = ex / jnp.sum(ex, axis=0, keepdims=True)
    gt_ref[...] = gate.reshape(PEER_TOPK * heads, slab[1]).T
    e_ref[...] = es_s[...].reshape(PEER_TOPK * heads, slab[1]).T


def _peer_a_select(xn, down_t, scores, heads, tm=1024, tn=1024):
    t, d = xn.shape
    n = down_t.shape[1]
    tb = SELECT_BLOCK
    nsel = PEER_TOPK * heads
    ncand = len(_staircase())
    ni = t // tm
    nj = n // (2 * tn)
    assert t % tm == 0 and n % (2 * tn) == 0 and heads == V7X_SUBLANES, (t, n, heads)
    assert nj * ni * tb == t, "one selection block per grid step must cover all tokens"
    blk = lambda j, i: j * ni + i
    kern = functools.partial(_peer_a_select_kernel, heads=heads)
    return pl.pallas_call(
        kern,
        grid=(nj, ni),
        in_specs=[pl.BlockSpec((tm, d), lambda j, i: (i, 0)),
                  pl.BlockSpec((d, tn), lambda j, i: (0, j)), pl.BlockSpec((d, tn), lambda j, i: (0, j + nj)),
                  pl.BlockSpec((2, N_KEYS, heads, tb), lambda j, i: (0, 0, 0, blk(j, i)))],
        out_specs=[pl.BlockSpec((tm, tn), lambda j, i: (i, j)),
                   pl.BlockSpec((tb, nsel), lambda j, i: (blk(j, i), 0)),
                   pl.BlockSpec((tb, nsel), lambda j, i: (blk(j, i), 0))],
        out_shape=[jax.ShapeDtypeStruct((t, n // 2), U32), jax.ShapeDtypeStruct((t, nsel), I32),
                   jax.ShapeDtypeStruct((t, nsel), F32)],
        scratch_shapes=[
            pltpu.VMEM((N_KEYS, heads, tb), F32),
            pltpu.VMEM((N_KEYS, heads, tb), F32),
            pltpu.VMEM((ncand, heads, tb), F32),
            pltpu.VMEM((ncand, heads, tb), I32),
            pltpu.VMEM((PEER_TOPK, heads, tb), F32),
            pltpu.VMEM((PEER_TOPK, heads, tb), I32),
        ],
        compiler_params=_cparams(("parallel", "parallel")),
        name="peer_a_select",
    )(xn, down_t, down_t, scores)


PEER_B_GROUP = 16


def _peer_b_kernel(a_ref, e_ref, gt_ref, m_ref):
    c_tok = a_ref.shape[0]
    nsel = e_ref.shape[1]
    grp = PEER_B_GROUP
    ngrp = c_tok // grp
    sub = lax.broadcasted_iota(I32, (grp, N_KEYS, nsel), 1)

    def rows_of(gi):
        return pl.ds(pl.multiple_of(gi * grp, grp), grp)

    def pick(gi):
        rows = rows_of(gi)
        e = e_ref[rows, :]
        ai = lax.shift_right_logical(e, KEY_SHIFT)
        bi = e & (N_KEYS - 1)
        a_word = ai & (N_KEYS // 2 - 1)
        accs = [jnp.zeros((grp, nsel), U32) for _ in range(4)]
        for a in range(N_KEYS // 2):
            blk = a_ref[rows, a * N_KEYS:(a + 1) * N_KEYS]
            accs[a % 4] = jnp.where(a_word == a, jnp.take_along_axis(blk, bi, axis=1), accs[a % 4])
        word = (accs[0] | accs[1]) | (accs[2] | accs[3])
        lo = pltpu.unpack_elementwise(word, index=0, packed_dtype=BF16, unpacked_dtype=F32)
        hi = pltpu.unpack_elementwise(word, index=1, packed_dtype=BF16, unpacked_dtype=F32)
        picked = jnp.where(ai >= N_KEYS // 2, hi, lo)
        return gt_ref[rows, :] * _gelu(picked), e

    def scatter(gi, w, e):
        ai = lax.shift_right_logical(e, KEY_SHIFT)
        bi = e & (N_KEYS - 1)
        w1t = jnp.where(sub == ai[:, None, :], w[:, None, :], 0.0).astype(BF16)
        e2t = jnp.where(sub == bi[:, None, :], 1.0, 0.0).astype(BF16)
        m3 = jnp.einsum("cas,cbs->cab", w1t, e2t, preferred_element_type=F32)
        mt = jnp.swapaxes(m3, 0, 1).astype(m_ref.dtype)
        rows = rows_of(gi)
        for a in range(N_KEYS):
            m_ref[rows, a * N_KEYS:(a + 1) * N_KEYS] = mt[a]

    def step(gi, carry):
        nxt = pick(gi)
        scatter(gi - 1, *carry)
        return nxt

    last = lax.fori_loop(1, ngrp, step, pick(0))
    scatter(ngrp - 1, *last)


def _peer_b(a2, eidx, gate, tc=256):
    t = a2.shape[0]
    n = 2 * a2.shape[1]
    nsel = eidx.shape[1]
    assert t % tc == 0 and tc % PEER_B_GROUP == 0 and n == N_KEYS * N_KEYS and nsel == N_KEYS, (t, n, nsel)
    return pl.pallas_call(
        _peer_b_kernel,
        grid=(t // tc,),
        in_specs=[pl.BlockSpec((tc, n // 2), lambda i: (i, 0)), pl.BlockSpec((tc, nsel), lambda i: (i, 0)),
                  pl.BlockSpec((tc, nsel), lambda i: (i, 0))],
        out_specs=pl.BlockSpec((tc, n), lambda i: (i, 0)),
        out_shape=jax.ShapeDtypeStruct((t, n), BF16),
        compiler_params=_cparams(("parallel",)),
        name="peer_b",
    )(a2, eidx, gate)


def _peer_c_kernel(m_ref, up_ref, x1_ref, fn_ref, o_ref, acc_s):
    k = pl.program_id(1)

    @pl.when(k == 0)
    def _():
        acc_s[...] = jnp.zeros_like(acc_s)

    acc_s[...] += jnp.dot(m_ref[...], up_ref[...], preferred_element_type=F32)

    @pl.when(k == pl.num_programs(1) - 1)
    def _():
        o_ref[...] = _rms(x1_ref[...] + acc_s[...], fn_ref[...])


def _peer_c(m, up, x1, final_norm, tm=1024, tk=1024):
    t, n = m.shape
    d = up.shape[1]
    assert t % tm == 0 and n % tk == 0, (t, n)
    return pl.pallas_call(
        _peer_c_kernel,
        grid=(t // tm, n // tk),
        in_specs=[pl.BlockSpec((tm, tk), lambda i, k: (i, k)), pl.BlockSpec((tk, d), lambda i, k: (k, 0)),
                  pl.BlockSpec((tm, d), lambda i, k: (i, 0), pipeline_mode=pl.Buffered(1)),
                  pl.BlockSpec((1, d), lambda i, k: (0, 0))],
        out_specs=pl.BlockSpec((tm, d), lambda i, k: (i, 0)),
        out_shape=jax.ShapeDtypeStruct((t, d), F32),
        scratch_shapes=[pltpu.VMEM((tm, d), F32)],
        compiler_params=_cparams(("parallel", "arbitrary")),
        name="peer_c",
    )(m, up, x1, final_norm)


def _prep_weights(norm1, w_in, a_ln_g, a_ln_b, a_ws, a_bs, b_conv_w, b_conv_b, b_bias, mix_norm, w_out, norm2,
                  peer_wq, peer_k1, peer_k2, peer_down, peer_up, final_norm):
    d = w_in.shape[0]
    d_a = a_ln_g.shape[0]
    d_b = b_bias.shape[0]
    heads, n_keys, half_key = peer_k1.shape
    row = lambda v: v.reshape(1, -1).astype(F32)
    w = dict(
        norm1=row(norm1), norm2=row(norm2), final_norm=row(final_norm),
        w_u=w_in[:, :d_a].astype(BF16), w_v=w_in[:, d_a:2 * d_a].astype(BF16),
        w_b=w_in[:, 2 * d_a:].reshape(d, 3, d_b).transpose(1, 0, 2).astype(BF16),
        ln_g=row(a_ln_g), ln_b=row(a_ln_b),
        ws=a_ws.astype(BF16),
        bsb=jnp.broadcast_to(a_bs[:, :, None], a_bs.shape + (A_HEAD_DIM,)).astype(F32),
        conv_w=b_conv_w.reshape(3, 3, d_b).transpose(1, 0, 2).astype(F32),
        conv_b=b_conv_b.reshape(3, d_b).astype(F32),
        bias=row(b_bias), mix_g_a=row(mix_norm[:d_a]), mix_g_b=row(mix_norm[d_a:]),
        w_out=w_out.astype(BF16),
        down_t=peer_down.astype(BF16).T, up=peer_up.astype(BF16),
    )
    wq_t = peer_wq.T.reshape(heads, 2, half_key, d).transpose(1, 0, 2, 3).reshape(2 * heads * half_key, d)
    w["wq_t"] = wq_t.astype(BF16)
    eye = jnp.eye(heads, dtype=F32)
    kbd = [jnp.einsum("hnd,hg->nhgd", k, eye).reshape(n_keys * heads, heads * half_key) for k in (peer_k1, peer_k2)]
    w["kbd"] = jnp.stack(kbd).astype(BF16)
    w["heads"] = heads
    return w


def _trunk(x3, w, hf):
    bsz, seq_len, d = x3.shape
    t = bsz * seq_len
    x = x3.reshape(t, d)
    yan, x0, g = _inproj(x, seq_len, w["norm1"], w["w_u"], w["w_v"], w["w_b"], w["ln_g"], w["ln_b"],
                         w["ws"], w["bsb"], w["conv_w"], w["conv_b"], w["mix_g_a"])
    d_b = x0.shape[1]
    yconv = _hyena_long_conv(g.reshape(bsz, seq_len, d_b), *hf).reshape(t, d_b)
    x1, xn = _postmix(x, yan, x0, g, yconv, w["bias"], w["mix_g_b"], w["w_out"], w["norm2"])
    scores = _peer_scores(xn, w["wq_t"], w["kbd"], w["heads"])
    a, eidx, gate = _peer_a_select(xn, w["down_t"], scores, w["heads"])
    m = _peer_b(a, eidx, gate)
    out = _peer_c(m, w["up"], x1, w["final_norm"])
    return out.reshape(bsz, seq_len, d)


def kernel(x_prompt, x_sample, norm1, w_in, a_ln_g, a_ln_b, a_ws, a_bs, b_conv_w, b_conv_b, hf_w1, hf_b1, hf_w2, hf_b2, hf_w3, hf_b3, hf_freq, hf_w4, b_bias, mix_norm, w_out, norm2, peer_wq, peer_k1, peer_k2, peer_down, peer_up, final_norm):
    assert norm1.shape[0] == 1, "single-layer trunk"
    w = _prep_weights(norm1[0], w_in[0], a_ln_g[0], a_ln_b[0], a_ws[0], a_bs[0], b_conv_w[0], b_conv_b[0],
                      b_bias[0], mix_norm[0], w_out[0], norm2[0], peer_wq[0], peer_k1[0], peer_k2[0],
                      peer_down[0], peer_up[0], final_norm)
    hf = (hf_w1[0], hf_b1[0].reshape(1, -1), hf_w2[0], hf_b2[0].reshape(1, -1), hf_w3[0], hf_b3[0].reshape(1, -1),
          hf_freq[0], hf_w4[0])
    return (_trunk(x_prompt, w, hf), _trunk(x_sample, w, hf))
```

```python
import functools
import math

import numpy as np
import jax
import jax.numpy as jnp
from jax import lax
from jax.experimental import pallas as pl
from jax.experimental.pallas import tpu as pltpu

F32 = jnp.float32
BF16 = jnp.bfloat16
I32 = jnp.int32
U32 = jnp.uint32

EPS = 1e-6
V7X_LANES = 128
V7X_SUBLANES = 8
V7X_MXU_WIDTH = 256
V7X_VMEM_BYTES = 64 * 1024 * 1024
VMEM_LIMIT = V7X_VMEM_BYTES - 8 * 1024 * 1024

CHUNK = 128
A_HEAD_DIM = 128
EMB_DIM = 33
BANDS = (EMB_DIM - 1) // 2
DECAY_TARGET = 1e-2
FAST_DECAY_PCT = 0.3
SLOW_DECAY_PCT = 1.5
FFT_N1 = 64
FFT_N2 = 128
FFT_N = FFT_N1 * FFT_N2
CONV_BLOCK = FFT_N // 2
N_KEYS = 128
KEY_SHIFT = 7
PEER_TOPK = 16
HALF_KEY = 128


def _cparams(sem):
    return pltpu.CompilerParams(dimension_semantics=sem, vmem_limit_bytes=VMEM_LIMIT)


def _rms(xf, g):
    return xf * lax.rsqrt(jnp.mean(xf * xf, axis=-1, keepdims=True) + EPS) * g


def _gelu(x):
    return 0.5 * x * (1.0 + lax.erf(x * np.float32(math.sqrt(0.5))))


def _inproj_kernel(xp_ref, x_ref, xn_ref, n1_ref, wu_ref, wv_ref, wb_ref, lng_ref, lnb_ref,
                   ws_ref, bsb_ref, cw_ref, cb_ref, mga_ref,
                   yan_ref, x0_ref, g_ref, ya_s, zs_s, *, tiles_per_seq, cblk):
    tm = x_ref.shape[0]
    d_a = wu_ref.shape[1]
    d_b = wb_ref.shape[2]
    i = pl.program_id(0)
    not_first = (i % tiles_per_seq != 0).astype(F32)
    not_last = (i % tiles_per_seq != tiles_per_seq - 1).astype(F32)

    xcat = jnp.concatenate([xp_ref[...], x_ref[...], xn_ref[...]], axis=0)
    hcat = _rms(xcat, n1_ref[...])
    h_all = hcat.astype(BF16)
    h = hcat[V7X_SUBLANES:V7X_SUBLANES + tm].astype(BF16)

    u = _gelu(jnp.dot(h, wu_ref[...], preferred_element_type=F32))
    v = _gelu(jnp.dot(h, wv_ref[...], preferred_element_type=F32))
    mu = jnp.mean(v, axis=-1, keepdims=True)
    vc = v - mu
    var = jnp.mean(vc * vc, axis=-1, keepdims=True)
    vb = (vc * lax.rsqrt(var + EPS) * lng_ref[...] + lnb_ref[...]).astype(BF16)
    for c in range(tm // CHUNK):
        rows = slice(c * CHUNK, (c + 1) * CHUNK)
        for hd in range(d_a // A_HEAD_DIM):
            cols = slice(hd * A_HEAD_DIM, (hd + 1) * A_HEAD_DIM)
            mixed = jnp.dot(ws_ref[hd], vb[rows, cols], preferred_element_type=F32) + bsb_ref[hd]
            ya_s[rows, cols] = u[rows, cols] * mixed
    yan_ref[...] = _rms(ya_s[...], mga_ref[...]).astype(yan_ref.dtype)

    halo = V7X_SUBLANES
    for cb in range(d_b // cblk):
        cols = slice(cb * cblk, (cb + 1) * cblk)
        parts = []
        for p in range(3):
            z = jnp.dot(h_all, wb_ref[p, :, cols], preferred_element_type=F32)
            zs_s[...] = z
            zs_s[0:halo, :] = z[0:halo] * not_first
            zs_s[tm + halo:tm + 2 * halo, :] = z[tm + halo:tm + 2 * halo] * not_last
            w = cw_ref[p]
            zc = (cb_ref[p:p + 1, cols]
                  + zs_s[halo - 1:halo - 1 + tm, :] * w[0:1, cols]
                  + zs_s[halo:halo + tm, :] * w[1:2, cols]
                  + zs_s[halo + 1:halo + 1 + tm, :] * w[2:3, cols])
            parts.append(zc)
        x0_ref[:, cols] = parts[0]
        g_ref[:, cols] = parts[1] * parts[2]


def _inproj(x, seq_len, norm1, w_u, w_v, w_b, ln_g, ln_b, ws, bsb, conv_w, conv_b, mix_g_a, tm=512, cblk=256):
    t, d = x.shape
    d_a = w_u.shape[1]
    d_b = w_b.shape[2]
    assert seq_len % tm == 0 and t % seq_len == 0 and tm % CHUNK == 0 and d_b % cblk == 0, (t, seq_len)
    nblk = tm // V7X_SUBLANES
    last8 = t // V7X_SUBLANES - 1
    const = lambda *shape: pl.BlockSpec(shape, lambda i: (0,) * len(shape), pipeline_mode=pl.Buffered(1))
    kern = functools.partial(_inproj_kernel, tiles_per_seq=seq_len // tm, cblk=cblk)
    return pl.pallas_call(
        kern,
        grid=(t // tm,),
        in_specs=[
            pl.BlockSpec((V7X_SUBLANES, d), lambda i: (jnp.maximum(i * nblk - 1, 0), 0)),
            pl.BlockSpec((tm, d), lambda i: (i, 0)),
            pl.BlockSpec((V7X_SUBLANES, d), lambda i: (jnp.minimum((i + 1) * nblk, last8), 0)),
            const(1, d), const(d, d_a), const(d, d_a), const(3, d, d_b), const(1, d_a), const(1, d_a),
            const(*ws.shape), const(*bsb.shape), const(3, 3, d_b), const(3, d_b), const(1, d_a),
        ],
        out_specs=[
            pl.BlockSpec((tm, d_a), lambda i: (i, 0)),
            pl.BlockSpec((tm, d_b), lambda i: (i, 0)),
            pl.BlockSpec((tm, d_b), lambda i: (i, 0)),
        ],
        out_shape=[
            jax.ShapeDtypeStruct((t, d_a), BF16),
            jax.ShapeDtypeStruct((t, d_b), F32),
            jax.ShapeDtypeStruct((t, d_b), F32),
        ],
        scratch_shapes=[pltpu.VMEM((tm, d_a), F32), pltpu.VMEM((tm + 2 * V7X_SUBLANES, cblk), F32)],
        compiler_params=_cparams(("parallel",)),
        name="inproj",
    )(x, x, x, norm1, w_u, w_v, w_b, ln_g, ln_b, ws, bsb, conv_w, conv_b, mix_g_a)


def _filt_kernel(fr_ref, dl_ref, w1_ref, b1_ref, w2_ref, b2_ref, w3_ref, b3_ref, fq_ref, w4_ref, o_ref,
                 *, seq_len, nblocks, n2_per_step):
    half = FFT_N1 // 2
    tr = half * n2_per_step
    e = pl.program_id(0) - nblocks
    r = lax.broadcasted_iota(I32, (tr, 1), 0)
    n2 = pl.program_id(1) * n2_per_step + r // half
    trow = (r % half) * FFT_N2 + n2
    lag = e * CONV_BLOCK + trow
    pos = jnp.abs(lag).astype(F32)
    valid = jnp.abs(lag) <= seq_len - 1
    t = pos / np.float32(max(seq_len - 1, 1))
    fw = np.float32(2.0 * math.pi / seq_len) * pos * fr_ref[...]
    hi = lax.Precision.HIGHEST
    w1 = w1_ref[...]
    z1 = (t * w1[0:1, :]
          + jnp.dot(jnp.cos(fw), w1[1:1 + BANDS, :], precision=hi, preferred_element_type=F32)
          + jnp.dot(-jnp.sin(fw), w1[1 + BANDS:, :], precision=hi, preferred_element_type=F32))
    fq = fq_ref[...]
    h = jnp.sin(fq[0:1, :] * (z1 + b1_ref[...]))
    h = jnp.sin(fq[1:2, :] * (jnp.dot(h, w2_ref[...], precision=hi, preferred_element_type=F32) + b2_ref[...]))
    h = jnp.sin(fq[2:3, :] * (jnp.dot(h, w3_ref[...], precision=hi, preferred_element_type=F32) + b3_ref[...]))
    h4 = jnp.dot(h, w4_ref[...], precision=hi, preferred_element_type=F32)
    window = jnp.exp(-t * dl_ref[...])
    res = jnp.where(valid, h4 * window, 0.0)
    d_b = res.shape[1]
    for i in range(n2_per_step):
        o_ref[:, i * d_b:(i + 1) * d_b] = res[i * half:(i + 1) * half]


def _hyena_filter_blocks(seq_len, w1, b1, w2, b2, w3, b3, freq, w4, n2_per_step=16):
    nblocks = seq_len // CONV_BLOCK
    width = w1.shape[1]
    d_b = w4.shape[1] // 2
    fr = jnp.asarray(np.linspace(1e-4, BANDS - 1, BANDS, dtype=np.float32)[None, :])
    min_decay = math.log(DECAY_TARGET) / SLOW_DECAY_PCT
    max_decay = math.log(DECAY_TARGET) / FAST_DECAY_PCT
    deltas = jnp.asarray(np.abs(np.linspace(min_decay, max_decay, d_b, dtype=np.float32))[None, :])
    w4h = w4.reshape(width, 2, d_b).transpose(1, 0, 2)
    const = lambda *shape: pl.BlockSpec(shape, lambda a, b: (0,) * len(shape))
    half = FFT_N1 // 2
    kern = functools.partial(_filt_kernel, seq_len=seq_len, nblocks=nblocks, n2_per_step=n2_per_step)
    return pl.pallas_call(
        kern,
        grid=(2 * nblocks, FFT_N2 // n2_per_step),
        in_specs=[
            const(1, BANDS), const(1, d_b), const(EMB_DIM, width), const(1, width), const(width, width),
            const(1, width), const(width, width), const(1, width), const(3, width),
            pl.BlockSpec((None, width, d_b), lambda ei, r: (jnp.where(ei >= nblocks, 0, 1), 0, 0)),
        ],
        out_specs=pl.BlockSpec((None, half, n2_per_step * d_b), lambda ei, r: (ei, 0, r)),
        out_shape=jax.ShapeDtypeStruct((2 * nblocks, half, FFT_N2 * d_b), F32),
        compiler_params=_cparams(("parallel", "parallel")),
        name="hyena_filter",
    )(fr, deltas, w1, b1, w2, b2, w3, b3, freq, w4h)


def _dft_tables():
    n1 = np.arange(FFT_N1)
    f64 = np.exp(-2j * np.pi * np.outer(n1, n1) / FFT_N1)
    half = FFT_N1 // 2
    fr, fi = f64.real, f64.imag
    lhs_data = np.block([[fr[:, :half], -fi[:, :half]], [fi[:, :half], fr[:, :half]]])
    lhs_real = np.concatenate([fr, fi], axis=0)
    gr, gi = fr[:half, :] / FFT_N, -fi[:half, :] / FFT_N
    lhs_inv = np.block([[gr, -gi], [gi, gr]])
    k1 = np.arange(FFT_N1)[:, None, None]
    k2 = np.arange(FFT_N2)[None, :, None]
    n2 = np.arange(FFT_N2)[None, None, :]
    g = np.exp(-2j * np.pi * (n2 * (k1 + FFT_N1 * k2) % FFT_N) / FFT_N)
    gfwd = np.concatenate([np.concatenate([g.real, -g.imag], axis=2),
                           np.concatenate([g.imag, g.real], axis=2)], axis=1)
    ht = np.conj(np.transpose(g, (0, 2, 1)))
    ginv = np.concatenate([np.concatenate([ht.real, -ht.imag], axis=2),
                           np.concatenate([ht.imag, ht.real], axis=2)], axis=1)
    as32 = lambda a: np.asarray(a, dtype=np.float32)
    return as32(lhs_data), as32(lhs_real), as32(lhs_inv), as32(gfwd), as32(ginv)


def _split(x):
    hi = x.astype(BF16)
    lo = (x - hi.astype(F32)).astype(BF16)
    return hi, lo


def _dot3(a_hi, a_lo, b):
    if b.dtype == BF16:
        return jnp.dot(a_hi, b, preferred_element_type=F32) + jnp.dot(a_lo, b, preferred_element_type=F32)
    b_hi, b_lo = _split(b)
    acc = jnp.dot(a_hi, b_hi, preferred_element_type=F32)
    acc = acc + jnp.dot(a_hi, b_lo, preferred_element_type=F32)
    acc = acc + jnp.dot(a_lo, b_hi, preferred_element_type=F32)
    return acc


SPECTRA_DTYPE = BF16
FFT_TN2 = 16


def _rows_to_lanes(x):
    xt = jnp.swapaxes(x, 0, 1)
    return jnp.concatenate([xt[i] for i in range(xt.shape[0])], axis=1)


def _lanes_to_rows(x, s):
    c = x.shape[1] // s
    return jnp.swapaxes(jnp.stack([x[:, i * c:(i + 1) * c] for i in range(s)], axis=0), 0, 1)


def _fft1_kernel(lh_ref, ll_ref, a_ref, b_ref, o_ref, *, zero_lanes, natural):
    if natural:
        a = _rows_to_lanes(a_ref[...])
        b = _rows_to_lanes(b_ref[...])
    else:
        a = a_ref[...]
        b = b_ref[...]
    if zero_lanes:
        row = lax.broadcasted_iota(I32, b.shape, 0)
        lane = lax.broadcasted_iota(I32, b.shape, 1) + pl.program_id(2) * b.shape[1]
        b = jnp.where(jnp.logical_and(row == 0, lane < zero_lanes), 0.0, b)
    res = _dot3(lh_ref[...], ll_ref[...], jnp.concatenate([a, b], axis=0))
    o_ref[...] = _lanes_to_rows(res, FFT_TN2).reshape(o_ref.shape).astype(o_ref.dtype)


def _fft1(lhs, x, a_of, b_of, g, p, d_b, natural, zero_lanes=0):
    half = FFT_N1 // 2
    lh, ll = _split(jnp.asarray(lhs))
    kern = functools.partial(_fft1_kernel, zero_lanes=zero_lanes, natural=natural)
    if natural:
        blk = lambda of: pl.BlockSpec((None, None, half, FFT_TN2, d_b), lambda q, j, t: of(q, j) + (0, t, 0))
    else:
        blk = lambda of: pl.BlockSpec((None, None, half, FFT_TN2 * d_b), lambda q, j, t: of(q, j) + (0, t))
    return pl.pallas_call(
        kern,
        grid=(g, p, FFT_N2 // FFT_TN2),
        in_specs=[
            pl.BlockSpec(lhs.shape, lambda q, j, t: (0, 0)),
            pl.BlockSpec(lhs.shape, lambda q, j, t: (0, 0)),
            blk(a_of), blk(b_of),
        ],
        out_specs=pl.BlockSpec((None, None, 2, FFT_N1, FFT_TN2, d_b), lambda q, j, t: (q, j, 0, 0, t, 0)),
        out_shape=jax.ShapeDtypeStruct((g, p, 2, FFT_N1, FFT_N2, d_b), SPECTRA_DTYPE),
        compiler_params=_cparams(("parallel", "parallel", "parallel")),
        name="fft1",
    )(lh, ll, x, x)


def _fft3_kernel(lh_ref, ll_ref, w_ref, o_ref):
    half = FFT_N1 // 2
    c = w_ref.shape[-1]
    w = _rows_to_lanes(w_ref[...].astype(F32).reshape(2 * FFT_N1, FFT_TN2, c)).astype(w_ref.dtype)
    res = _dot3(lh_ref[...], ll_ref[...], w)
    o_ref[0] = _lanes_to_rows(res[:half], FFT_TN2)
    o_ref[1] = _lanes_to_rows(res[half:], FFT_TN2)


def _fft3(lhs, w6):
    g, p, _, _, _, d_b = w6.shape
    half = FFT_N1 // 2
    lh, ll = _split(jnp.asarray(lhs))
    return pl.pallas_call(
        _fft3_kernel,
        grid=(g, p, FFT_N2 // FFT_TN2),
        in_specs=[
            pl.BlockSpec(lhs.shape, lambda q, j, t: (0, 0)),
            pl.BlockSpec(lhs.shape, lambda q, j, t: (0, 0)),
            pl.BlockSpec((None, None, 2, FFT_N1, FFT_TN2, d_b), lambda q, j, t: (q, j, 0, 0, t, 0)),
        ],
        out_specs=pl.BlockSpec((None, 2, None, half, FFT_TN2, d_b), lambda q, j, t: (q, 0, j, 0, t, 0)),
        out_shape=jax.ShapeDtypeStruct((g, 2, p, half, FFT_N2, d_b), F32),
        compiler_params=_cparams(("parallel", "parallel", "parallel")),
        name="fft3",
    )(lh, ll, w6)


def _fft2_filter_kernel(gh_ref, gl_ref, x_ref, o_ref):
    ct = x_ref.shape[-1]
    for k in range(x_ref.shape[1]):
        z = _dot3(gh_ref[k], gl_ref[k], x_ref[:, k].reshape(2 * FFT_N2, ct))
        o_ref[:, k] = z.reshape(2, FFT_N2, ct).astype(o_ref.dtype)


def _fft2_filter(gfwd_hl, x1f, ct=256, kb=8):
    nd, _, _, _, d_b = x1f.shape
    gh, gl = gfwd_hl
    gspec = pl.BlockSpec((kb, 2 * FFT_N2, 2 * FFT_N2), lambda k, d, c: (k, 0, 0))
    xspec = pl.BlockSpec((None, 2, kb, FFT_N2, ct), lambda k, d, c: (d, 0, k, 0, c))
    return pl.pallas_call(
        _fft2_filter_kernel,
        grid=(FFT_N1 // kb, nd, d_b // ct),
        in_specs=[gspec, gspec, xspec],
        out_specs=xspec,
        out_shape=jax.ShapeDtypeStruct(x1f.shape, SPECTRA_DTYPE),
        compiler_params=_cparams(("parallel", "parallel", "parallel")),
        name="fft2_filter",
    )(gh, gl, x1f)


def _fft2_mix_kernel(gh_ref, gl_ref, ih_ref, il_ref, x_ref, k_ref, o_ref, *, nblocks):
    ct = x_ref.shape[-1]
    for k in range(x_ref.shape[2]):
        zs = []
        for j in range(nblocks):
            z = _dot3(gh_ref[k], gl_ref[k], x_ref[j, :, k].reshape(2 * FFT_N2, ct))
            zs.append((z[:FFT_N2], z[FFT_N2:]))
        for i in range(nblocks):
            yr = jnp.zeros((FFT_N2, ct), F32)
            yi = jnp.zeros((FFT_N2, ct), F32)
            for j in range(nblocks):
                d = i - j + nblocks - 1
                kr, ki = k_ref[d, 0, k].astype(F32), k_ref[d, 1, k].astype(F32)
                zr, zi = zs[j]
                yr = yr + kr * zr - ki * zi
                yi = yi + kr * zi + ki * zr
            w = _dot3(ih_ref[k], il_ref[k], jnp.concatenate([yr, yi], axis=0))
            o_ref[i, :, k] = w.reshape(2, FFT_N2, ct).astype(o_ref.dtype)


def _fft2_mix(gfwd_hl, ginv_hl, x1, kspec, ct=256):
    g, p, _, _, _, d_b = x1.shape
    nd = kspec.shape[0]
    kb = max(1, 8 // p)
    gspec = pl.BlockSpec((kb, 2 * FFT_N2, 2 * FFT_N2), lambda k, q, c: (k, 0, 0))
    xspec = pl.BlockSpec((None, p, 2, kb, FFT_N2, ct), lambda k, q, c: (q, 0, 0, k, 0, c))
    kern = functools.partial(_fft2_mix_kernel, nblocks=p)
    return pl.pallas_call(
        kern,
        grid=(FFT_N1 // kb, g, d_b // ct),
        in_specs=[gspec, gspec, gspec, gspec, xspec,
                  pl.BlockSpec((nd, 2, kb, FFT_N2, ct), lambda k, q, c: (0, 0, k, 0, c))],
        out_specs=xspec,
        out_shape=jax.ShapeDtypeStruct(x1.shape, SPECTRA_DTYPE),
        compiler_params=_cparams(("parallel", "parallel", "parallel")),
        name="fft2_mix",
    )(gfwd_hl[0], gfwd_hl[1], ginv_hl[0], ginv_hl[1], x1, kspec)


def _hyena_long_conv(g3, w1, b1, w2, b2, w3, b3, freq, w4):
    bsz, seq_len, d_b = g3.shape
    nblocks = seq_len // CONV_BLOCK
    half = FFT_N1 // 2
    lhs_data, lhs_real, lhs_inv, gfwd, ginv = _dft_tables()
    gfwd_hl = _split(jnp.asarray(gfwd))
    ginv_hl = _split(jnp.asarray(ginv))
    kext = _hyena_filter_blocks(seq_len, w1, b1, w2, b2, w3, b3, freq, w4)
    nd = 2 * nblocks - 1
    k1f = _fft1(lhs_real, kext[None], lambda q, j: (0, q + 1), lambda q, j: (0, q), nd, 1, d_b,
                natural=False, zero_lanes=d_b)
    kspec = _fft2_filter(gfwd_hl, k1f.reshape(nd, 2, FFT_N1, FFT_N2, d_b))
    g5 = g3.reshape(bsz, nblocks, half, FFT_N2, d_b)
    x1 = _fft1(lhs_data, g5, lambda q, j: (2 * q, j), lambda q, j: (2 * q + 1, j), bsz // 2, nblocks, d_b,
               natural=True)
    wmix = _fft2_mix(gfwd_hl, ginv_hl, x1, kspec)
    y6 = _fft3(lhs_inv, wmix)
    return y6.reshape(bsz, seq_len, d_b)


def _postmix_kernel(x_ref, yan_ref, x0_ref, g_ref, yc_ref, bias_ref, mgb_ref, wo_ref, n2_ref, x1_ref, xn_ref):
    d_a = yan_ref.shape[1]
    gg = g_ref[...]
    yb = x0_ref[...] * (yc_ref[...] + gg * bias_ref[...])
    ybn = _rms(yb, mgb_ref[...]).astype(BF16)
    y = jnp.dot(yan_ref[...], wo_ref[0:d_a, :], preferred_element_type=F32)
    y = y + jnp.dot(ybn, wo_ref[d_a:, :], preferred_element_type=F32)
    x1 = x_ref[...] + y
    x1_ref[...] = x1
    xn_ref[...] = _rms(x1, n2_ref[...]).astype(xn_ref.dtype)


def _postmix(x, yan, x0, g, yconv, bias, mix_g_b, w_out, norm2, tm=512):
    t, d = x.shape
    d_a = yan.shape[1]
    d_b = x0.shape[1]
    assert t % tm == 0, t
    row = lambda w: pl.BlockSpec((tm, w), lambda i: (i, 0))
    const = lambda *shape: pl.BlockSpec(shape, lambda i: (0,) * len(shape), pipeline_mode=pl.Buffered(1))
    return pl.pallas_call(
        _postmix_kernel,
        grid=(t // tm,),
        in_specs=[row(d), row(d_a), row(d_b), row(d_b), row(d_b), const(1, d_b), const(1, d_b),
                  const(d_a + d_b, d), const(1, d)],
        out_specs=[row(d), row(d)],
        out_shape=[jax.ShapeDtypeStruct((t, d), F32), jax.ShapeDtypeStruct((t, d), BF16)],
        compiler_params=_cparams(("parallel",)),
        name="postmix",
    )(x, yan, x0, g, yconv, bias, mix_g_b, w_out, norm2)


def _staircase():
    return [(i, j) for i in range(PEER_TOPK) for j in range(PEER_TOPK) if (i + 1) * (j + 1) <= PEER_TOPK]


ARGMAX_LANES = 2


def _stream_argmax(val_ref, tag_ref, nrows, prev, slab):
    neg = np.float32(-np.inf)
    best_v = [jnp.full(slab, neg, F32) for _ in range(ARGMAX_LANES)]
    best_i = [jnp.zeros(slab, I32) for _ in range(ARGMAX_LANES)]
    best_t = [jnp.zeros(slab, I32) for _ in range(ARGMAX_LANES)]
    for n in range(nrows):
        v = jnp.where(prev == n, neg, val_ref[n])
        val_ref[n] = v
        k = n % ARGMAX_LANES
        better = v > best_v[k]
        best_v[k] = jnp.maximum(v, best_v[k])
        best_i[k] = jnp.where(better, n, best_i[k])
        if tag_ref is not None:
            best_t[k] = jnp.where(better, tag_ref[n], best_t[k])
    width = ARGMAX_LANES
    while width > 1:
        width //= 2
        for k in range(width):
            va, vb = best_v[k], best_v[k + width]
            ia, ib = best_i[k], best_i[k + width]
            take_b = jnp.logical_or(vb > va, jnp.logical_and(vb == va, ib < ia))
            best_v[k] = jnp.where(take_b, vb, va)
            best_i[k] = jnp.where(take_b, ib, ia)
            best_t[k] = jnp.where(take_b, best_t[k + width], best_t[k])
    return best_v[0], best_i[0], best_t[0]


def _scores_kernel(xn_ref, wqt_ref, kbd_ref, s_ref, *, heads):
    tm = xn_ref.shape[0]
    hk = heads * HALF_KEY
    q_t = lax.dot_general(wqt_ref[...], xn_ref[...], (((1,), (1,)), ((), ())),
                          preferred_element_type=F32).astype(BF16)
    for half in range(2):
        s_t = jnp.dot(kbd_ref[half], q_t[half * hk:(half + 1) * hk], preferred_element_type=F32)
        s_ref[half] = s_t.reshape(N_KEYS, heads, tm)


def _peer_scores(xn, wqt, kbd, heads, tm=512):
    t, d = xn.shape
    assert t % tm == 0 and heads == V7X_SUBLANES, (t, heads)
    const = lambda *shape: pl.BlockSpec(shape, lambda i: (0,) * len(shape), pipeline_mode=pl.Buffered(1))
    kern = functools.partial(_scores_kernel, heads=heads)
    return pl.pallas_call(
        kern,
        grid=(t // tm,),
        in_specs=[pl.BlockSpec((tm, d), lambda i: (i, 0)), const(*wqt.shape), const(*kbd.shape)],
        out_specs=pl.BlockSpec((2, N_KEYS, heads, tm), lambda i: (0, 0, 0, i)),
        out_shape=jax.ShapeDtypeStruct((2, N_KEYS, heads, t), F32),
        compiler_params=_cparams(("parallel",)),
        name="peer_scores",
    )(xn, wqt, kbd)


SELECT_BLOCK = 128


def _peer_a_select_kernel(xn_ref, dlo_ref, dhi_ref, s_ref, a_ref, e_ref, gt_ref, s0_s, s1_s, c_s, ce_s, b_s, es_s,
                          *, heads):
    xn = xn_ref[...]
    a_lo = jnp.dot(xn, dlo_ref[...], preferred_element_type=F32)
    a_hi = jnp.dot(xn, dhi_ref[...], preferred_element_type=F32)
    a_ref[...] = pltpu.pack_elementwise([a_lo, a_hi], packed_dtype=BF16)

    slab = (heads, s_ref.shape[-1])
    no_pick = jnp.full(slab, -1, I32)
    half_s = (s0_s, s1_s)
    vals, idxs, prevs = ([], []), ([], []), [no_pick, no_pick]
    for half in range(2):
        half_s[half][...] = s_ref[half]
    for r in range(PEER_TOPK):
        for half in range(2):
            m, prevs[half], _ = _stream_argmax(half_s[half], None, N_KEYS, prevs[half], slab)
            vals[half].append(m)
            idxs[half].append(prevs[half])
    cands = _staircase()
    for p, (i, j) in enumerate(cands):
        c_s[p] = vals[0][i] + vals[1][j]
        ce_s[p] = idxs[0][i] * N_KEYS + idxs[1][j]
    prev = no_pick
    for r in range(PEER_TOPK):
        m, prev, e = _stream_argmax(c_s, ce_s, len(cands), prev, slab)
        es_s[r] = e
        b_s[r] = m
    best = b_s[...]
    ex = jnp.exp(best - jnp.max(best, axis=0, keepdims=True))
    gate = ex / jnp.sum(ex, axis=0, keepdims=True)
    gt_ref[...] = gate.reshape(PEER_TOPK * heads, slab[1]).T
    e_ref[...] = es_s[...].reshape(PEER_TOPK * heads, slab[1]).T


def _peer_a_select(xn, down_t, scores, heads, tm=1024, tn=1024):
    t, d = xn.shape
    n = down_t.shape[1]
    tb = SELECT_BLOCK
    nsel = PEER_TOPK * heads
    ncand = len(_staircase())
    ni = t // tm
    nj = n // (2 * tn)
    assert t % tm == 0 and n % (2 * tn) == 0 and heads == V7X_SUBLANES, (t, n, heads)
    assert nj * ni * tb == t, "one selection block per grid step must cover all tokens"
    blk = lambda j, i: j * ni + i
    kern = functools.partial(_peer_a_select_kernel, heads=heads)
    return pl.pallas_call(
        kern,
        grid=(nj, ni),
        in_specs=[pl.BlockSpec((tm, d), lambda j, i: (i, 0)),
                  pl.BlockSpec((d, tn), lambda j, i: (0, j)), pl.BlockSpec((d, tn), lambda j, i: (0, j + nj)),
                  pl.BlockSpec((2, N_KEYS, heads, tb), lambda j, i: (0, 0, 0, blk(j, i)))],
        out_specs=[pl.BlockSpec((tm, tn), lambda j, i: (i, j)),
                   pl.BlockSpec((tb, nsel), lambda j, i: (blk(j, i), 0)),
                   pl.BlockSpec((tb, nsel), lambda j, i: (blk(j, i), 0))],
        out_shape=[jax.ShapeDtypeStruct((t, n // 2), U32), jax.ShapeDtypeStruct((t, nsel), I32),
                   jax.ShapeDtypeStruct((t, nsel), F32)],
        scratch_shapes=[
            pltpu.VMEM((N_KEYS, heads, tb), F32),
            pltpu.VMEM((N_KEYS, heads, tb), F32),
            pltpu.VMEM((ncand, heads, tb), F32),
            pltpu.VMEM((ncand, heads, tb), I32),
            pltpu.VMEM((PEER_TOPK, heads, tb), F32),
            pltpu.VMEM((PEER_TOPK, heads, tb), I32),
        ],
        compiler_params=_cparams(("parallel", "parallel")),
        name="peer_a_select",
    )(xn, down_t, down_t, scores)


PEER_B_GROUP = 16


def _peer_b_kernel(a_ref, e_ref, gt_ref, m_ref):
    c_tok = a_ref.shape[0]
    nsel = e_ref.shape[1]
    grp = PEER_B_GROUP
    ngrp = c_tok // grp
    sub = lax.broadcasted_iota(I32, (grp, N_KEYS, nsel), 1)

    def rows_of(gi):
        return pl.ds(pl.multiple_of(gi * grp, grp), grp)

    def pick(gi):
        rows = rows_of(gi)
        e = e_ref[rows, :]
        ai = lax.shift_right_logical(e, KEY_SHIFT)
        bi = e & (N_KEYS - 1)
        a_word = ai & (N_KEYS // 2 - 1)
        accs = [jnp.zeros((grp, nsel), U32) for _ in range(4)]
        for a in range(N_KEYS // 2):
            blk = a_ref[rows, a * N_KEYS:(a + 1) * N_KEYS]
            accs[a % 4] = jnp.where(a_word == a, jnp.take_along_axis(blk, bi, axis=1), accs[a % 4])
        word = (accs[0] | accs[1]) | (accs[2] | accs[3])
        lo = pltpu.unpack_elementwise(word, index=0, packed_dtype=BF16, unpacked_dtype=F32)
        hi = pltpu.unpack_elementwise(word, index=1, packed_dtype=BF16, unpacked_dtype=F32)
        picked = jnp.where(ai >= N_KEYS // 2, hi, lo)
        return gt_ref[rows, :] * _gelu(picked), e

    def scatter(gi, w, e):
        ai = lax.shift_right_logical(e, KEY_SHIFT)
        bi = e & (N_KEYS - 1)
        w1t = jnp.where(sub == ai[:, None, :], w[:, None, :], 0.0).astype(BF16)
        e2t = jnp.where(sub == bi[:, None, :], 1.0, 0.0).astype(BF16)
        m3 = jnp.einsum("cas,cbs->cab", w1t, e2t, preferred_element_type=F32)
        mt = jnp.swapaxes(m3, 0, 1).astype(m_ref.dtype)
        rows = rows_of(gi)
        for a in range(N_KEYS):
            m_ref[rows, a * N_KEYS:(a + 1) * N_KEYS] = mt[a]

    def step(gi, carry):
        nxt = pick(gi)
        scatter(gi - 1, *carry)
        return nxt

    last = lax.fori_loop(1, ngrp, step, pick(0))
    scatter(ngrp - 1, *last)


def _peer_b(a2, eidx, gate, tc=256):
    t = a2.shape[0]
    n = 2 * a2.shape[1]
    nsel = eidx.shape[1]
    assert t % tc == 0 and tc % PEER_B_GROUP == 0 and n == N_KEYS * N_KEYS and nsel == N_KEYS, (t, n, nsel)
    return pl.pallas_call(
        _peer_b_kernel,
        grid=(t // tc,),
        in_specs=[pl.BlockSpec((tc, n // 2), lambda i: (i, 0)), pl.BlockSpec((tc, nsel), lambda i: (i, 0)),
                  pl.BlockSpec((tc, nsel), lambda i: (i, 0))],
        out_specs=pl.BlockSpec((tc, n), lambda i: (i, 0)),
        out_shape=jax.ShapeDtypeStruct((t, n), BF16),
        compiler_params=_cparams(("parallel",)),
        name="peer_b",
    )(a2, eidx, gate)


def _peer_c_kernel(m_ref, up_ref, x1_ref, fn_ref, o_ref, acc_s):
    k = pl.program_id(1)

    @pl.when(k == 0)
    def _():
        acc_s[...] = jnp.zeros_like(acc_s)

    acc_s[...] += jnp.dot(m_ref[...], up_ref[...], preferred_element_type=F32)

    @pl.when(k == pl.num_programs(1) - 1)
    def _():
        o_ref[...] = _rms(x1_ref[...] + acc_s[...], fn_ref[...])


def _peer_c(m, up, x1, final_norm, tm=1024, tk=1024):
    t, n = m.shape
    d = up.shape[1]
    assert t % tm == 0 and n % tk == 0, (t, n)
    return pl.pallas_call(
        _peer_c_kernel,
        grid=(t // tm, n // tk),
        in_specs=[pl.BlockSpec((tm, tk), lambda i, k: (i, k)), pl.BlockSpec((tk, d), lambda i, k: (k, 0)),
                  pl.BlockSpec((tm, d), lambda i, k: (i, 0), pipeline_mode=pl.Buffered(1)),
                  pl.BlockSpec((1, d), lambda i, k: (0, 0))],
        out_specs=pl.BlockSpec((tm, d), lambda i, k: (i, 0)),
        out_shape=jax.ShapeDtypeStruct((t, d), F32),
        scratch_shapes=[pltpu.VMEM((tm, d), F32)],
        compiler_params=_cparams(("parallel", "arbitrary")),
        name="peer_c",
    )(m, up, x1, final_norm)


def _prep_weights(norm1, w_in, a_ln_g, a_ln_b, a_ws, a_bs, b_conv_w, b_conv_b, b_bias, mix_norm, w_out, norm2,
                  peer_wq, peer_k1, peer_k2, peer_down, peer_up, final_norm):
    d = w_in.shape[0]
    d_a = a_ln_g.shape[0]
    d_b = b_bias.shape[0]
    heads, n_keys, half_key = peer_k1.shape
    row = lambda v: v.reshape(1, -1).astype(F32)
    w = dict(
        norm1=row(norm1), norm2=row(norm2), final_norm=row(final_norm),
        w_u=w_in[:, :d_a].astype(BF16), w_v=w_in[:, d_a:2 * d_a].astype(BF16),
        w_b=w_in[:, 2 * d_a:].reshape(d, 3, d_b).transpose(1, 0, 2).astype(BF16),
        ln_g=row(a_ln_g), ln_b=row(a_ln_b),
        ws=a_ws.astype(BF16),
        bsb=jnp.broadcast_to(a_bs[:, :, None], a_bs.shape + (A_HEAD_DIM,)).astype(F32),
        conv_w=b_conv_w.reshape(3, 3, d_b).transpose(1, 0, 2).astype(F32),
        conv_b=b_conv_b.reshape(3, d_b).astype(F32),
        bias=row(b_bias), mix_g_a=row(mix_norm[:d_a]), mix_g_b=row(mix_norm[d_a:]),
        w_out=w_out.astype(BF16),
        down_t=peer_down.astype(BF16).T, up=peer_up.astype(BF16),
    )
    wq_t = peer_wq.T.reshape(heads, 2, half_key, d).transpose(1, 0, 2, 3).reshape(2 * heads * half_key, d)
    w["wq_t"] = wq_t.astype(BF16)
    eye = jnp.eye(heads, dtype=F32)
    kbd = [jnp.einsum("hnd,hg->nhgd", k, eye).reshape(n_keys * heads, heads * half_key) for k in (peer_k1, peer_k2)]
    w["kbd"] = jnp.stack(kbd).astype(BF16)
    w["heads"] = heads
    return w


def _trunk(x3, w, hf):
    bsz, seq_len, d = x3.shape
    t = bsz * seq_len
    x = x3.reshape(t, d)
    yan, x0, g = _inproj(x, seq_len, w["norm1"], w["w_u"], w["w_v"], w["w_b"], w["ln_g"], w["ln_b"],
                         w["ws"], w["bsb"], w["conv_w"], w["conv_b"], w["mix_g_a"])
    d_b = x0.shape[1]
    yconv = _hyena_long_conv(g.reshape(bsz, seq_len, d_b), *hf).reshape(t, d_b)
    x1, xn = _postmix(x, yan, x0, g, yconv, w["bias"], w["mix_g_b"], w["w_out"], w["norm2"])
    scores = _peer_scores(xn, w["wq_t"], w["kbd"], w["heads"])
    a, eidx, gate = _peer_a_select(xn, w["down_t"], scores, w["heads"])
    m = _peer_b(a, eidx, gate)
    out = _peer_c(m, w["up"], x1, w["final_norm"])
    return out.reshape(bsz, seq_len, d)


def kernel(x_prompt, x_sample, norm1, w_in, a_ln_g, a_ln_b, a_ws, a_bs, b_conv_w, b_conv_b, hf_w1, hf_b1, hf_w2, hf_b2, hf_w3, hf_b3, hf_freq, hf_w4, b_bias, mix_norm, w_out, norm2, peer_wq, peer_k1, peer_k2, peer_down, peer_up, final_norm):
    assert norm1.shape[0] == 1, "single-layer trunk"
    w = _prep_weights(norm1[0], w_in[0], a_ln_g[0], a_ln_b[0], a_ws[0], a_bs[0], b_conv_w[0], b_conv_b[0],
                      b_bias[0], mix_norm[0], w_out[0], norm2[0], peer_wq[0], peer_k1[0], peer_k2[0],
                      peer_down[0], peer_up[0], final_norm)
    hf = (hf_w1[0], hf_b1[0].reshape(1, -1), hf_w2[0], hf_b2[0].reshape(1, -1), hf_w3[0], hf_b3[0].reshape(1, -1),
          hf_freq[0], hf_w4[0])
    return (_trunk(x_prompt, w, hf), _trunk(x_sample, w, hf))
```

```python
import functools
import math

import numpy as np
import jax
import jax.numpy as jnp
from jax import lax
from jax.experimental import pallas as pl
from jax.experimental.pallas import tpu as pltpu

F32 = jnp.float32
BF16 = jnp.bfloat16
I32 = jnp.int32
U32 = jnp.uint32

EPS = 1e-6
V7X_LANES = 128
V7X_SUBLANES = 8
V7X_MXU_WIDTH = 256
V7X_VMEM_BYTES = 64 * 1024 * 1024
VMEM_LIMIT = V7X_VMEM_BYTES - 8 * 1024 * 1024

CHUNK = 128
A_HEAD_DIM = 128
EMB_DIM = 33
BANDS = (EMB_DIM - 1) // 2
DECAY_TARGET = 1e-2
FAST_DECAY_PCT = 0.3
SLOW_DECAY_PCT = 1.5
FFT_N1 = 64
FFT_N2 = 128
FFT_N = FFT_N1 * FFT_N2
CONV_BLOCK = FFT_N // 2
N_KEYS = 128
KEY_SHIFT = 7
PEER_TOPK = 16
HALF_KEY = 128


def _cparams(sem):
    return pltpu.CompilerParams(dimension_semantics=sem, vmem_limit_bytes=VMEM_LIMIT)


def _rms(xf, g):
    return xf * lax.rsqrt(jnp.mean(xf * xf, axis=-1, keepdims=True) + EPS) * g


def _gelu(x):
    return 0.5 * x * (1.0 + lax.erf(x * np.float32(math.sqrt(0.5))))


def _inproj_kernel(xp_ref, x_ref, xn_ref, n1_ref, wu_ref, wv_ref, wb_ref, lng_ref, lnb_ref,
                   ws_ref, bsb_ref, cw_ref, cb_ref, mga_ref,
                   yan_ref, x0_ref, g_ref, ya_s, zs_s, *, tiles_per_seq, cblk):
    tm = x_ref.shape[0]
    d_a = wu_ref.shape[1]
    d_b = wb_ref.shape[2]
    i = pl.program_id(0)
    not_first = (i % tiles_per_seq != 0).astype(F32)
    not_last = (i % tiles_per_seq != tiles_per_seq - 1).astype(F32)

    xcat = jnp.concatenate([xp_ref[...], x_ref[...], xn_ref[...]], axis=0)
    hcat = _rms(xcat, n1_ref[...])
    h_all = hcat.astype(BF16)
    h = hcat[V7X_SUBLANES:V7X_SUBLANES + tm].astype(BF16)

    u = _gelu(jnp.dot(h, wu_ref[...], preferred_element_type=F32))
    v = _gelu(jnp.dot(h, wv_ref[...], preferred_element_type=F32))
    mu = jnp.mean(v, axis=-1, keepdims=True)
    vc = v - mu
    var = jnp.mean(vc * vc, axis=-1, keepdims=True)
    vb = (vc * lax.rsqrt(var + EPS) * lng_ref[...] + lnb_ref[...]).astype(BF16)
    for c in range(tm // CHUNK):
        rows = slice(c * CHUNK, (c + 1) * CHUNK)
        for hd in range(d_a // A_HEAD_DIM):
            cols = slice(hd * A_HEAD_DIM, (hd + 1) * A_HEAD_DIM)
            mixed = jnp.dot(ws_ref[hd], vb[rows, cols], preferred_element_type=F32) + bsb_ref[hd]
            ya_s[rows, cols] = u[rows, cols] * mixed
    yan_ref[...] = _rms(ya_s[...], mga_ref[...]).astype(yan_ref.dtype)

    halo = V7X_SUBLANES
    for cb in range(d_b // cblk):
        cols = slice(cb * cblk, (cb + 1) * cblk)
        parts = []
        for p in range(3):
            z = jnp.dot(h_all, wb_ref[p, :, cols], preferred_element_type=F32)
            zs_s[...] = z
            zs_s[0:halo, :] = z[0:halo] * not_first
            zs_s[tm + halo:tm + 2 * halo, :] = z[tm + halo:tm + 2 * halo] * not_last
            w = cw_ref[p]
            zc = (cb_ref[p:p + 1, cols]
                  + zs_s[halo - 1:halo - 1 + tm, :] * w[0:1, cols]
                  + zs_s[halo:halo + tm, :] * w[1:2, cols]
                  + zs_s[halo + 1:halo + 1 + tm, :] * w[2:3, cols])
            parts.append(zc)
        x0_ref[:, cols] = parts[0]
        g_ref[:, cols] = parts[1] * parts[2]


def _inproj(x, seq_len, norm1, w_u, w_v, w_b, ln_g, ln_b, ws, bsb, conv_w, conv_b, mix_g_a, tm=512, cblk=256):
    t, d = x.shape
    d_a = w_u.shape[1]
    d_b = w_b.shape[2]
    assert seq_len % tm == 0 and t % seq_len == 0 and tm % CHUNK == 0 and d_b % cblk == 0, (t, seq_len)
    nblk = tm // V7X_SUBLANES
    last8 = t // V7X_SUBLANES - 1
    const = lambda *shape: pl.BlockSpec(shape, lambda i: (0,) * len(shape), pipeline_mode=pl.Buffered(1))
    kern = functools.partial(_inproj_kernel, tiles_per_seq=seq_len // tm, cblk=cblk)
    return pl.pallas_call(
        kern,
        grid=(t // tm,),
        in_specs=[
            pl.BlockSpec((V7X_SUBLANES, d), lambda i: (jnp.maximum(i * nblk - 1, 0), 0)),
            pl.BlockSpec((tm, d), lambda i: (i, 0)),
            pl.BlockSpec((V7X_SUBLANES, d), lambda i: (jnp.minimum((i + 1) * nblk, last8), 0)),
            const(1, d), const(d, d_a), const(d, d_a), const(3, d, d_b), const(1, d_a), const(1, d_a),
            const(*ws.shape), const(*bsb.shape), const(3, 3, d_b), const(3, d_b), const(1, d_a),
        ],
        out_specs=[
            pl.BlockSpec((tm, d_a), lambda i: (i, 0)),
            pl.BlockSpec((tm, d_b), lambda i: (i, 0)),
            pl.BlockSpec((tm, d_b), lambda i: (i, 0)),
        ],
        out_shape=[
            jax.ShapeDtypeStruct((t, d_a), BF16),
            jax.ShapeDtypeStruct((t, d_b), F32),
            jax.ShapeDtypeStruct((t, d_b), F32),
        ],
        scratch_shapes=[pltpu.VMEM((tm, d_a), F32), pltpu.VMEM((tm + 2 * V7X_SUBLANES, cblk), F32)],
        compiler_params=_cparams(("parallel",)),
        name="inproj",
    )(x, x, x, norm1, w_u, w_v, w_b, ln_g, ln_b, ws, bsb, conv_w, conv_b, mix_g_a)


def _filt_kernel(fr_ref, dl_ref, w1_ref, b1_ref, w2_ref, b2_ref, w3_ref, b3_ref, fq_ref, w4_ref, o_ref,
                 *, seq_len, nblocks, n2_per_step):
    half = FFT_N1 // 2
    tr = half * n2_per_step
    e = pl.program_id(0) - nblocks
    r = lax.broadcasted_iota(I32, (tr, 1), 0)
    n2 = pl.program_id(1) * n2_per_step + r // half
    trow = (r % half) * FFT_N2 + n2
    lag = e * CONV_BLOCK + trow
    pos = jnp.abs(lag).astype(F32)
    valid = jnp.abs(lag) <= seq_len - 1
    t = pos / np.float32(max(seq_len - 1, 1))
    fw = np.float32(2.0 * math.pi / seq_len) * pos * fr_ref[...]
    hi = lax.Precision.HIGHEST
    w1 = w1_ref[...]
    z1 = (t * w1[0:1, :]
          + jnp.dot(jnp.cos(fw), w1[1:1 + BANDS, :], precision=hi, preferred_element_type=F32)
          + jnp.dot(-jnp.sin(fw), w1[1 + BANDS:, :], precision=hi, preferred_element_type=F32))
    fq = fq_ref[...]
    h = jnp.sin(fq[0:1, :] * (z1 + b1_ref[...]))
    h = jnp.sin(fq[1:2, :] * (jnp.dot(h, w2_ref[...], precision=hi, preferred_element_type=F32) + b2_ref[...]))
    h = jnp.sin(fq[2:3, :] * (jnp.dot(h, w3_ref[...], precision=hi, preferred_element_type=F32) + b3_ref[...]))
    h4 = jnp.dot(h, w4_ref[...], precision=hi, preferred_element_type=F32)
    window = jnp.exp(-t * dl_ref[...])
    res = jnp.where(valid, h4 * window, 0.0)
    d_b = res.shape[1]
    for i in range(n2_per_step):
        o_ref[:, i * d_b:(i + 1) * d_b] = res[i * half:(i + 1) * half]


def _hyena_filter_blocks(seq_len, w1, b1, w2, b2, w3, b3, freq, w4, n2_per_step=16):
    nblocks = seq_len // CONV_BLOCK
    width = w1.shape[1]
    d_b = w4.shape[1] // 2
    fr = jnp.asarray(np.linspace(1e-4, BANDS - 1, BANDS, dtype=np.float32)[None, :])
    min_decay = math.log(DECAY_TARGET) / SLOW_DECAY_PCT
    max_decay = math.log(DECAY_TARGET) / FAST_DECAY_PCT
    deltas = jnp.asarray(np.abs(np.linspace(min_decay, max_decay, d_b, dtype=np.float32))[None, :])
    w4h = w4.reshape(width, 2, d_b).transpose(1, 0, 2)
    const = lambda *shape: pl.BlockSpec(shape, lambda a, b: (0,) * len(shape))
    half = FFT_N1 // 2
    kern = functools.partial(_filt_kernel, seq_len=seq_len, nblocks=nblocks, n2_per_step=n2_per_step)
    return pl.pallas_call(
        kern,
        grid=(2 * nblocks, FFT_N2 // n2_per_step),
        in_specs=[
            const(1, BANDS), const(1, d_b), const(EMB_DIM, width), const(1, width), const(width, width),
            const(1, width), const(width, width), const(1, width), const(3, width),
            pl.BlockSpec((None, width, d_b), lambda ei, r: (jnp.where(ei >= nblocks, 0, 1), 0, 0)),
        ],
        out_specs=pl.BlockSpec((None, half, n2_per_step * d_b), lambda ei, r: (ei, 0, r)),
        out_shape=jax.ShapeDtypeStruct((2 * nblocks, half, FFT_N2 * d_b), F32),
        compiler_params=_cparams(("parallel", "parallel")),
        name="hyena_filter",
    )(fr, deltas, w1, b1, w2, b2, w3, b3, freq, w4h)


def _dft_tables():
    n1 = np.arange(FFT_N1)
    f64 = np.exp(-2j * np.pi * np.outer(n1, n1) / FFT_N1)
    half = FFT_N1 // 2
    fr, fi = f64.real, f64.imag
    lhs_data = np.block([[fr[:, :half], -fi[:, :half]], [fi[:, :half], fr[:, :half]]])
    lhs_real = np.concatenate([fr, fi], axis=0)
    gr, gi = fr[:half, :] / FFT_N, -fi[:half, :] / FFT_N
    lhs_inv = np.block([[gr, -gi], [gi, gr]])
    k1 = np.arange(FFT_N1)[:, None, None]
    k2 = np.arange(FFT_N2)[None, :, None]
    n2 = np.arange(FFT_N2)[None, None, :]
    g = np.exp(-2j * np.pi * (n2 * (k1 + FFT_N1 * k2) % FFT_N) / FFT_N)
    gfwd = np.concatenate([np.concatenate([g.real, -g.imag], axis=2),
                           np.concatenate([g.imag, g.real], axis=2)], axis=1)
    ht = np.conj(np.transpose(g, (0, 2, 1)))
    ginv = np.concatenate([np.concatenate([ht.real, -ht.imag], axis=2),
                           np.concatenate([ht.imag, ht.real], axis=2)], axis=1)
    as32 = lambda a: np.asarray(a, dtype=np.float32)
    return as32(lhs_data), as32(lhs_real), as32(lhs_inv), as32(gfwd), as32(ginv)


def _split(x):
    hi = x.astype(BF16)
    lo = (x - hi.astype(F32)).astype(BF16)
    return hi, lo


def _dot3(a_hi, a_lo, b):
    if b.dtype == BF16:
        return jnp.dot(a_hi, b, preferred_element_type=F32) + jnp.dot(a_lo, b, preferred_element_type=F32)
    b_hi, b_lo = _split(b)
    acc = jnp.dot(a_hi, b_hi, preferred_element_type=F32)
    acc = acc + jnp.dot(a_hi, b_lo, preferred_element_type=F32)
    acc = acc + jnp.dot(a_lo, b_hi, preferred_element_type=F32)
    return acc


SPECTRA_DTYPE = BF16
FFT_TN2 = 16


def _rows_to_lanes(x):
    xt = jnp.swapaxes(x, 0, 1)
    return jnp.concatenate([xt[i] for i in range(xt.shape[0])], axis=1)


def _lanes_to_rows(x, s):
    c = x.shape[1] // s
    return jnp.swapaxes(jnp.stack([x[:, i * c:(i + 1) * c] for i in range(s)], axis=0), 0, 1)


def _fft1_kernel(lh_ref, ll_ref, a_ref, b_ref, o_ref, *, zero_lanes, natural):
    if natural:
        a = _rows_to_lanes(a_ref[...])
        b = _rows_to_lanes(b_ref[...])
    else:
        a = a_ref[...]
        b = b_ref[...]
    if zero_lanes:
        row = lax.broadcasted_iota(I32, b.shape, 0)
        lane = lax.broadcasted_iota(I32, b.shape, 1) + pl.program_id(2) * b.shape[1]
        b = jnp.where(jnp.logical_and(row == 0, lane < zero_lanes), 0.0, b)
    res = _dot3(lh_ref[...], ll_ref[...], jnp.concatenate([a, b], axis=0))
    o_ref[...] = _lanes_to_rows(res, FFT_TN2).reshape(o_ref.shape).astype(o_ref.dtype)


def _fft1(lhs, x, a_of, b_of, g, p, d_b, natural, zero_lanes=0):
    half = FFT_N1 // 2
    lh, ll = _split(jnp.asarray(lhs))
    kern = functools.partial(_fft1_kernel, zero_lanes=zero_lanes, natural=natural)
    if natural:
        blk = lambda of: pl.BlockSpec((None, None, half, FFT_TN2, d_b), lambda q, j, t: of(q, j) + (0, t, 0))
    else:
        blk = lambda of: pl.BlockSpec((None, None, half, FFT_TN2 * d_b), lambda q, j, t: of(q, j) + (0, t))
    return pl.pallas_call(
        kern,
        grid=(g, p, FFT_N2 // FFT_TN2),
        in_specs=[
            pl.BlockSpec(lhs.shape, lambda q, j, t: (0, 0)),
            pl.BlockSpec(lhs.shape, lambda q, j, t: (0, 0)),
            blk(a_of), blk(b_of),
        ],
        out_specs=pl.BlockSpec((None, None, 2, FFT_N1, FFT_TN2, d_b), lambda q, j, t: (q, j, 0, 0, t, 0)),
        out_shape=jax.ShapeDtypeStruct((g, p, 2, FFT_N1, FFT_N2, d_b), SPECTRA_DTYPE),
        compiler_params=_cparams(("parallel", "parallel", "parallel")),
        name="fft1",
    )(lh, ll, x, x)


def _fft3_kernel(lh_ref, ll_ref, w_ref, o_ref):
    half = FFT_N1 // 2
    c = w_ref.shape[-1]
    w = _rows_to_lanes(w_ref[...].astype(F32).reshape(2 * FFT_N1, FFT_TN2, c)).astype(w_ref.dtype)
    res = _dot3(lh_ref[...], ll_ref[...], w)
    o_ref[0] = _lanes_to_rows(res[:half], FFT_TN2)
    o_ref[1] = _lanes_to_rows(res[half:], FFT_TN2)


def _fft3(lhs, w6):
    g, p, _, _, _, d_b = w6.shape
    half = FFT_N1 // 2
    lh, ll = _split(jnp.asarray(lhs))
    return pl.pallas_call(
        _fft3_kernel,
        grid=(g, p, FFT_N2 // FFT_TN2),
        in_specs=[
            pl.BlockSpec(lhs.shape, lambda q, j, t: (0, 0)),
            pl.BlockSpec(lhs.shape, lambda q, j, t: (0, 0)),
            pl.BlockSpec((None, None, 2, FFT_N1, FFT_TN2, d_b), lambda q, j, t: (q, j, 0, 0, t, 0)),
        ],
        out_specs=pl.BlockSpec((None, 2, None, half, FFT_TN2, d_b), lambda q, j, t: (q, 0, j, 0, t, 0)),
        out_shape=jax.ShapeDtypeStruct((g, 2, p, half, FFT_N2, d_b), F32),
        compiler_params=_cparams(("parallel", "parallel", "parallel")),
        name="fft3",
    )(lh, ll, w6)


def _fft2_filter_kernel(gh_ref, gl_ref, x_ref, o_ref):
    ct = x_ref.shape[-1]
    for k in range(x_ref.shape[1]):
        z = _dot3(gh_ref[k], gl_ref[k], x_ref[:, k].reshape(2 * FFT_N2, ct))
        o_ref[:, k] = z.reshape(2, FFT_N2, ct).astype(o_ref.dtype)


def _fft2_filter(gfwd_hl, x1f, ct=256, kb=8):
    nd, _, _, _, d_b = x1f.shape
    gh, gl = gfwd_hl
    gspec = pl.BlockSpec((kb, 2 * FFT_N2, 2 * FFT_N2), lambda k, d, c: (k, 0, 0))
    xspec = pl.BlockSpec((None, 2, kb, FFT_N2, ct), lambda k, d, c: (d, 0, k, 0, c))
    return pl.pallas_call(
        _fft2_filter_kernel,
        grid=(FFT_N1 // kb, nd, d_b // ct),
        in_specs=[gspec, gspec, xspec],
        out_specs=xspec,
        out_shape=jax.ShapeDtypeStruct(x1f.shape, SPECTRA_DTYPE),
        compiler_params=_cparams(("parallel", "parallel", "parallel")),
        name="fft2_filter",
    )(gh, gl, x1f)


def _fft2_mix_kernel(gh_ref, gl_ref, ih_ref, il_ref, x_ref, k_ref, o_ref, *, nblocks):
    ct = x_ref.shape[-1]
    for k in range(x_ref.shape[2]):
        zs = []
        for j in range(nblocks):
            z = _dot3(gh_ref[k], gl_ref[k], x_ref[j, :, k].reshape(2 * FFT_N2, ct))
            zs.append((z[:FFT_N2], z[FFT_N2:]))
        for i in range(nblocks):
            yr = jnp.zeros((FFT_N2, ct), F32)
            yi = jnp.zeros((FFT_N2, ct), F32)
            for j in range(nblocks):
                d = i - j + nblocks - 1
                kr, ki = k_ref[d, 0, k].astype(F32), k_ref[d, 1, k].astype(F32)
                zr, zi = zs[j]
                yr = yr + kr * zr - ki * zi
                yi = yi + kr * zi + ki * zr
            w = _dot3(ih_ref[k], il_ref[k], jnp.concatenate([yr, yi], axis=0))
            o_ref[i, :, k] = w.reshape(2, FFT_N2, ct).astype(o_ref.dtype)


def _fft2_mix(gfwd_hl, ginv_hl, x1, kspec, ct=256):
    g, p, _, _, _, d_b = x1.shape
    nd = kspec.shape[0]
    kb = max(1, 8 // p)
    gspec = pl.BlockSpec((kb, 2 * FFT_N2, 2 * FFT_N2), lambda k, q, c: (k, 0, 0))
    xspec = pl.BlockSpec((None, p, 2, kb, FFT_N2, ct), lambda k, q, c: (q, 0, 0, k, 0, c))
    kern = functools.partial(_fft2_mix_kernel, nblocks=p)
    return pl.pallas_call(
        kern,
        grid=(FFT_N1 // kb, g, d_b // ct),
        in_specs=[gspec, gspec, gspec, gspec, xspec,
                  pl.BlockSpec((nd, 2, kb, FFT_N2, ct), lambda k, q, c: (0, 0, k, 0, c))],
        out_specs=xspec,
        out_shape=jax.ShapeDtypeStruct(x1.shape, SPECTRA_DTYPE),
        compiler_params=_cparams(("parallel", "parallel", "parallel")),
        name="fft2_mix",
    )(gfwd_hl[0], gfwd_hl[1], ginv_hl[0], ginv_hl[1], x1, kspec)


def _hyena_long_conv(g3, w1, b1, w2, b2, w3, b3, freq, w4):
    bsz, seq_len, d_b = g3.shape
    nblocks = seq_len // CONV_BLOCK
    half = FFT_N1 // 2
    lhs_data, lhs_real, lhs_inv, gfwd, ginv = _dft_tables()
    gfwd_hl = _split(jnp.asarray(gfwd))
    ginv_hl = _split(jnp.asarray(ginv))
    kext = _hyena_filter_blocks(seq_len, w1, b1, w2, b2, w3, b3, freq, w4)
    nd = 2 * nblocks - 1
    k1f = _fft1(lhs_real, kext[None], lambda q, j: (0, q + 1), lambda q, j: (0, q), nd, 1, d_b,
                natural=False, zero_lanes=d_b)
    kspec = _fft2_filter(gfwd_hl, k1f.reshape(nd, 2, FFT_N1, FFT_N2, d_b))
    g5 = g3.reshape(bsz, nblocks, half, FFT_N2, d_b)
    x1 = _fft1(lhs_data, g5, lambda q, j: (2 * q, j), lambda q, j: (2 * q + 1, j), bsz // 2, nblocks, d_b,
               natural=True)
    wmix = _fft2_mix(gfwd_hl, ginv_hl, x1, kspec)
    y6 = _fft3(lhs_inv, wmix)
    return y6.reshape(bsz, seq_len, d_b)


def _postmix_kernel(x_ref, yan_ref, x0_ref, g_ref, yc_ref, bias_ref, mgb_ref, wo_ref, n2_ref, x1_ref, xn_ref):
    d_a = yan_ref.shape[1]
    gg = g_ref[...]
    yb = x0_ref[...] * (yc_ref[...] + gg * bias_ref[...])
    ybn = _rms(yb, mgb_ref[...]).astype(BF16)
    y = jnp.dot(yan_ref[...], wo_ref[0:d_a, :], preferred_element_type=F32)
    y = y + jnp.dot(ybn, wo_ref[d_a:, :], preferred_element_type=F32)
    x1 = x_ref[...] + y
    x1_ref[...] = x1
    xn_ref[...] = _rms(x1, n2_ref[...]).astype(xn_ref.dtype)


def _postmix(x, yan, x0, g, yconv, bias, mix_g_b, w_out, norm2, tm=512):
    t, d = x.shape
    d_a = yan.shape[1]
    d_b = x0.shape[1]
    assert t % tm == 0, t
    row = lambda w: pl.BlockSpec((tm, w), lambda i: (i, 0))
    const = lambda *shape: pl.BlockSpec(shape, lambda i: (0,) * len(shape), pipeline_mode=pl.Buffered(1))
    return pl.pallas_call(
        _postmix_kernel,
        grid=(t // tm,),
        in_specs=[row(d), row(d_a), row(d_b), row(d_b), row(d_b), const(1, d_b), const(1, d_b),
                  const(d_a + d_b, d), const(1, d)],
        out_specs=[row(d), row(d)],
        out_shape=[jax.ShapeDtypeStruct((t, d), F32), jax.ShapeDtypeStruct((t, d), BF16)],
        compiler_params=_cparams(("parallel",)),
        name="postmix",
    )(x, yan, x0, g, yconv, bias, mix_g_b, w_out, norm2)


def _staircase():
    return [(i, j) for i in range(PEER_TOPK) for j in range(PEER_TOPK) if (i + 1) * (j + 1) <= PEER_TOPK]


ARGMAX_LANES = 2


def _stream_argmax(val_ref, tag_ref, nrows, prev, slab):
    neg = np.float32(-np.inf)
    best_v = [jnp.full(slab, neg, F32) for _ in range(ARGMAX_LANES)]
    best_i = [jnp.zeros(slab, I32) for _ in range(ARGMAX_LANES)]
    best_t = [jnp.zeros(slab, I32) for _ in range(ARGMAX_LANES)]
    for n in range(nrows):
        v = jnp.where(prev == n, neg, val_ref[n])
        val_ref[n] = v
        k = n % ARGMAX_LANES
        better = v > best_v[k]
        best_v[k] = jnp.maximum(v, best_v[k])
        best_i[k] = jnp.where(better, n, best_i[k])
        if tag_ref is not None:
            best_t[k] = jnp.where(better, tag_ref[n], best_t[k])
    width = ARGMAX_LANES
    while width > 1:
        width //= 2
        for k in range(width):
            va, vb = best_v[k], best_v[k + width]
            ia, ib = best_i[k], best_i[k + width]
            take_b = jnp.logical_or(vb > va, jnp.logical_and(vb == va, ib < ia))
            best_v[k] = jnp.where(take_b, vb, va)
            best_i[k] = jnp.where(take_b, ib, ia)
            best_t[k] = jnp.where(take_b, best_t[k + width], best_t[k])
    return best_v[0], best_i[0], best_t[0]


def _scores_kernel(xn_ref, wqt_ref, kbd_ref, s_ref, *, heads):
    tm = xn_ref.shape[0]
    hk = heads * HALF_KEY
    q_t = lax.dot_general(wqt_ref[...], xn_ref[...], (((1,), (1,)), ((), ())),
                          preferred_element_type=F32).astype(BF16)
    for half in range(2):
        s_t = jnp.dot(kbd_ref[half], q_t[half * hk:(half + 1) * hk], preferred_element_type=F32)
        s_ref[half] = s_t.reshape(N_KEYS, heads, tm)


def _peer_scores(xn, wqt, kbd, heads, tm=512):
    t, d = xn.shape
    assert t % tm == 0 and heads == V7X_SUBLANES, (t, heads)
    const = lambda *shape: pl.BlockSpec(shape, lambda i: (0,) * len(shape), pipeline_mode=pl.Buffered(1))
    kern = functools.partial(_scores_kernel, heads=heads)
    return pl.pallas_call(
        kern,
        grid=(t // tm,),
        in_specs=[pl.BlockSpec((tm, d), lambda i: (i, 0)), const(*wqt.shape), const(*kbd.shape)],
        out_specs=pl.BlockSpec((2, N_KEYS, heads, tm), lambda i: (0, 0, 0, i)),
        out_shape=jax.ShapeDtypeStruct((2, N_KEYS, heads, t), F32),
        compiler_params=_cparams(("parallel",)),
        name="peer_scores",
    )(xn, wqt, kbd)


SELECT_BLOCK = 128


def _peer_a_select_kernel(xn_ref, dlo_ref, dhi_ref, s_ref, a_ref, e_ref, gt_ref, s0_s, s1_s, c_s, ce_s, b_s, es_s,
                          *, heads):
    xn = xn_ref[...]
    a_lo = jnp.dot(xn, dlo_ref[...], preferred_element_type=F32)
    a_hi = jnp.dot(xn, dhi_ref[...], preferred_element_type=F32)
    a_ref[...] = pltpu.pack_elementwise([a_lo, a_hi], packed_dtype=BF16)

    slab = (heads, s_ref.shape[-1])
    no_pick = jnp.full(slab, -1, I32)
    half_s = (s0_s, s1_s)
    vals, idxs, prevs = ([], []), ([], []), [no_pick, no_pick]
    for half in range(2):
        half_s[half][...] = s_ref[half]
    for r in range(PEER_TOPK):
        for half in range(2):
            m, prevs[half], _ = _stream_argmax(half_s[half], None, N_KEYS, prevs[half], slab)
            vals[half].append(m)
            idxs[half].append(prevs[half])
    cands = _staircase()
    for p, (i, j) in enumerate(cands):
        c_s[p] = vals[0][i] + vals[1][j]
        ce_s[p] = idxs[0][i] * N_KEYS + idxs[1][j]
    prev = no_pick
    for r in range(PEER_TOPK):
        m, prev, e = _stream_argmax(c_s, ce_s, len(cands), prev, slab)
        es_s[r] = e
        b_s[r] = m
    best = b_s[...]
    ex = jnp.exp(best - jnp.max(best, axis=0, keepdims=True))
    gate = ex / jnp.sum(ex, axis=0, keepdims=True)
    gt_ref[...] = gate.reshape(PEER_TOPK * heads, slab[1]).T
    e_ref[...] = es_s[...].reshape(PEER_TOPK * heads, slab[1]).T


def _peer_a_select(xn, down_t, scores, heads, tm=1024, tn=1024):
    t, d = xn.shape
    n = down_t.shape[1]
    tb = SELECT_BLOCK
    nsel = PEER_TOPK * heads
    ncand = len(_staircase())
    ni = t // tm
    nj = n // (2 * tn)
    assert t % tm == 0 and n % (2 * tn) == 0 and heads == V7X_SUBLANES, (t, n, heads)
    assert nj * ni * tb == t, "one selection block per grid step must cover all tokens"
    blk = lambda j, i: j * ni + i
    kern = functools.partial(_peer_a_select_kernel, heads=heads)
    return pl.pallas_call(
        kern,
        grid=(nj, ni),
        in_specs=[pl.BlockSpec((tm, d), lambda j, i: (i, 0)),
                  pl.BlockSpec((d, tn), lambda j, i: (0, j)), pl.BlockSpec((d, tn), lambda j, i: (0, j + nj)),
                  pl.BlockSpec((2, N_KEYS, heads, tb), lambda j, i: (0, 0, 0, blk(j, i)))],
        out_specs=[pl.BlockSpec((tm, tn), lambda j, i: (i, j)),
                   pl.BlockSpec((tb, nsel), lambda j, i: (blk(j, i), 0)),
                   pl.BlockSpec((tb, nsel), lambda j, i: (blk(j, i), 0))],
        out_shape=[jax.ShapeDtypeStruct((t, n // 2), U32), jax.ShapeDtypeStruct((t, nsel), I32),
                   jax.ShapeDtypeStruct((t, nsel), F32)],
        scratch_shapes=[
            pltpu.VMEM((N_KEYS, heads, tb), F32),
            pltpu.VMEM((N_KEYS, heads, tb), F32),
            pltpu.VMEM((ncand, heads, tb), F32),
            pltpu.VMEM((ncand, heads, tb), I32),
            pltpu.VMEM((PEER_TOPK, heads, tb), F32),
            pltpu.VMEM((PEER_TOPK, heads, tb), I32),
        ],
        compiler_params=_cparams(("parallel", "parallel")),
        name="peer_a_select",
    )(xn, down_t, down_t, scores)


PEER_B_GROUP = 16


def _peer_b_pick(a_ref, e_ref, gt_ref, rows):
    e = e_ref[rows, :]
    ai = lax.shift_right_logical(e, KEY_SHIFT)
    bi = e & (N_KEYS - 1)
    a_word = ai & (N_KEYS // 2 - 1)
    accs = [jnp.zeros(e.shape, U32) for _ in range(4)]
    for a in range(N_KEYS // 2):
        blk = a_ref[rows, a * N_KEYS:(a + 1) * N_KEYS]
        accs[a % 4] = jnp.where(a_word == a, jnp.take_along_axis(blk, bi, axis=1), accs[a % 4])
    word = (accs[0] | accs[1]) | (accs[2] | accs[3])
    lo = pltpu.unpack_elementwise(word, index=0, packed_dtype=BF16, unpacked_dtype=F32)
    hi = pltpu.unpack_elementwise(word, index=1, packed_dtype=BF16, unpacked_dtype=F32)
    picked = jnp.where(ai >= N_KEYS // 2, hi, lo)
    return gt_ref[rows, :] * _gelu(picked), e


def _peer_b_scatter(m_ref, rows, w, e):
    sub = lax.broadcasted_iota(I32, (e.shape[0], N_KEYS, e.shape[1]), 1)
    ai = lax.shift_right_logical(e, KEY_SHIFT)
    bi = e & (N_KEYS - 1)
    w1t = jnp.where(sub == ai[:, None, :], w[:, None, :], 0.0).astype(BF16)
    e2t = jnp.where(sub == bi[:, None, :], 1.0, 0.0).astype(BF16)
    m3 = jnp.einsum("cas,cbs->cab", w1t, e2t, preferred_element_type=F32)
    mt = jnp.swapaxes(m3, 0, 1).astype(m_ref.dtype)
    for a in range(N_KEYS):
        m_ref[rows, a * N_KEYS:(a + 1) * N_KEYS] = mt[a]


def _peer_b_kernel(a_ref, e_ref, gt_ref, m_ref):
    grp = PEER_B_GROUP
    ngrp = a_ref.shape[0] // grp

    def rows_of(gi):
        return pl.ds(pl.multiple_of(gi * grp, grp), grp)

    def step(gi, carry):
        nxt = _peer_b_pick(a_ref, e_ref, gt_ref, rows_of(gi))
        _peer_b_scatter(m_ref, rows_of(gi - 1), *carry)
        return nxt

    last = lax.fori_loop(1, ngrp, step, _peer_b_pick(a_ref, e_ref, gt_ref, rows_of(0)))
    _peer_b_scatter(m_ref, rows_of(ngrp - 1), *last)


def _peer_b(a2, eidx, gate, tc=256):
    t = a2.shape[0]
    n = 2 * a2.shape[1]
    nsel = eidx.shape[1]
    assert t % tc == 0 and tc % PEER_B_GROUP == 0 and n == N_KEYS * N_KEYS and nsel == N_KEYS, (t, n, nsel)
    return pl.pallas_call(
        _peer_b_kernel,
        grid=(t // tc,),
        in_specs=[pl.BlockSpec((tc, n // 2), lambda i: (i, 0)), pl.BlockSpec((tc, nsel), lambda i: (i, 0)),
                  pl.BlockSpec((tc, nsel), lambda i: (i, 0))],
        out_specs=pl.BlockSpec((tc, n), lambda i: (i, 0)),
        out_shape=jax.ShapeDtypeStruct((t, n), BF16),
        compiler_params=_cparams(("parallel",)),
        name="peer_b",
    )(a2, eidx, gate)


def _peer_c_kernel(m_ref, up_ref, x1_ref, fn_ref, o_ref, acc_s):
    k = pl.program_id(1)

    @pl.when(k == 0)
    def _():
        acc_s[...] = jnp.zeros_like(acc_s)

    acc_s[...] += jnp.dot(m_ref[...], up_ref[...], preferred_element_type=F32)

    @pl.when(k == pl.num_programs(1) - 1)
    def _():
        o_ref[...] = _rms(x1_ref[...] + acc_s[...], fn_ref[...])


def _peer_c(m, up, x1, final_norm, tm=1024, tk=1024):
    t, n = m.shape
    d = up.shape[1]
    assert t % tm == 0 and n % tk == 0, (t, n)
    return pl.pallas_call(
        _peer_c_kernel,
        grid=(t // tm, n // tk),
        in_specs=[pl.BlockSpec((tm, tk), lambda i, k: (i, k)), pl.BlockSpec((tk, d), lambda i, k: (k, 0)),
                  pl.BlockSpec((tm, d), lambda i, k: (i, 0), pipeline_mode=pl.Buffered(1)),
                  pl.BlockSpec((1, d), lambda i, k: (0, 0))],
        out_specs=pl.BlockSpec((tm, d), lambda i, k: (i, 0)),
        out_shape=jax.ShapeDtypeStruct((t, d), F32),
        scratch_shapes=[pltpu.VMEM((tm, d), F32)],
        compiler_params=_cparams(("parallel", "arbitrary")),
        name="peer_c",
    )(m, up, x1, final_norm)


def _peer_cb_kernel(m_ref, up_ref, x1_ref, fn_ref, a_ref, e_ref, gt_ref, o_ref, mo_ref, acc_s):
    k = pl.program_id(1)

    @pl.when(k == 0)
    def _():
        acc_s[...] = jnp.zeros_like(acc_s)

    acc_s[...] += jnp.dot(m_ref[...], up_ref[...], preferred_element_type=F32)
    grp = PEER_B_GROUP
    picked = [_peer_b_pick(a_ref, e_ref, gt_ref, slice(gi * grp, (gi + 1) * grp))
              for gi in range(a_ref.shape[0] // grp)]
    for gi, (w, e) in enumerate(picked):
        _peer_b_scatter(mo_ref, slice(gi * grp, (gi + 1) * grp), w, e)

    @pl.when(k == pl.num_programs(1) - 1)
    def _():
        o_ref[...] = _rms(x1_ref[...] + acc_s[...], fn_ref[...])


def _peer_cb(m, up, x1, final_norm, a2, eidx, gate, tm=512, tk=1024):
    t, n = m.shape
    d = up.shape[1]
    nsel = eidx.shape[1]
    nk = n // tk
    tb = t // ((t // tm) * nk)
    assert t % tm == 0 and n % tk == 0 and a2.shape == (t, n // 2), (t, n, a2.shape)
    assert tb * (t // tm) * nk == t and tb % PEER_B_GROUP == 0, (t, tb)
    blk = lambda i, k: i * nk + k
    return pl.pallas_call(
        _peer_cb_kernel,
        grid=(t // tm, nk),
        in_specs=[pl.BlockSpec((tm, tk), lambda i, k: (i, k)), pl.BlockSpec((tk, d), lambda i, k: (k, 0)),
                  pl.BlockSpec((tm, d), lambda i, k: (i, 0), pipeline_mode=pl.Buffered(1)),
                  pl.BlockSpec((1, d), lambda i, k: (0, 0)),
                  pl.BlockSpec((tb, n // 2), lambda i, k: (blk(i, k), 0)),
                  pl.BlockSpec((tb, nsel), lambda i, k: (blk(i, k), 0)),
                  pl.BlockSpec((tb, nsel), lambda i, k: (blk(i, k), 0))],
        out_specs=[pl.BlockSpec((tm, d), lambda i, k: (i, 0)),
                   pl.BlockSpec((tb, n), lambda i, k: (blk(i, k), 0))],
        out_shape=[jax.ShapeDtypeStruct((t, d), F32), jax.ShapeDtypeStruct((t, n), BF16)],
        scratch_shapes=[pltpu.VMEM((tm, d), F32)],
        compiler_params=_cparams(("parallel", "arbitrary")),
        name="peer_cb",
    )(m, up, x1, final_norm, a2, eidx, gate)


def _prep_weights(norm1, w_in, a_ln_g, a_ln_b, a_ws, a_bs, b_conv_w, b_conv_b, b_bias, mix_norm, w_out, norm2,
                  peer_wq, peer_k1, peer_k2, peer_down, peer_up, final_norm):
    d = w_in.shape[0]
    d_a = a_ln_g.shape[0]
    d_b = b_bias.shape[0]
    heads, n_keys, half_key = peer_k1.shape
    row = lambda v: v.reshape(1, -1).astype(F32)
    w = dict(
        norm1=row(norm1), norm2=row(norm2), final_norm=row(final_norm),
        w_u=w_in[:, :d_a].astype(BF16), w_v=w_in[:, d_a:2 * d_a].astype(BF16),
        w_b=w_in[:, 2 * d_a:].reshape(d, 3, d_b).transpose(1, 0, 2).astype(BF16),
        ln_g=row(a_ln_g), ln_b=row(a_ln_b),
        ws=a_ws.astype(BF16),
        bsb=jnp.broadcast_to(a_bs[:, :, None], a_bs.shape + (A_HEAD_DIM,)).astype(F32),
        conv_w=b_conv_w.reshape(3, 3, d_b).transpose(1, 0, 2).astype(F32),
        conv_b=b_conv_b.reshape(3, d_b).astype(F32),
        bias=row(b_bias), mix_g_a=row(mix_norm[:d_a]), mix_g_b=row(mix_norm[d_a:]),
        w_out=w_out.astype(BF16),
        down_t=peer_down.astype(BF16).T, up=peer_up.astype(BF16),
    )
    wq_t = peer_wq.T.reshape(heads, 2, half_key, d).transpose(1, 0, 2, 3).reshape(2 * heads * half_key, d)
    w["wq_t"] = wq_t.astype(BF16)
    eye = jnp.eye(heads, dtype=F32)
    kbd = [jnp.einsum("hnd,hg->nhgd", k, eye).reshape(n_keys * heads, heads * half_key) for k in (peer_k1, peer_k2)]
    w["kbd"] = jnp.stack(kbd).astype(BF16)
    w["heads"] = heads
    return w


def _trunk_front(x3, w, hf):
    bsz, seq_len, d = x3.shape
    t = bsz * seq_len
    x = x3.reshape(t, d)
    yan, x0, g = _inproj(x, seq_len, w["norm1"], w["w_u"], w["w_v"], w["w_b"], w["ln_g"], w["ln_b"],
                         w["ws"], w["bsb"], w["conv_w"], w["conv_b"], w["mix_g_a"])
    d_b = x0.shape[1]
    yconv = _hyena_long_conv(g.reshape(bsz, seq_len, d_b), *hf).reshape(t, d_b)
    x1, xn = _postmix(x, yan, x0, g, yconv, w["bias"], w["mix_g_b"], w["w_out"], w["norm2"])
    scores = _peer_scores(xn, w["wq_t"], w["kbd"], w["heads"])
    a2, eidx, gate = _peer_a_select(xn, w["down_t"], scores, w["heads"])
    return x1, a2, eidx, gate


def kernel(x_prompt, x_sample, norm1, w_in, a_ln_g, a_ln_b, a_ws, a_bs, b_conv_w, b_conv_b, hf_w1, hf_b1, hf_w2, hf_b2, hf_w3, hf_b3, hf_freq, hf_w4, b_bias, mix_norm, w_out, norm2, peer_wq, peer_k1, peer_k2, peer_down, peer_up, final_norm):
    assert norm1.shape[0] == 1, "single-layer trunk"
    w = _prep_weights(norm1[0], w_in[0], a_ln_g[0], a_ln_b[0], a_ws[0], a_bs[0], b_conv_w[0], b_conv_b[0],
                      b_bias[0], mix_norm[0], w_out[0], norm2[0], peer_wq[0], peer_k1[0], peer_k2[0],
                      peer_down[0], peer_up[0], final_norm)
    hf = (hf_w1[0], hf_b1[0].reshape(1, -1), hf_w2[0], hf_b2[0].reshape(1, -1), hf_w3[0], hf_b3[0].reshape(1, -1),
          hf_freq[0], hf_w4[0])
    x1_p, a2_p, e_p, g_p = _trunk_front(x_prompt, w, hf)
    x1_s, a2_s, e_s, g_s = _trunk_front(x_sample, w, hf)
    m_p = _peer_b(a2_p, e_p, g_p)
    if x1_p.shape == x1_s.shape:
        out_p, m_s = _peer_cb(m_p, w["up"], x1_p, w["final_norm"], a2_s, e_s, g_s)
    else:
        out_p = _peer_c(m_p, w["up"], x1_p, w["final_norm"])
        m_s = _peer_b(a2_s, e_s, g_s)
    out_s = _peer_c(m_s, w["up"], x1_s, w["final_norm"])
    return (out_p.reshape(x_prompt.shape), out_s.reshape(x_sample.shape))
```

```python
import functools
import math

import numpy as np
import jax
import jax.numpy as jnp
from jax import lax
from jax.experimental import pallas as pl
from jax.experimental.pallas import tpu as pltpu

F32 = jnp.float32
BF16 = jnp.bfloat16
I32 = jnp.int32
U32 = jnp.uint32

EPS = 1e-6
V7X_LANES = 128
V7X_SUBLANES = 8
V7X_MXU_WIDTH = 256
V7X_VMEM_BYTES = 64 * 1024 * 1024
VMEM_LIMIT = V7X_VMEM_BYTES - 8 * 1024 * 1024

CHUNK = 128
A_HEAD_DIM = 128
EMB_DIM = 33
BANDS = (EMB_DIM - 1) // 2
DECAY_TARGET = 1e-2
FAST_DECAY_PCT = 0.3
SLOW_DECAY_PCT = 1.5
FFT_N1 = 64
FFT_N2 = 128
FFT_N = FFT_N1 * FFT_N2
CONV_BLOCK = FFT_N // 2
N_KEYS = 128
KEY_SHIFT = 7
PEER_TOPK = 16
HALF_KEY = 128


def _cparams(sem):
    return pltpu.CompilerParams(dimension_semantics=sem, vmem_limit_bytes=VMEM_LIMIT)


def _rms(xf, g):
    return xf * lax.rsqrt(jnp.mean(xf * xf, axis=-1, keepdims=True) + EPS) * g


def _gelu(x):
    return 0.5 * x * (1.0 + lax.erf(x * np.float32(math.sqrt(0.5))))


def _inproj_kernel(xp_ref, x_ref, xn_ref, n1_ref, wu_ref, wv_ref, wb_ref, lng_ref, lnb_ref,
                   ws_ref, bsb_ref, cw_ref, cb_ref, mga_ref,
                   yan_ref, x0_ref, g_ref, ya_s, zs_s, *, tiles_per_seq, cblk):
    tm = x_ref.shape[0]
    d_a = wu_ref.shape[1]
    d_b = wb_ref.shape[2]
    i = pl.program_id(0)
    not_first = (i % tiles_per_seq != 0).astype(F32)
    not_last = (i % tiles_per_seq != tiles_per_seq - 1).astype(F32)

    xcat = jnp.concatenate([xp_ref[...], x_ref[...], xn_ref[...]], axis=0)
    hcat = _rms(xcat, n1_ref[...])
    h_all = hcat.astype(BF16)
    h = hcat[V7X_SUBLANES:V7X_SUBLANES + tm].astype(BF16)

    u = _gelu(jnp.dot(h, wu_ref[...], preferred_element_type=F32))
    v = _gelu(jnp.dot(h, wv_ref[...], preferred_element_type=F32))
    mu = jnp.mean(v, axis=-1, keepdims=True)
    vc = v - mu
    var = jnp.mean(vc * vc, axis=-1, keepdims=True)
    vb = (vc * lax.rsqrt(var + EPS) * lng_ref[...] + lnb_ref[...]).astype(BF16)
    for c in range(tm // CHUNK):
        rows = slice(c * CHUNK, (c + 1) * CHUNK)
        for hd in range(d_a // A_HEAD_DIM):
            cols = slice(hd * A_HEAD_DIM, (hd + 1) * A_HEAD_DIM)
            mixed = jnp.dot(ws_ref[hd], vb[rows, cols], preferred_element_type=F32) + bsb_ref[hd]
            ya_s[rows, cols] = u[rows, cols] * mixed
    yan_ref[...] = _rms(ya_s[...], mga_ref[...]).astype(yan_ref.dtype)

    halo = V7X_SUBLANES
    for cb in range(d_b // cblk):
        cols = slice(cb * cblk, (cb + 1) * cblk)
        parts = []
        for p in range(3):
            z = jnp.dot(h_all, wb_ref[p, :, cols], preferred_element_type=F32)
            zs_s[...] = z
            zs_s[0:halo, :] = z[0:halo] * not_first
            zs_s[tm + halo:tm + 2 * halo, :] = z[tm + halo:tm + 2 * halo] * not_last
            w = cw_ref[p]
            zc = (cb_ref[p:p + 1, cols]
                  + zs_s[halo - 1:halo - 1 + tm, :] * w[0:1, cols]
                  + zs_s[halo:halo + tm, :] * w[1:2, cols]
                  + zs_s[halo + 1:halo + 1 + tm, :] * w[2:3, cols])
            parts.append(zc)
        x0_ref[:, cols] = parts[0]
        g_ref[:, cols] = parts[1] * parts[2]


def _inproj(x, seq_len, norm1, w_u, w_v, w_b, ln_g, ln_b, ws, bsb, conv_w, conv_b, mix_g_a, tm=512, cblk=256):
    t, d = x.shape
    d_a = w_u.shape[1]
    d_b = w_b.shape[2]
    assert seq_len % tm == 0 and t % seq_len == 0 and tm % CHUNK == 0 and d_b % cblk == 0, (t, seq_len)
    nblk = tm // V7X_SUBLANES
    last8 = t // V7X_SUBLANES - 1
    const = lambda *shape: pl.BlockSpec(shape, lambda i: (0,) * len(shape), pipeline_mode=pl.Buffered(1))
    kern = functools.partial(_inproj_kernel, tiles_per_seq=seq_len // tm, cblk=cblk)
    return pl.pallas_call(
        kern,
        grid=(t // tm,),
        in_specs=[
            pl.BlockSpec((V7X_SUBLANES, d), lambda i: (jnp.maximum(i * nblk - 1, 0), 0)),
            pl.BlockSpec((tm, d), lambda i: (i, 0)),
            pl.BlockSpec((V7X_SUBLANES, d), lambda i: (jnp.minimum((i + 1) * nblk, last8), 0)),
            const(1, d), const(d, d_a), const(d, d_a), const(3, d, d_b), const(1, d_a), const(1, d_a),
            const(*ws.shape), const(*bsb.shape), const(3, 3, d_b), const(3, d_b), const(1, d_a),
        ],
        out_specs=[
            pl.BlockSpec((tm, d_a), lambda i: (i, 0)),
            pl.BlockSpec((tm, d_b), lambda i: (i, 0)),
            pl.BlockSpec((tm, d_b), lambda i: (i, 0)),
        ],
        out_shape=[
            jax.ShapeDtypeStruct((t, d_a), BF16),
            jax.ShapeDtypeStruct((t, d_b), F32),
            jax.ShapeDtypeStruct((t, d_b), F32),
        ],
        scratch_shapes=[pltpu.VMEM((tm, d_a), F32), pltpu.VMEM((tm + 2 * V7X_SUBLANES, cblk), F32)],
        compiler_params=_cparams(("parallel",)),
        name="inproj",
    )(x, x, x, norm1, w_u, w_v, w_b, ln_g, ln_b, ws, bsb, conv_w, conv_b, mix_g_a)


def _filt_kernel(fr_ref, dl_ref, w1_ref, b1_ref, w2_ref, b2_ref, w3_ref, b3_ref, fq_ref, w4_ref, o_ref,
                 *, seq_len, nblocks, n2_per_step):
    half = FFT_N1 // 2
    tr = half * n2_per_step
    e = pl.program_id(0) - nblocks

    def lags(shape, axis):
        r = lax.broadcasted_iota(I32, shape, axis)
        n2 = pl.program_id(1) * n2_per_step + r // half
        return e * CONV_BLOCK + (r % half) * FFT_N2 + n2

    inv_len = np.float32(max(seq_len - 1, 1))
    pos = jnp.abs(lags((1, tr), 1)).astype(F32)
    t = pos / inv_len
    fw = (np.float32(2.0 * math.pi / seq_len) * pos) * fr_ref[...]
    hi = lax.Precision.HIGHEST
    w1 = w1_ref[...]
    z1 = (w1[:, 0:1] * t
          + jnp.dot(w1[:, 1:1 + BANDS], jnp.cos(fw), precision=hi, preferred_element_type=F32)
          + jnp.dot(w1[:, 1 + BANDS:], -jnp.sin(fw), precision=hi, preferred_element_type=F32))
    fq = fq_ref[...]
    h = jnp.sin(fq[:, 0:1] * (z1 + b1_ref[...]))
    h = jnp.sin(fq[:, 1:2] * (jnp.dot(w2_ref[...], h, precision=hi, preferred_element_type=F32) + b2_ref[...]))
    h = jnp.sin(fq[:, 2:3] * (jnp.dot(w3_ref[...], h, precision=hi, preferred_element_type=F32) + b3_ref[...]))
    h4 = jnp.dot(h.T, w4_ref[...], precision=hi, preferred_element_type=F32)
    lag_col = lags((tr, 1), 0)
    t_col = jnp.abs(lag_col).astype(F32) / inv_len
    window = jnp.exp(-t_col * dl_ref[...])
    res = jnp.where(jnp.abs(lag_col) <= seq_len - 1, h4 * window, 0.0)
    d_b = res.shape[1]
    for i in range(n2_per_step):
        o_ref[:, i * d_b:(i + 1) * d_b] = res[i * half:(i + 1) * half]


def _hyena_filter_blocks(seq_len, w1, b1, w2, b2, w3, b3, freq, w4, n2_per_step=16):
    nblocks = seq_len // CONV_BLOCK
    width = w1.shape[1]
    d_b = w4.shape[1] // 2
    fr = jnp.asarray(np.linspace(1e-4, BANDS - 1, BANDS, dtype=np.float32)[:, None])
    min_decay = math.log(DECAY_TARGET) / SLOW_DECAY_PCT
    max_decay = math.log(DECAY_TARGET) / FAST_DECAY_PCT
    deltas = jnp.asarray(np.abs(np.linspace(min_decay, max_decay, d_b, dtype=np.float32))[None, :])
    w4h = w4.reshape(width, 2, d_b).transpose(1, 0, 2)
    const = lambda *shape: pl.BlockSpec(shape, lambda a, b: (0,) * len(shape))
    half = FFT_N1 // 2
    kern = functools.partial(_filt_kernel, seq_len=seq_len, nblocks=nblocks, n2_per_step=n2_per_step)
    return pl.pallas_call(
        kern,
        grid=(2 * nblocks, FFT_N2 // n2_per_step),
        in_specs=[
            const(BANDS, 1), const(1, d_b), const(width, EMB_DIM), const(width, 1), const(width, width),
            const(width, 1), const(width, width), const(width, 1), const(width, 3),
            pl.BlockSpec((None, width, d_b), lambda ei, r: (jnp.where(ei >= nblocks, 0, 1), 0, 0)),
        ],
        out_specs=pl.BlockSpec((None, half, n2_per_step * d_b), lambda ei, r: (ei, 0, r)),
        out_shape=jax.ShapeDtypeStruct((2 * nblocks, half, FFT_N2 * d_b), F32),
        compiler_params=_cparams(("parallel", "parallel")),
        name="hyena_filter",
    )(fr, deltas, w1.T, b1.reshape(width, 1), w2.T, b2.reshape(width, 1), w3.T, b3.reshape(width, 1), freq.T, w4h)


def _dft_tables():
    n1 = np.arange(FFT_N1)
    f64 = np.exp(-2j * np.pi * np.outer(n1, n1) / FFT_N1)
    half = FFT_N1 // 2
    fr, fi = f64.real, f64.imag
    lhs_data = np.block([[fr[:, :half], -fi[:, :half]], [fi[:, :half], fr[:, :half]]])
    lhs_real = np.concatenate([fr, fi], axis=0)
    gr, gi = fr[:half, :] / FFT_N, -fi[:half, :] / FFT_N
    lhs_inv = np.block([[gr, -gi], [gi, gr]])
    k1 = np.arange(FFT_N1)[:, None, None]
    k2 = np.arange(FFT_N2)[None, :, None]
    n2 = np.arange(FFT_N2)[None, None, :]
    g = np.exp(-2j * np.pi * (n2 * (k1 + FFT_N1 * k2) % FFT_N) / FFT_N)
    gfwd = np.concatenate([np.concatenate([g.real, -g.imag], axis=2),
                           np.concatenate([g.imag, g.real], axis=2)], axis=1)
    ht = np.conj(np.transpose(g, (0, 2, 1)))
    ginv = np.concatenate([np.concatenate([ht.real, -ht.imag], axis=2),
                           np.concatenate([ht.imag, ht.real], axis=2)], axis=1)
    as32 = lambda a: np.asarray(a, dtype=np.float32)
    return as32(lhs_data), as32(lhs_real), as32(lhs_inv), as32(gfwd), as32(ginv)


def _split(x):
    hi = x.astype(BF16)
    lo = (x - hi.astype(F32)).astype(BF16)
    return hi, lo


def _dot3(a_hi, a_lo, b):
    if b.dtype == BF16:
        return jnp.dot(a_hi, b, preferred_element_type=F32) + jnp.dot(a_lo, b, preferred_element_type=F32)
    b_hi, b_lo = _split(b)
    acc = jnp.dot(a_hi, b_hi, preferred_element_type=F32)
    acc = acc + jnp.dot(a_hi, b_lo, preferred_element_type=F32)
    acc = acc + jnp.dot(a_lo, b_hi, preferred_element_type=F32)
    return acc


SPECTRA_DTYPE = BF16
FFT_TN2 = 16


def _rows_to_lanes(x):
    xt = jnp.swapaxes(x, 0, 1)
    return jnp.concatenate([xt[i] for i in range(xt.shape[0])], axis=1)


def _lanes_to_rows(x, s):
    c = x.shape[1] // s
    return jnp.swapaxes(jnp.stack([x[:, i * c:(i + 1) * c] for i in range(s)], axis=0), 0, 1)


def _fft1_kernel(lh_ref, ll_ref, a_ref, b_ref, o_ref, *, zero_lanes, natural):
    if natural:
        a = _rows_to_lanes(a_ref[...])
        b = _rows_to_lanes(b_ref[...])
    else:
        a = a_ref[...]
        b = b_ref[...]
    if zero_lanes:
        row = lax.broadcasted_iota(I32, b.shape, 0)
        lane = lax.broadcasted_iota(I32, b.shape, 1) + pl.program_id(2) * b.shape[1]
        b = jnp.where(jnp.logical_and(row == 0, lane < zero_lanes), 0.0, b)
    res = _dot3(lh_ref[...], ll_ref[...], jnp.concatenate([a, b], axis=0))
    o_ref[...] = _lanes_to_rows(res, FFT_TN2).reshape(o_ref.shape).astype(o_ref.dtype)


def _fft1(lhs, x, a_of, b_of, g, p, d_b, natural, zero_lanes=0):
    half = FFT_N1 // 2
    lh, ll = _split(jnp.asarray(lhs))
    kern = functools.partial(_fft1_kernel, zero_lanes=zero_lanes, natural=natural)
    if natural:
        blk = lambda of: pl.BlockSpec((None, None, half, FFT_TN2, d_b), lambda q, j, t: of(q, j) + (0, t, 0))
    else:
        blk = lambda of: pl.BlockSpec((None, None, half, FFT_TN2 * d_b), lambda q, j, t: of(q, j) + (0, t))
    return pl.pallas_call(
        kern,
        grid=(g, p, FFT_N2 // FFT_TN2),
        in_specs=[
            pl.BlockSpec(lhs.shape, lambda q, j, t: (0, 0)),
            pl.BlockSpec(lhs.shape, lambda q, j, t: (0, 0)),
            blk(a_of), blk(b_of),
        ],
        out_specs=pl.BlockSpec((None, None, 2, FFT_N1, FFT_TN2, d_b), lambda q, j, t: (q, j, 0, 0, t, 0)),
        out_shape=jax.ShapeDtypeStruct((g, p, 2, FFT_N1, FFT_N2, d_b), SPECTRA_DTYPE),
        compiler_params=_cparams(("parallel", "parallel", "parallel")),
        name="fft1",
    )(lh, ll, x, x)


def _fft3_kernel(lh_ref, ll_ref, w_ref, o_ref):
    half = FFT_N1 // 2
    c = w_ref.shape[-1]
    w = _rows_to_lanes(w_ref[...].astype(F32).reshape(2 * FFT_N1, FFT_TN2, c)).astype(w_ref.dtype)
    res = _dot3(lh_ref[...], ll_ref[...], w)
    o_ref[0] = _lanes_to_rows(res[:half], FFT_TN2)
    o_ref[1] = _lanes_to_rows(res[half:], FFT_TN2)


def _fft3(lhs, w6):
    g, p, _, _, _, d_b = w6.shape
    half = FFT_N1 // 2
    lh, ll = _split(jnp.asarray(lhs))
    return pl.pallas_call(
        _fft3_kernel,
        grid=(g, p, FFT_N2 // FFT_TN2),
        in_specs=[
            pl.BlockSpec(lhs.shape, lambda q, j, t: (0, 0)),
            pl.BlockSpec(lhs.shape, lambda q, j, t: (0, 0)),
            pl.BlockSpec((None, None, 2, FFT_N1, FFT_TN2, d_b), lambda q, j, t: (q, j, 0, 0, t, 0)),
        ],
        out_specs=pl.BlockSpec((None, 2, None, half, FFT_TN2, d_b), lambda q, j, t: (q, 0, j, 0, t, 0)),
        out_shape=jax.ShapeDtypeStruct((g, 2, p, half, FFT_N2, d_b), F32),
        compiler_params=_cparams(("parallel", "parallel", "parallel")),
        name="fft3",
    )(lh, ll, w6)


def _fft2_filter_kernel(gh_ref, gl_ref, x_ref, o_ref):
    ct = x_ref.shape[-1]
    for k in range(x_ref.shape[1]):
        z = _dot3(gh_ref[k], gl_ref[k], x_ref[:, k].reshape(2 * FFT_N2, ct))
        o_ref[:, k] = z.reshape(2, FFT_N2, ct).astype(o_ref.dtype)


def _fft2_filter(gfwd_hl, x1f, ct=256, kb=8):
    nd, _, _, _, d_b = x1f.shape
    gh, gl = gfwd_hl
    gspec = pl.BlockSpec((kb, 2 * FFT_N2, 2 * FFT_N2), lambda k, d, c: (k, 0, 0))
    xspec = pl.BlockSpec((None, 2, kb, FFT_N2, ct), lambda k, d, c: (d, 0, k, 0, c))
    return pl.pallas_call(
        _fft2_filter_kernel,
        grid=(FFT_N1 // kb, nd, d_b // ct),
        in_specs=[gspec, gspec, xspec],
        out_specs=xspec,
        out_shape=jax.ShapeDtypeStruct(x1f.shape, SPECTRA_DTYPE),
        compiler_params=_cparams(("parallel", "parallel", "parallel")),
        name="fft2_filter",
    )(gh, gl, x1f)


def _fft2_mix_kernel(gh_ref, gl_ref, ih_ref, il_ref, x_ref, k_ref, o_ref, *, nblocks):
    ct = x_ref.shape[-1]
    for k in range(x_ref.shape[2]):
        zs = []
        for j in range(nblocks):
            z = _dot3(gh_ref[k], gl_ref[k], x_ref[j, :, k].reshape(2 * FFT_N2, ct))
            zs.append((z[:FFT_N2], z[FFT_N2:]))
        for i in range(nblocks):
            yr = jnp.zeros((FFT_N2, ct), F32)
            yi = jnp.zeros((FFT_N2, ct), F32)
            for j in range(nblocks):
                d = i - j + nblocks - 1
                kr, ki = k_ref[d, 0, k].astype(F32), k_ref[d, 1, k].astype(F32)
                zr, zi = zs[j]
                yr = yr + kr * zr - ki * zi
                yi = yi + kr * zi + ki * zr
            w = _dot3(ih_ref[k], il_ref[k], jnp.concatenate([yr, yi], axis=0))
            o_ref[i, :, k] = w.reshape(2, FFT_N2, ct).astype(o_ref.dtype)


def _fft2_mix(gfwd_hl, ginv_hl, x1, kspec, ct=256):
    g, p, _, _, _, d_b = x1.shape
    nd = kspec.shape[0]
    kb = max(1, 8 // p)
    gspec = pl.BlockSpec((kb, 2 * FFT_N2, 2 * FFT_N2), lambda k, q, c: (k, 0, 0))
    xspec = pl.BlockSpec((None, p, 2, kb, FFT_N2, ct), lambda k, q, c: (q, 0, 0, k, 0, c))
    kern = functools.partial(_fft2_mix_kernel, nblocks=p)
    return pl.pallas_call(
        kern,
        grid=(FFT_N1 // kb, g, d_b // ct),
        in_specs=[gspec, gspec, gspec, gspec, xspec,
                  pl.BlockSpec((nd, 2, kb, FFT_N2, ct), lambda k, q, c: (0, 0, k, 0, c))],
        out_specs=xspec,
        out_shape=jax.ShapeDtypeStruct(x1.shape, SPECTRA_DTYPE),
        compiler_params=_cparams(("parallel", "parallel", "parallel")),
        name="fft2_mix",
    )(gfwd_hl[0], gfwd_hl[1], ginv_hl[0], ginv_hl[1], x1, kspec)


def _hyena_long_conv(g3, w1, b1, w2, b2, w3, b3, freq, w4):
    bsz, seq_len, d_b = g3.shape
    nblocks = seq_len // CONV_BLOCK
    half = FFT_N1 // 2
    lhs_data, lhs_real, lhs_inv, gfwd, ginv = _dft_tables()
    gfwd_hl = _split(jnp.asarray(gfwd))
    ginv_hl = _split(jnp.asarray(ginv))
    kext = _hyena_filter_blocks(seq_len, w1, b1, w2, b2, w3, b3, freq, w4)
    nd = 2 * nblocks - 1
    k1f = _fft1(lhs_real, kext[None], lambda q, j: (0, q + 1), lambda q, j: (0, q), nd, 1, d_b,
                natural=False, zero_lanes=d_b)
    kspec = _fft2_filter(gfwd_hl, k1f.reshape(nd, 2, FFT_N1, FFT_N2, d_b))
    g5 = g3.reshape(bsz, nblocks, half, FFT_N2, d_b)
    x1 = _fft1(lhs_data, g5, lambda q, j: (2 * q, j), lambda q, j: (2 * q + 1, j), bsz // 2, nblocks, d_b,
               natural=True)
    wmix = _fft2_mix(gfwd_hl, ginv_hl, x1, kspec)
    y6 = _fft3(lhs_inv, wmix)
    return y6.reshape(bsz, seq_len, d_b)


def _postmix_kernel(x_ref, yan_ref, x0_ref, g_ref, yc_ref, bias_ref, mgb_ref, wo_ref, n2_ref, x1_ref, xn_ref):
    d_a = yan_ref.shape[1]
    gg = g_ref[...]
    yb = x0_ref[...] * (yc_ref[...] + gg * bias_ref[...])
    ybn = _rms(yb, mgb_ref[...]).astype(BF16)
    y = jnp.dot(yan_ref[...], wo_ref[0:d_a, :], preferred_element_type=F32)
    y = y + jnp.dot(ybn, wo_ref[d_a:, :], preferred_element_type=F32)
    x1 = x_ref[...] + y
    x1_ref[...] = x1
    xn_ref[...] = _rms(x1, n2_ref[...]).astype(xn_ref.dtype)


def _postmix(x, yan, x0, g, yconv, bias, mix_g_b, w_out, norm2, tm=512):
    t, d = x.shape
    d_a = yan.shape[1]
    d_b = x0.shape[1]
    assert t % tm == 0, t
    row = lambda w: pl.BlockSpec((tm, w), lambda i: (i, 0))
    const = lambda *shape: pl.BlockSpec(shape, lambda i: (0,) * len(shape), pipeline_mode=pl.Buffered(1))
    return pl.pallas_call(
        _postmix_kernel,
        grid=(t // tm,),
        in_specs=[row(d), row(d_a), row(d_b), row(d_b), row(d_b), const(1, d_b), const(1, d_b),
                  const(d_a + d_b, d), const(1, d)],
        out_specs=[row(d), row(d)],
        out_shape=[jax.ShapeDtypeStruct((t, d), F32), jax.ShapeDtypeStruct((t, d), BF16)],
        compiler_params=_cparams(("parallel",)),
        name="postmix",
    )(x, yan, x0, g, yconv, bias, mix_g_b, w_out, norm2)


def _staircase():
    return [(i, j) for i in range(PEER_TOPK) for j in range(PEER_TOPK) if (i + 1) * (j + 1) <= PEER_TOPK]


ARGMAX_LANES = 2


def _stream_argmax(val_ref, tag_ref, nrows, prev, slab):
    neg = np.float32(-np.inf)
    best_v = [jnp.full(slab, neg, F32) for _ in range(ARGMAX_LANES)]
    best_i = [jnp.zeros(slab, I32) for _ in range(ARGMAX_LANES)]
    best_t = [jnp.zeros(slab, I32) for _ in range(ARGMAX_LANES)]
    for n in range(nrows):
        v = jnp.where(prev == n, neg, val_ref[n])
        val_ref[n] = v
        k = n % ARGMAX_LANES
        better = v > best_v[k]
        best_v[k] = jnp.maximum(v, best_v[k])
        best_i[k] = jnp.where(better, n, best_i[k])
        if tag_ref is not None:
            best_t[k] = jnp.where(better, tag_ref[n], best_t[k])
    width = ARGMAX_LANES
    while width > 1:
        width //= 2
        for k in range(width):
            va, vb = best_v[k], best_v[k + width]
            ia, ib = best_i[k], best_i[k + width]
            take_b = jnp.logical_or(vb > va, jnp.logical_and(vb == va, ib < ia))
            best_v[k] = jnp.where(take_b, vb, va)
            best_i[k] = jnp.where(take_b, ib, ia)
            best_t[k] = jnp.where(take_b, best_t[k + width], best_t[k])
    return best_v[0], best_i[0], best_t[0]


def _scores_kernel(xn_ref, wqt_ref, kbd_ref, s_ref, *, heads):
    tm = xn_ref.shape[0]
    hk = heads * HALF_KEY
    q_t = lax.dot_general(wqt_ref[...], xn_ref[...], (((1,), (1,)), ((), ())),
                          preferred_element_type=F32).astype(BF16)
    for half in range(2):
        s_t = jnp.dot(kbd_ref[half], q_t[half * hk:(half + 1) * hk], preferred_element_type=F32)
        s_ref[half] = s_t.reshape(N_KEYS, heads, tm)


def _peer_scores(xn, wqt, kbd, heads, tm=512):
    t, d = xn.shape
    assert t % tm == 0 and heads == V7X_SUBLANES, (t, heads)
    const = lambda *shape: pl.BlockSpec(shape, lambda i: (0,) * len(shape), pipeline_mode=pl.Buffered(1))
    kern = functools.partial(_scores_kernel, heads=heads)
    return pl.pallas_call(
        kern,
        grid=(t // tm,),
        in_specs=[pl.BlockSpec((tm, d), lambda i: (i, 0)), const(*wqt.shape), const(*kbd.shape)],
        out_specs=pl.BlockSpec((2, N_KEYS, heads, tm), lambda i: (0, 0, 0, i)),
        out_shape=jax.ShapeDtypeStruct((2, N_KEYS, heads, t), F32),
        compiler_params=_cparams(("parallel",)),
        name="peer_scores",
    )(xn, wqt, kbd)


SELECT_BLOCK = 128


def _peer_a_select_kernel(xn_ref, dlo_ref, dhi_ref, s_ref, a_ref, e_ref, gt_ref, s0_s, s1_s, c_s, ce_s, b_s, es_s,
                          *, heads):
    xn = xn_ref[...]
    a_lo = jnp.dot(xn, dlo_ref[...], preferred_element_type=F32)
    a_hi = jnp.dot(xn, dhi_ref[...], preferred_element_type=F32)
    a_ref[...] = pltpu.pack_elementwise([a_lo, a_hi], packed_dtype=BF16)

    slab = (heads, s_ref.shape[-1])
    no_pick = jnp.full(slab, -1, I32)
    half_s = (s0_s, s1_s)
    vals, idxs, prevs = ([], []), ([], []), [no_pick, no_pick]
    for half in range(2):
        half_s[half][...] = s_ref[half]
    for r in range(PEER_TOPK):
        for half in range(2):
            m, prevs[half], _ = _stream_argmax(half_s[half], None, N_KEYS, prevs[half], slab)
            vals[half].append(m)
            idxs[half].append(prevs[half])
    cands = _staircase()
    for p, (i, j) in enumerate(cands):
        c_s[p] = vals[0][i] + vals[1][j]
        ce_s[p] = idxs[0][i] * N_KEYS + idxs[1][j]
    prev = no_pick
    for r in range(PEER_TOPK):
        m, prev, e = _stream_argmax(c_s, ce_s, len(cands), prev, slab)
        es_s[r] = e
        b_s[r] = m
    best = b_s[...]
    ex = jnp.exp(best - jnp.max(best, axis=0, keepdims=True))
    gate = ex / jnp.sum(ex, axis=0, keepdims=True)
    gt_ref[...] = gate.reshape(PEER_TOPK * heads, slab[1]).T
    e_ref[...] = es_s[...].reshape(PEER_TOPK * heads, slab[1]).T


def _peer_a_select(xn, down_t, scores, heads, tm=1024, tn=1024):
    t, d = xn.shape
    n = down_t.shape[1]
    tb = SELECT_BLOCK
    nsel = PEER_TOPK * heads
    ncand = len(_staircase())
    ni = t // tm
    nj = n // (2 * tn)
    assert t % tm == 0 and n % (2 * tn) == 0 and heads == V7X_SUBLANES, (t, n, heads)
    assert nj * ni * tb == t, "one selection block per grid step must cover all tokens"
    blk = lambda j, i: j * ni + i
    kern = functools.partial(_peer_a_select_kernel, heads=heads)
    return pl.pallas_call(
        kern,
        grid=(nj, ni),
        in_specs=[pl.BlockSpec((tm, d), lambda j, i: (i, 0)),
                  pl.BlockSpec((d, tn), lambda j, i: (0, j)), pl.BlockSpec((d, tn), lambda j, i: (0, j + nj)),
                  pl.BlockSpec((2, N_KEYS, heads, tb), lambda j, i: (0, 0, 0, blk(j, i)))],
        out_specs=[pl.BlockSpec((tm, tn), lambda j, i: (i, j)),
                   pl.BlockSpec((tb, nsel), lambda j, i: (blk(j, i), 0)),
                   pl.BlockSpec((tb, nsel), lambda j, i: (blk(j, i), 0))],
        out_shape=[jax.ShapeDtypeStruct((t, n // 2), U32), jax.ShapeDtypeStruct((t, nsel), I32),
                   jax.ShapeDtypeStruct((t, nsel), F32)],
        scratch_shapes=[
            pltpu.VMEM((N_KEYS, heads, tb), F32),
            pltpu.VMEM((N_KEYS, heads, tb), F32),
            pltpu.VMEM((ncand, heads, tb), F32),
            pltpu.VMEM((ncand, heads, tb), I32),
            pltpu.VMEM((PEER_TOPK, heads, tb), F32),
            pltpu.VMEM((PEER_TOPK, heads, tb), I32),
        ],
        compiler_params=_cparams(("parallel", "parallel")),
        name="peer_a_select",
    )(xn, down_t, down_t, scores)


PEER_B_GROUP = 16


def _peer_b_pick(a_ref, e_ref, gt_ref, rows):
    e = e_ref[rows, :]
    ai = lax.shift_right_logical(e, KEY_SHIFT)
    bi = e & (N_KEYS - 1)
    a_word = ai & (N_KEYS // 2 - 1)
    accs = [jnp.zeros(e.shape, U32) for _ in range(4)]
    for a in range(N_KEYS // 2):
        blk = a_ref[rows, a * N_KEYS:(a + 1) * N_KEYS]
        accs[a % 4] = jnp.where(a_word == a, jnp.take_along_axis(blk, bi, axis=1), accs[a % 4])
    word = (accs[0] | accs[1]) | (accs[2] | accs[3])
    lo = pltpu.unpack_elementwise(word, index=0, packed_dtype=BF16, unpacked_dtype=F32)
    hi = pltpu.unpack_elementwise(word, index=1, packed_dtype=BF16, unpacked_dtype=F32)
    picked = jnp.where(ai >= N_KEYS // 2, hi, lo)
    return gt_ref[rows, :] * _gelu(picked), e


def _peer_b_scatter(m_ref, rows, w, e):
    sub = lax.broadcasted_iota(I32, (e.shape[0], N_KEYS, e.shape[1]), 1)
    ai = lax.shift_right_logical(e, KEY_SHIFT)
    bi = e & (N_KEYS - 1)
    w1t = jnp.where(sub == ai[:, None, :], w[:, None, :], 0.0).astype(BF16)
    e2t = jnp.where(sub == bi[:, None, :], 1.0, 0.0).astype(BF16)
    m3 = jnp.einsum("cas,cbs->cab", w1t, e2t, preferred_element_type=F32)
    mt = jnp.swapaxes(m3, 0, 1).astype(m_ref.dtype)
    for a in range(N_KEYS):
        m_ref[rows, a * N_KEYS:(a + 1) * N_KEYS] = mt[a]


def _peer_b_kernel(a_ref, e_ref, gt_ref, m_ref):
    grp = PEER_B_GROUP
    ngrp = a_ref.shape[0] // grp

    def rows_of(gi):
        return pl.ds(pl.multiple_of(gi * grp, grp), grp)

    def step(gi, carry):
        nxt = _peer_b_pick(a_ref, e_ref, gt_ref, rows_of(gi))
        _peer_b_scatter(m_ref, rows_of(gi - 1), *carry)
        return nxt

    last = lax.fori_loop(1, ngrp, step, _peer_b_pick(a_ref, e_ref, gt_ref, rows_of(0)))
    _peer_b_scatter(m_ref, rows_of(ngrp - 1), *last)


def _peer_b(a2, eidx, gate, tc=256):
    t = a2.shape[0]
    n = 2 * a2.shape[1]
    nsel = eidx.shape[1]
    assert t % tc == 0 and tc % PEER_B_GROUP == 0 and n == N_KEYS * N_KEYS and nsel == N_KEYS, (t, n, nsel)
    return pl.pallas_call(
        _peer_b_kernel,
        grid=(t // tc,),
        in_specs=[pl.BlockSpec((tc, n // 2), lambda i: (i, 0)), pl.BlockSpec((tc, nsel), lambda i: (i, 0)),
                  pl.BlockSpec((tc, nsel), lambda i: (i, 0))],
        out_specs=pl.BlockSpec((tc, n), lambda i: (i, 0)),
        out_shape=jax.ShapeDtypeStruct((t, n), BF16),
        compiler_params=_cparams(("parallel",)),
        name="peer_b",
    )(a2, eidx, gate)


def _peer_c_kernel(m_ref, up_ref, x1_ref, fn_ref, o_ref, acc_s):
    k = pl.program_id(1)

    @pl.when(k == 0)
    def _():
        acc_s[...] = jnp.zeros_like(acc_s)

    acc_s[...] += jnp.dot(m_ref[...], up_ref[...], preferred_element_type=F32)

    @pl.when(k == pl.num_programs(1) - 1)
    def _():
        o_ref[...] = _rms(x1_ref[...] + acc_s[...], fn_ref[...])


def _peer_c(m, up, x1, final_norm, tm=1024, tk=1024):
    t, n = m.shape
    d = up.shape[1]
    assert t % tm == 0 and n % tk == 0, (t, n)
    return pl.pallas_call(
        _peer_c_kernel,
        grid=(t // tm, n // tk),
        in_specs=[pl.BlockSpec((tm, tk), lambda i, k: (i, k)), pl.BlockSpec((tk, d), lambda i, k: (k, 0)),
                  pl.BlockSpec((tm, d), lambda i, k: (i, 0), pipeline_mode=pl.Buffered(1)),
                  pl.BlockSpec((1, d), lambda i, k: (0, 0))],
        out_specs=pl.BlockSpec((tm, d), lambda i, k: (i, 0)),
        out_shape=jax.ShapeDtypeStruct((t, d), F32),
        scratch_shapes=[pltpu.VMEM((tm, d), F32)],
        compiler_params=_cparams(("parallel", "arbitrary")),
        name="peer_c",
    )(m, up, x1, final_norm)


def _prep_weights(norm1, w_in, a_ln_g, a_ln_b, a_ws, a_bs, b_conv_w, b_conv_b, b_bias, mix_norm, w_out, norm2,
                  peer_wq, peer_k1, peer_k2, peer_down, peer_up, final_norm):
    d = w_in.shape[0]
    d_a = a_ln_g.shape[0]
    d_b = b_bias.shape[0]
    heads, n_keys, half_key = peer_k1.shape
    row = lambda v: v.reshape(1, -1).astype(F32)
    w = dict(
        norm1=row(norm1), norm2=row(norm2), final_norm=row(final_norm),
        w_u=w_in[:, :d_a].astype(BF16), w_v=w_in[:, d_a:2 * d_a].astype(BF16),
        w_b=w_in[:, 2 * d_a:].reshape(d, 3, d_b).transpose(1, 0, 2).astype(BF16),
        ln_g=row(a_ln_g), ln_b=row(a_ln_b),
        ws=a_ws.astype(BF16),
        bsb=jnp.broadcast_to(a_bs[:, :, None], a_bs.shape + (A_HEAD_DIM,)).astype(F32),
        conv_w=b_conv_w.reshape(3, 3, d_b).transpose(1, 0, 2).astype(F32),
        conv_b=b_conv_b.reshape(3, d_b).astype(F32),
        bias=row(b_bias), mix_g_a=row(mix_norm[:d_a]), mix_g_b=row(mix_norm[d_a:]),
        w_out=w_out.astype(BF16),
        down_t=peer_down.astype(BF16).T, up=peer_up.astype(BF16),
    )
    wq_t = peer_wq.T.reshape(heads, 2, half_key, d).transpose(1, 0, 2, 3).reshape(2 * heads * half_key, d)
    w["wq_t"] = wq_t.astype(BF16)
    eye = jnp.eye(heads, dtype=F32)
    kbd = [jnp.einsum("hnd,hg->nhgd", k, eye).reshape(n_keys * heads, heads * half_key) for k in (peer_k1, peer_k2)]
    w["kbd"] = jnp.stack(kbd).astype(BF16)
    w["heads"] = heads
    return w


def _trunk(x3, w, hf):
    bsz, seq_len, d = x3.shape
    t = bsz * seq_len
    x = x3.reshape(t, d)
    yan, x0, g = _inproj(x, seq_len, w["norm1"], w["w_u"], w["w_v"], w["w_b"], w["ln_g"], w["ln_b"],
                         w["ws"], w["bsb"], w["conv_w"], w["conv_b"], w["mix_g_a"])
    d_b = x0.shape[1]
    yconv = _hyena_long_conv(g.reshape(bsz, seq_len, d_b), *hf).reshape(t, d_b)
    x1, xn = _postmix(x, yan, x0, g, yconv, w["bias"], w["mix_g_b"], w["w_out"], w["norm2"])
    scores = _peer_scores(xn, w["wq_t"], w["kbd"], w["heads"])
    a2, eidx, gate = _peer_a_select(xn, w["down_t"], scores, w["heads"])
    m = _peer_b(a2, eidx, gate)
    out = _peer_c(m, w["up"], x1, w["final_norm"])
    return out.reshape(bsz, seq_len, d)


def kernel(x_prompt, x_sample, norm1, w_in, a_ln_g, a_ln_b, a_ws, a_bs, b_conv_w, b_conv_b, hf_w1, hf_b1, hf_w2, hf_b2, hf_w3, hf_b3, hf_freq, hf_w4, b_bias, mix_norm, w_out, norm2, peer_wq, peer_k1, peer_k2, peer_down, peer_up, final_norm):
    assert norm1.shape[0] == 1, "single-layer trunk"
    w = _prep_weights(norm1[0], w_in[0], a_ln_g[0], a_ln_b[0], a_ws[0], a_bs[0], b_conv_w[0], b_conv_b[0],
                      b_bias[0], mix_norm[0], w_out[0], norm2[0], peer_wq[0], peer_k1[0], peer_k2[0],
                      peer_down[0], peer_up[0], final_norm)
    hf = (hf_w1[0], hf_b1[0].reshape(1, -1), hf_w2[0], hf_b2[0].reshape(1, -1), hf_w3[0], hf_b3[0].reshape(1, -1),
          hf_freq[0], hf_w4[0])
    return (_trunk(x_prompt, w, hf), _trunk(x_sample, w, hf))
```

```python
import functools
import math

import numpy as np
import jax
import jax.numpy as jnp
from jax import lax
from jax.experimental import pallas as pl
from jax.experimental.pallas import tpu as pltpu

F32 = jnp.float32
BF16 = jnp.bfloat16
I32 = jnp.int32
U32 = jnp.uint32

EPS = 1e-6
V7X_LANES = 128
V7X_SUBLANES = 8
V7X_VMEM_BYTES = 64 * 1024 * 1024
VMEM_LIMIT = V7X_VMEM_BYTES - 8 * 1024 * 1024

CHUNK = 128
A_HEAD_DIM = 128
EMB_DIM = 33
BANDS = (EMB_DIM - 1) // 2
DECAY_TARGET = 1e-2
FAST_DECAY_PCT = 0.3
SLOW_DECAY_PCT = 1.5
FFT_N1 = 64
FFT_N2 = 128
FFT_N = FFT_N1 * FFT_N2
CONV_BLOCK = FFT_N // 2
N_KEYS = 128
KEY_SHIFT = 7
PEER_TOPK = 16
HALF_KEY = 128


def _cparams(sem):
    return pltpu.CompilerParams(dimension_semantics=sem, vmem_limit_bytes=VMEM_LIMIT)


def _rms(xf, g):
    return xf * lax.rsqrt(jnp.mean(xf * xf, axis=-1, keepdims=True) + EPS) * g


def _gelu(x):
    return 0.5 * x * (1.0 + lax.erf(x * np.float32(math.sqrt(0.5))))


def _inproj_kernel(xp_ref, x_ref, xn_ref, n1_ref, wu_ref, wv_ref, wb_ref, lng_ref, lnb_ref,
                   ws_ref, bsb_ref, cw_ref, cb_ref, mga_ref,
                   yan_ref, x0_ref, g_ref, ya_s, zs_s, *, tiles_per_seq, cblk):
    tm = x_ref.shape[0]
    d_a = wu_ref.shape[1]
    d_b = wb_ref.shape[2]
    i = pl.program_id(0)
    not_first = (i % tiles_per_seq != 0).astype(F32)
    not_last = (i % tiles_per_seq != tiles_per_seq - 1).astype(F32)

    xcat = jnp.concatenate([xp_ref[...], x_ref[...], xn_ref[...]], axis=0)
    hcat = _rms(xcat, n1_ref[...])
    h_all = hcat.astype(BF16)
    h = hcat[V7X_SUBLANES:V7X_SUBLANES + tm].astype(BF16)

    u = _gelu(jnp.dot(h, wu_ref[...], preferred_element_type=F32))
    v = _gelu(jnp.dot(h, wv_ref[...], preferred_element_type=F32))
    mu = jnp.mean(v, axis=-1, keepdims=True)
    vc = v - mu
    var = jnp.mean(vc * vc, axis=-1, keepdims=True)
    vb = (vc * lax.rsqrt(var + EPS) * lng_ref[...] + lnb_ref[...]).astype(BF16)
    chunks = [slice(c * CHUNK, (c + 1) * CHUNK) for c in range(tm // CHUNK)]
    for hd in range(d_a // A_HEAD_DIM):
        cols = slice(hd * A_HEAD_DIM, (hd + 1) * A_HEAD_DIM)
        v_head = jnp.concatenate([vb[rows, cols] for rows in chunks], axis=1)
        mixed = jnp.dot(ws_ref[hd], v_head, preferred_element_type=F32)
        for c, rows in enumerate(chunks):
            ya_s[rows, cols] = u[rows, cols] * (mixed[:, c * A_HEAD_DIM:(c + 1) * A_HEAD_DIM] + bsb_ref[hd])
    yan_ref[...] = _rms(ya_s[...], mga_ref[...]).astype(yan_ref.dtype)

    halo = V7X_SUBLANES
    for cb in range(d_b // cblk):
        cols = slice(cb * cblk, (cb + 1) * cblk)
        parts = []
        for p in range(3):
            z = jnp.dot(h_all, wb_ref[p, :, cols], preferred_element_type=F32)
            zs_s[...] = z
            zs_s[0:halo, :] = z[0:halo] * not_first
            zs_s[tm + halo:tm + 2 * halo, :] = z[tm + halo:tm + 2 * halo] * not_last
            w = cw_ref[p]
            zc = (cb_ref[p:p + 1, cols]
                  + zs_s[halo - 1:halo - 1 + tm, :] * w[0:1, cols]
                  + zs_s[halo:halo + tm, :] * w[1:2, cols]
                  + zs_s[halo + 1:halo + 1 + tm, :] * w[2:3, cols])
            parts.append(zc)
        x0_ref[:, cols] = parts[0]
        g_ref[:, cols] = parts[1] * parts[2]


def _inproj(x, seq_len, norm1, w_u, w_v, w_b, ln_g, ln_b, ws, bsb, conv_w, conv_b, mix_g_a, tm=512, cblk=256):
    t, d = x.shape
    d_a = w_u.shape[1]
    d_b = w_b.shape[2]
    assert seq_len % tm == 0 and t % seq_len == 0 and tm % CHUNK == 0 and d_b % cblk == 0, (t, seq_len)
    nblk = tm // V7X_SUBLANES
    last8 = t // V7X_SUBLANES - 1
    const = lambda *shape: pl.BlockSpec(shape, lambda i: (0,) * len(shape), pipeline_mode=pl.Buffered(1))
    kern = functools.partial(_inproj_kernel, tiles_per_seq=seq_len // tm, cblk=cblk)
    return pl.pallas_call(
        kern,
        grid=(t // tm,),
        in_specs=[
            pl.BlockSpec((V7X_SUBLANES, d), lambda i: (jnp.maximum(i * nblk - 1, 0), 0)),
            pl.BlockSpec((tm, d), lambda i: (i, 0)),
            pl.BlockSpec((V7X_SUBLANES, d), lambda i: (jnp.minimum((i + 1) * nblk, last8), 0)),
            const(1, d), const(d, d_a), const(d, d_a), const(3, d, d_b), const(1, d_a), const(1, d_a),
            const(*ws.shape), const(*bsb.shape), const(3, 3, d_b), const(3, d_b), const(1, d_a),
        ],
        out_specs=[
            pl.BlockSpec((tm, d_a), lambda i: (i, 0)),
            pl.BlockSpec((tm, d_b), lambda i: (i, 0)),
            pl.BlockSpec((tm, d_b), lambda i: (i, 0)),
        ],
        out_shape=[
            jax.ShapeDtypeStruct((t, d_a), BF16),
            jax.ShapeDtypeStruct((t, d_b), F32),
            jax.ShapeDtypeStruct((t, d_b), F32),
        ],
        scratch_shapes=[pltpu.VMEM((tm, d_a), F32), pltpu.VMEM((tm + 2 * V7X_SUBLANES, cblk), F32)],
        compiler_params=_cparams(("parallel",)),
        name="inproj",
    )(x, x, x, norm1, w_u, w_v, w_b, ln_g, ln_b, ws, bsb, conv_w, conv_b, mix_g_a)


def _filt_kernel(fr_ref, dl_ref, w1_ref, b1_ref, w2_ref, b2_ref, w3_ref, b3_ref, fq_ref, w4_ref, o_ref,
                 *, seq_len, nblocks, n2_per_step):
    half = FFT_N1 // 2
    tr = half * n2_per_step
    e = pl.program_id(0) - nblocks

    def lags(shape, axis):
        r = lax.broadcasted_iota(I32, shape, axis)
        n2 = pl.program_id(1) * n2_per_step + r // half
        return e * CONV_BLOCK + (r % half) * FFT_N2 + n2

    inv_len = np.float32(max(seq_len - 1, 1))
    pos = jnp.abs(lags((1, tr), 1)).astype(F32)
    t = pos / inv_len
    fw = (np.float32(2.0 * math.pi / seq_len) * pos) * fr_ref[...]
    hi = lax.Precision.HIGHEST
    w1 = w1_ref[...]
    z1 = (w1[:, 0:1] * t
          + jnp.dot(w1[:, 1:1 + BANDS], jnp.cos(fw), precision=hi, preferred_element_type=F32)
          + jnp.dot(w1[:, 1 + BANDS:], -jnp.sin(fw), precision=hi, preferred_element_type=F32))
    fq = fq_ref[...]
    h = jnp.sin(fq[:, 0:1] * (z1 + b1_ref[...]))
    h = jnp.sin(fq[:, 1:2] * (jnp.dot(w2_ref[...], h, precision=hi, preferred_element_type=F32) + b2_ref[...]))
    h = jnp.sin(fq[:, 2:3] * (jnp.dot(w3_ref[...], h, precision=hi, preferred_element_type=F32) + b3_ref[...]))
    h4 = jnp.dot(h.T, w4_ref[...], precision=hi, preferred_element_type=F32)
    lag_col = lags((tr, 1), 0)
    t_col = jnp.abs(lag_col).astype(F32) / inv_len
    window = jnp.exp(-t_col * dl_ref[...])
    res = jnp.where(jnp.abs(lag_col) <= seq_len - 1, h4 * window, 0.0)
    d_b = res.shape[1]
    for i in range(n2_per_step):
        o_ref[:, i * d_b:(i + 1) * d_b] = res[i * half:(i + 1) * half]


def _hyena_filter_blocks(seq_len, w1, b1, w2, b2, w3, b3, freq, w4, n2_per_step=16):
    nblocks = seq_len // CONV_BLOCK
    width = w1.shape[1]
    d_b = w4.shape[1] // 2
    fr = jnp.asarray(np.linspace(1e-4, BANDS - 1, BANDS, dtype=np.float32)[:, None])
    min_decay = math.log(DECAY_TARGET) / SLOW_DECAY_PCT
    max_decay = math.log(DECAY_TARGET) / FAST_DECAY_PCT
    deltas = jnp.asarray(np.abs(np.linspace(min_decay, max_decay, d_b, dtype=np.float32))[None, :])
    w4h = w4.reshape(width, 2, d_b).transpose(1, 0, 2)
    const = lambda *shape: pl.BlockSpec(shape, lambda a, b: (0,) * len(shape))
    half = FFT_N1 // 2
    kern = functools.partial(_filt_kernel, seq_len=seq_len, nblocks=nblocks, n2_per_step=n2_per_step)
    return pl.pallas_call(
        kern,
        grid=(2 * nblocks, FFT_N2 // n2_per_step),
        in_specs=[
            const(BANDS, 1), const(1, d_b), const(width, EMB_DIM), const(width, 1), const(width, width),
            const(width, 1), const(width, width), const(width, 1), const(width, 3),
            pl.BlockSpec((None, width, d_b), lambda ei, r: (jnp.where(ei >= nblocks, 0, 1), 0, 0)),
        ],
        out_specs=pl.BlockSpec((None, half, n2_per_step * d_b), lambda ei, r: (ei, 0, r)),
        out_shape=jax.ShapeDtypeStruct((2 * nblocks, half, FFT_N2 * d_b), F32),
        compiler_params=_cparams(("parallel", "parallel")),
        name="hyena_filter",
    )(fr, deltas, w1.T, b1.reshape(width, 1), w2.T, b2.reshape(width, 1), w3.T, b3.reshape(width, 1), freq.T, w4h)


def _dft_tables():
    n1 = np.arange(FFT_N1)
    f64 = np.exp(-2j * np.pi * np.outer(n1, n1) / FFT_N1)
    half = FFT_N1 // 2
    fr, fi = f64.real, f64.imag
    lhs_data = np.block([[fr[:, :half], -fi[:, :half]], [fi[:, :half], fr[:, :half]]])
    lhs_real = np.concatenate([fr, fi], axis=0)
    gr, gi = fr[:half, :] / FFT_N, -fi[:half, :] / FFT_N
    lhs_inv = np.block([[gr, -gi], [gi, gr]])
    k1 = np.arange(FFT_N1)[:, None, None]
    k2 = np.arange(FFT_N2)[None, :, None]
    n2 = np.arange(FFT_N2)[None, None, :]
    g = np.exp(-2j * np.pi * (n2 * (k1 + FFT_N1 * k2) % FFT_N) / FFT_N)
    gfwd = np.concatenate([np.concatenate([g.real, -g.imag], axis=2),
                           np.concatenate([g.imag, g.real], axis=2)], axis=1)
    ht = np.conj(np.transpose(g, (0, 2, 1)))
    ginv = np.concatenate([np.concatenate([ht.real, -ht.imag], axis=2),
                           np.concatenate([ht.imag, ht.real], axis=2)], axis=1)
    as32 = lambda a: np.asarray(a, dtype=np.float32)
    return as32(lhs_data), as32(lhs_real), as32(lhs_inv), as32(gfwd), as32(ginv)


def _split(x):
    hi = x.astype(BF16)
    lo = (x - hi.astype(F32)).astype(BF16)
    return hi, lo


def _dot3(a_hi, a_lo, b):
    if b.dtype == BF16:
        return jnp.dot(a_hi, b, preferred_element_type=F32) + jnp.dot(a_lo, b, preferred_element_type=F32)
    b_hi, b_lo = _split(b)
    acc = jnp.dot(a_hi, b_hi, preferred_element_type=F32)
    acc = acc + jnp.dot(a_hi, b_lo, preferred_element_type=F32)
    acc = acc + jnp.dot(a_lo, b_hi, preferred_element_type=F32)
    return acc


SPECTRA_DTYPE = BF16
FFT_TN2 = 16


def _rows_to_lanes(x):
    xt = jnp.swapaxes(x, 0, 1)
    return jnp.concatenate([xt[i] for i in range(xt.shape[0])], axis=1)


def _lanes_to_rows(x, s):
    c = x.shape[1] // s
    return jnp.swapaxes(jnp.stack([x[:, i * c:(i + 1) * c] for i in range(s)], axis=0), 0, 1)


def _fft1_kernel(lh_ref, ll_ref, a_ref, b_ref, o_ref, *, zero_lanes, natural):
    if natural:
        a = _rows_to_lanes(a_ref[...])
        b = _rows_to_lanes(b_ref[...])
    else:
        a = a_ref[...]
        b = b_ref[...]
    if zero_lanes:
        row = lax.broadcasted_iota(I32, b.shape, 0)
        lane = lax.broadcasted_iota(I32, b.shape, 1) + pl.program_id(2) * b.shape[1]
        b = jnp.where(jnp.logical_and(row == 0, lane < zero_lanes), 0.0, b)
    res = _dot3(lh_ref[...], ll_ref[...], jnp.concatenate([a, b], axis=0))
    o_ref[...] = _lanes_to_rows(res, FFT_TN2).reshape(o_ref.shape).astype(o_ref.dtype)


def _fft1(lhs, x, a_of, b_of, g, p, d_b, natural, zero_lanes=0):
    half = FFT_N1 // 2
    lh, ll = _split(jnp.asarray(lhs))
    kern = functools.partial(_fft1_kernel, zero_lanes=zero_lanes, natural=natural)
    if natural:
        blk = lambda of: pl.BlockSpec((None, None, half, FFT_TN2, d_b), lambda q, j, t: of(q, j) + (0, t, 0))
    else:
        blk = lambda of: pl.BlockSpec((None, None, half, FFT_TN2 * d_b), lambda q, j, t: of(q, j) + (0, t))
    return pl.pallas_call(
        kern,
        grid=(g, p, FFT_N2 // FFT_TN2),
        in_specs=[
            pl.BlockSpec(lhs.shape, lambda q, j, t: (0, 0)),
            pl.BlockSpec(lhs.shape, lambda q, j, t: (0, 0)),
            blk(a_of), blk(b_of),
        ],
        out_specs=pl.BlockSpec((None, None, 2, FFT_N1, FFT_TN2, d_b), lambda q, j, t: (q, j, 0, 0, t, 0)),
        out_shape=jax.ShapeDtypeStruct((g, p, 2, FFT_N1, FFT_N2, d_b), SPECTRA_DTYPE),
        compiler_params=_cparams(("parallel", "parallel", "parallel")),
        name="fft1",
    )(lh, ll, x, x)


def _fft3_kernel(lh_ref, ll_ref, w_ref, o_ref):
    half = FFT_N1 // 2
    c = w_ref.shape[-1]
    w = _rows_to_lanes(w_ref[...].astype(F32).reshape(2 * FFT_N1, FFT_TN2, c)).astype(w_ref.dtype)
    res = _dot3(lh_ref[...], ll_ref[...], w)
    o_ref[0] = _lanes_to_rows(res[:half], FFT_TN2)
    o_ref[1] = _lanes_to_rows(res[half:], FFT_TN2)


def _fft3(lhs, w6):
    g, p, _, _, _, d_b = w6.shape
    half = FFT_N1 // 2
    lh, ll = _split(jnp.asarray(lhs))
    return pl.pallas_call(
        _fft3_kernel,
        grid=(g, p, FFT_N2 // FFT_TN2),
        in_specs=[
            pl.BlockSpec(lhs.shape, lambda q, j, t: (0, 0)),
            pl.BlockSpec(lhs.shape, lambda q, j, t: (0, 0)),
            pl.BlockSpec((None, None, 2, FFT_N1, FFT_TN2, d_b), lambda q, j, t: (q, j, 0, 0, t, 0)),
        ],
        out_specs=pl.BlockSpec((None, 2, None, half, FFT_TN2, d_b), lambda q, j, t: (q, 0, j, 0, t, 0)),
        out_shape=jax.ShapeDtypeStruct((g, 2, p, half, FFT_N2, d_b), F32),
        compiler_params=_cparams(("parallel", "parallel", "parallel")),
        name="fft3",
    )(lh, ll, w6)


def _fft2_filter_kernel(gh_ref, gl_ref, x_ref, o_ref):
    ct = x_ref.shape[-1]
    for k in range(x_ref.shape[1]):
        z = _dot3(gh_ref[k], gl_ref[k], x_ref[:, k].reshape(2 * FFT_N2, ct))
        o_ref[:, k] = z.reshape(2, FFT_N2, ct).astype(o_ref.dtype)


def _fft2_filter(gfwd_hl, x1f, ct=256, kb=8):
    nd, _, _, _, d_b = x1f.shape
    gh, gl = gfwd_hl
    gspec = pl.BlockSpec((kb, 2 * FFT_N2, 2 * FFT_N2), lambda k, d, c: (k, 0, 0))
    xspec = pl.BlockSpec((None, 2, kb, FFT_N2, ct), lambda k, d, c: (d, 0, k, 0, c))
    return pl.pallas_call(
        _fft2_filter_kernel,
        grid=(FFT_N1 // kb, nd, d_b // ct),
        in_specs=[gspec, gspec, xspec],
        out_specs=xspec,
        out_shape=jax.ShapeDtypeStruct(x1f.shape, SPECTRA_DTYPE),
        compiler_params=_cparams(("parallel", "parallel", "parallel")),
        name="fft2_filter",
    )(gh, gl, x1f)


def _fft2_mix_kernel(gh_ref, gl_ref, ih_ref, il_ref, x_ref, k_ref, o_ref, *, nblocks):
    ct = x_ref.shape[-1]
    for k in range(x_ref.shape[2]):
        zs = []
        for j in range(nblocks):
            z = _dot3(gh_ref[k], gl_ref[k], x_ref[j, :, k].reshape(2 * FFT_N2, ct))
            zs.append((z[:FFT_N2], z[FFT_N2:]))
        for i in range(nblocks):
            yr = jnp.zeros((FFT_N2, ct), F32)
            yi = jnp.zeros((FFT_N2, ct), F32)
            for j in range(nblocks):
                d = i - j + nblocks - 1
                kr, ki = k_ref[d, 0, k].astype(F32), k_ref[d, 1, k].astype(F32)
                zr, zi = zs[j]
                yr = yr + kr * zr - ki * zi
                yi = yi + kr * zi + ki * zr
            w = _dot3(ih_ref[k], il_ref[k], jnp.concatenate([yr, yi], axis=0).astype(SPECTRA_DTYPE))
            o_ref[i, :, k] = w.reshape(2, FFT_N2, ct).astype(o_ref.dtype)


def _fft2_mix(gfwd_hl, ginv_hl, x1, kspec, ct=256):
    g, p, _, _, _, d_b = x1.shape
    nd = kspec.shape[0]
    kb = max(1, 8 // p)
    gspec = pl.BlockSpec((kb, 2 * FFT_N2, 2 * FFT_N2), lambda k, q, c: (k, 0, 0))
    xspec = pl.BlockSpec((None, p, 2, kb, FFT_N2, ct), lambda k, q, c: (q, 0, 0, k, 0, c))
    kern = functools.partial(_fft2_mix_kernel, nblocks=p)
    return pl.pallas_call(
        kern,
        grid=(FFT_N1 // kb, g, d_b // ct),
        in_specs=[gspec, gspec, gspec, gspec, xspec,
                  pl.BlockSpec((nd, 2, kb, FFT_N2, ct), lambda k, q, c: (0, 0, k, 0, c))],
        out_specs=xspec,
        out_shape=jax.ShapeDtypeStruct(x1.shape, SPECTRA_DTYPE),
        compiler_params=_cparams(("parallel", "parallel", "parallel")),
        name="fft2_mix",
    )(gfwd_hl[0], gfwd_hl[1], ginv_hl[0], ginv_hl[1], x1, kspec)


def _hyena_long_conv(g3, w1, b1, w2, b2, w3, b3, freq, w4):
    bsz, seq_len, d_b = g3.shape
    nblocks = seq_len // CONV_BLOCK
    half = FFT_N1 // 2
    lhs_data, lhs_real, lhs_inv, gfwd, ginv = _dft_tables()
    gfwd_hl = _split(jnp.asarray(gfwd))
    ginv_hl = _split(jnp.asarray(ginv))
    kext = _hyena_filter_blocks(seq_len, w1, b1, w2, b2, w3, b3, freq, w4)
    nd = 2 * nblocks - 1
    k1f = _fft1(lhs_real, kext[None], lambda q, j: (0, q + 1), lambda q, j: (0, q), nd, 1, d_b,
                natural=False, zero_lanes=d_b)
    kspec = _fft2_filter(gfwd_hl, k1f.reshape(nd, 2, FFT_N1, FFT_N2, d_b))
    g5 = g3.reshape(bsz, nblocks, half, FFT_N2, d_b)
    x1 = _fft1(lhs_data, g5, lambda q, j: (2 * q, j), lambda q, j: (2 * q + 1, j), bsz // 2, nblocks, d_b,
               natural=True)
    wmix = _fft2_mix(gfwd_hl, ginv_hl, x1, kspec)
    y6 = _fft3(lhs_inv, wmix)
    return y6.reshape(bsz, seq_len, d_b)


def _postmix_kernel(x_ref, yan_ref, x0_ref, g_ref, yc_ref, bias_ref, mgb_ref, wo_ref, n2_ref, x1_ref, xn_ref):
    d_a = yan_ref.shape[1]
    gg = g_ref[...]
    yb = x0_ref[...] * (yc_ref[...] + gg * bias_ref[...])
    ybn = _rms(yb, mgb_ref[...]).astype(BF16)
    y = jnp.dot(yan_ref[...], wo_ref[0:d_a, :], preferred_element_type=F32)
    y = y + jnp.dot(ybn, wo_ref[d_a:, :], preferred_element_type=F32)
    x1 = x_ref[...] + y
    x1_ref[...] = x1
    xn_ref[...] = _rms(x1, n2_ref[...]).astype(xn_ref.dtype)


def _postmix(x, yan, x0, g, yconv, bias, mix_g_b, w_out, norm2, tm=512):
    t, d = x.shape
    d_a = yan.shape[1]
    d_b = x0.shape[1]
    assert t % tm == 0, t
    row = lambda w: pl.BlockSpec((tm, w), lambda i: (i, 0))
    const = lambda *shape: pl.BlockSpec(shape, lambda i: (0,) * len(shape), pipeline_mode=pl.Buffered(1))
    return pl.pallas_call(
        _postmix_kernel,
        grid=(t // tm,),
        in_specs=[row(d), row(d_a), row(d_b), row(d_b), row(d_b), const(1, d_b), const(1, d_b),
                  const(d_a + d_b, d), const(1, d)],
        out_specs=[row(d), row(d)],
        out_shape=[jax.ShapeDtypeStruct((t, d), F32), jax.ShapeDtypeStruct((t, d), BF16)],
        compiler_params=_cparams(("parallel",)),
        name="postmix",
    )(x, yan, x0, g, yconv, bias, mix_g_b, w_out, norm2)


def _staircase():
    return [(i, j) for i in range(PEER_TOPK) for j in range(PEER_TOPK) if (i + 1) * (j + 1) <= PEER_TOPK]


ARGMAX_LANES = 2


def _stream_argmax(val_ref, tag_ref, nrows, prev, slab):
    neg = np.float32(-np.inf)
    best_v = [jnp.full(slab, neg, F32) for _ in range(ARGMAX_LANES)]
    best_i = [jnp.zeros(slab, I32) for _ in range(ARGMAX_LANES)]
    best_t = [jnp.zeros(slab, I32) for _ in range(ARGMAX_LANES)]
    for n in range(nrows):
        v = jnp.where(prev == n, neg, val_ref[n])
        val_ref[n] = v
        k = n % ARGMAX_LANES
        better = v > best_v[k]
        best_v[k] = jnp.maximum(v, best_v[k])
        best_i[k] = jnp.where(better, n, best_i[k])
        if tag_ref is not None:
            best_t[k] = jnp.where(better, tag_ref[n], best_t[k])
    width = ARGMAX_LANES
    while width > 1:
        width //= 2
        for k in range(width):
            va, vb = best_v[k], best_v[k + width]
            ia, ib = best_i[k], best_i[k + width]
            take_b = jnp.logical_or(vb > va, jnp.logical_and(vb == va, ib < ia))
            best_v[k] = jnp.where(take_b, vb, va)
            best_i[k] = jnp.where(take_b, ib, ia)
            best_t[k] = jnp.where(take_b, best_t[k + width], best_t[k])
    return best_v[0], best_i[0], best_t[0]


def _scores_kernel(xn_ref, wqt_ref, kbd_ref, s_ref, *, heads):
    tm = xn_ref.shape[0]
    hk = heads * HALF_KEY
    q_t = lax.dot_general(wqt_ref[...], xn_ref[...], (((1,), (1,)), ((), ())),
                          preferred_element_type=F32).astype(BF16)
    for half in range(2):
        s_t = jnp.dot(kbd_ref[half], q_t[half * hk:(half + 1) * hk], preferred_element_type=F32)
        s_ref[half] = s_t.reshape(N_KEYS, heads, tm)


def _peer_scores(xn, wqt, kbd, heads, tm=512):
    t, d = xn.shape
    assert t % tm == 0 and heads == V7X_SUBLANES, (t, heads)
    const = lambda *shape: pl.BlockSpec(shape, lambda i: (0,) * len(shape), pipeline_mode=pl.Buffered(1))
    kern = functools.partial(_scores_kernel, heads=heads)
    return pl.pallas_call(
        kern,
        grid=(t // tm,),
        in_specs=[pl.BlockSpec((tm, d), lambda i: (i, 0)), const(*wqt.shape), const(*kbd.shape)],
        out_specs=pl.BlockSpec((2, N_KEYS, heads, tm), lambda i: (0, 0, 0, i)),
        out_shape=jax.ShapeDtypeStruct((2, N_KEYS, heads, t), F32),
        compiler_params=_cparams(("parallel",)),
        name="peer_scores",
    )(xn, wqt, kbd)


SELECT_BLOCK = 128


def _peer_a_select_kernel(xn_ref, dlo_ref, dhi_ref, s_ref, a_ref, e_ref, gt_ref, s0_s, s1_s, c_s, ce_s, b_s, es_s,
                          *, heads):
    xn = xn_ref[...]
    a_lo = jnp.dot(xn, dlo_ref[...], preferred_element_type=F32)
    a_hi = jnp.dot(xn, dhi_ref[...], preferred_element_type=F32)
    a_ref[...] = pltpu.pack_elementwise([a_lo, a_hi], packed_dtype=BF16)

    slab = (heads, s_ref.shape[-1])
    no_pick = jnp.full(slab, -1, I32)
    half_s = (s0_s, s1_s)
    vals, idxs, prevs = ([], []), ([], []), [no_pick, no_pick]
    for half in range(2):
        half_s[half][...] = s_ref[half]
    for r in range(PEER_TOPK):
        for half in range(2):
            m, prevs[half], _ = _stream_argmax(half_s[half], None, N_KEYS, prevs[half], slab)
            vals[half].append(m)
            idxs[half].append(prevs[half])
    cands = _staircase()
    for p, (i, j) in enumerate(cands):
        c_s[p] = vals[0][i] + vals[1][j]
        ce_s[p] = idxs[0][i] * N_KEYS + idxs[1][j]
    prev = no_pick
    for r in range(PEER_TOPK):
        m, prev, e = _stream_argmax(c_s, ce_s, len(cands), prev, slab)
        es_s[r] = e
        b_s[r] = m
    best = b_s[...]
    ex = jnp.exp(best - jnp.max(best, axis=0, keepdims=True))
    gate = ex / jnp.sum(ex, axis=0, keepdims=True)
    gt_ref[...] = gate.reshape(PEER_TOPK * heads, slab[1]).T
    e_ref[...] = es_s[...].reshape(PEER_TOPK * heads, slab[1]).T


def _peer_a_select(xn, down_t, scores, heads, tm=1024, tn=1024):
    t, d = xn.shape
    n = down_t.shape[1]
    tb = SELECT_BLOCK
    nsel = PEER_TOPK * heads
    ncand = len(_staircase())
    ni = t // tm
    nj = n // (2 * tn)
    assert t % tm == 0 and n % (2 * tn) == 0 and heads == V7X_SUBLANES, (t, n, heads)
    assert nj * ni * tb == t, "one selection block per grid step must cover all tokens"
    blk = lambda j, i: j * ni + i
    kern = functools.partial(_peer_a_select_kernel, heads=heads)
    return pl.pallas_call(
        kern,
        grid=(nj, ni),
        in_specs=[pl.BlockSpec((tm, d), lambda j, i: (i, 0)),
                  pl.BlockSpec((d, tn), lambda j, i: (0, j)), pl.BlockSpec((d, tn), lambda j, i: (0, j + nj)),
                  pl.BlockSpec((2, N_KEYS, heads, tb), lambda j, i: (0, 0, 0, blk(j, i)))],
        out_specs=[pl.BlockSpec((tm, tn), lambda j, i: (i, j)),
                   pl.BlockSpec((tb, nsel), lambda j, i: (blk(j, i), 0)),
                   pl.BlockSpec((tb, nsel), lambda j, i: (blk(j, i), 0))],
        out_shape=[jax.ShapeDtypeStruct((t, n // 2), U32), jax.ShapeDtypeStruct((t, nsel), I32),
                   jax.ShapeDtypeStruct((t, nsel), F32)],
        scratch_shapes=[
            pltpu.VMEM((N_KEYS, heads, tb), F32),
            pltpu.VMEM((N_KEYS, heads, tb), F32),
            pltpu.VMEM((ncand, heads, tb), F32),
            pltpu.VMEM((ncand, heads, tb), I32),
            pltpu.VMEM((PEER_TOPK, heads, tb), F32),
            pltpu.VMEM((PEER_TOPK, heads, tb), I32),
        ],
        compiler_params=_cparams(("parallel", "parallel")),
        name="peer_a_select",
    )(xn, down_t, down_t, scores)


PEER_B_GROUP = 16


def _peer_b_pick(a_ref, e_ref, gt_ref, rows):
    e = e_ref[rows, :]
    ai = lax.shift_right_logical(e, KEY_SHIFT)
    bi = e & (N_KEYS - 1)
    a_word = ai & (N_KEYS // 2 - 1)
    accs = [jnp.zeros(e.shape, U32) for _ in range(4)]
    for a in range(N_KEYS // 2):
        blk = a_ref[rows, a * N_KEYS:(a + 1) * N_KEYS]
        accs[a % 4] = jnp.where(a_word == a, jnp.take_along_axis(blk, bi, axis=1), accs[a % 4])
    word = (accs[0] | accs[1]) | (accs[2] | accs[3])
    lo = pltpu.unpack_elementwise(word, index=0, packed_dtype=BF16, unpacked_dtype=F32)
    hi = pltpu.unpack_elementwise(word, index=1, packed_dtype=BF16, unpacked_dtype=F32)
    picked = jnp.where(ai >= N_KEYS // 2, hi, lo)
    return gt_ref[rows, :] * _gelu(picked), e


def _peer_b_scatter(m_ref, rows, w, e):
    sub = lax.broadcasted_iota(I32, (e.shape[0], N_KEYS, e.shape[1]), 1)
    ai = lax.shift_right_logical(e, KEY_SHIFT)
    bi = e & (N_KEYS - 1)
    w1t = jnp.where(sub == ai[:, None, :], w[:, None, :], 0.0).astype(BF16)
    e2t = jnp.where(sub == bi[:, None, :], 1.0, 0.0).astype(BF16)
    m3 = jnp.einsum("cas,cbs->cab", w1t, e2t, preferred_element_type=F32)
    mt = jnp.swapaxes(m3, 0, 1).astype(m_ref.dtype)
    for a in range(N_KEYS):
        m_ref[rows, a * N_KEYS:(a + 1) * N_KEYS] = mt[a]


def _peer_b_kernel(a_ref, e_ref, gt_ref, m_ref):
    grp = PEER_B_GROUP
    ngrp = a_ref.shape[0] // grp

    def rows_of(gi):
        return pl.ds(pl.multiple_of(gi * grp, grp), grp)

    def step(gi, carry):
        nxt = _peer_b_pick(a_ref, e_ref, gt_ref, rows_of(gi))
        _peer_b_scatter(m_ref, rows_of(gi - 1), *carry)
        return nxt

    last = lax.fori_loop(1, ngrp, step, _peer_b_pick(a_ref, e_ref, gt_ref, rows_of(0)))
    _peer_b_scatter(m_ref, rows_of(ngrp - 1), *last)


def _peer_b(a2, eidx, gate, tc=256):
    t = a2.shape[0]
    n = 2 * a2.shape[1]
    nsel = eidx.shape[1]
    assert t % tc == 0 and tc % PEER_B_GROUP == 0 and n == N_KEYS * N_KEYS and nsel == N_KEYS, (t, n, nsel)
    return pl.pallas_call(
        _peer_b_kernel,
        grid=(t // tc,),
        in_specs=[pl.BlockSpec((tc, n // 2), lambda i: (i, 0)), pl.BlockSpec((tc, nsel), lambda i: (i, 0)),
                  pl.BlockSpec((tc, nsel), lambda i: (i, 0))],
        out_specs=pl.BlockSpec((tc, n), lambda i: (i, 0)),
        out_shape=jax.ShapeDtypeStruct((t, n), BF16),
        compiler_params=_cparams(("parallel",)),
        name="peer_b",
    )(a2, eidx, gate)


def _peer_c_kernel(m_ref, up_ref, x1_ref, fn_ref, o_ref, acc_s):
    k = pl.program_id(1)

    @pl.when(k == 0)
    def _():
        acc_s[...] = jnp.zeros_like(acc_s)

    acc_s[...] += jnp.dot(m_ref[...], up_ref[...], preferred_element_type=F32)

    @pl.when(k == pl.num_programs(1) - 1)
    def _():
        o_ref[...] = _rms(x1_ref[...] + acc_s[...], fn_ref[...])


def _peer_c(m, up, x1, final_norm, tm=1024, tk=1024):
    t, n = m.shape
    d = up.shape[1]
    assert t % tm == 0 and n % tk == 0, (t, n)
    return pl.pallas_call(
        _peer_c_kernel,
        grid=(t // tm, n // tk),
        in_specs=[pl.BlockSpec((tm, tk), lambda i, k: (i, k)), pl.BlockSpec((tk, d), lambda i, k: (k, 0)),
                  pl.BlockSpec((tm, d), lambda i, k: (i, 0), pipeline_mode=pl.Buffered(1)),
                  pl.BlockSpec((1, d), lambda i, k: (0, 0))],
        out_specs=pl.BlockSpec((tm, d), lambda i, k: (i, 0)),
        out_shape=jax.ShapeDtypeStruct((t, d), F32),
        scratch_shapes=[pltpu.VMEM((tm, d), F32)],
        compiler_params=_cparams(("parallel", "arbitrary")),
        name="peer_c",
    )(m, up, x1, final_norm)


def _prep_weights(norm1, w_in, a_ln_g, a_ln_b, a_ws, a_bs, b_conv_w, b_conv_b, b_bias, mix_norm, w_out, norm2,
                  peer_wq, peer_k1, peer_k2, peer_down, peer_up, final_norm):
    d = w_in.shape[0]
    d_a = a_ln_g.shape[0]
    d_b = b_bias.shape[0]
    heads, n_keys, half_key = peer_k1.shape
    row = lambda v: v.reshape(1, -1).astype(F32)
    w = dict(
        norm1=row(norm1), norm2=row(norm2), final_norm=row(final_norm),
        w_u=w_in[:, :d_a].astype(BF16), w_v=w_in[:, d_a:2 * d_a].astype(BF16),
        w_b=w_in[:, 2 * d_a:].reshape(d, 3, d_b).transpose(1, 0, 2).astype(BF16),
        ln_g=row(a_ln_g), ln_b=row(a_ln_b),
        ws=a_ws.astype(BF16),
        bsb=jnp.broadcast_to(a_bs[:, :, None], a_bs.shape + (A_HEAD_DIM,)).astype(F32),
        conv_w=b_conv_w.reshape(3, 3, d_b).transpose(1, 0, 2).astype(F32),
        conv_b=b_conv_b.reshape(3, d_b).astype(F32),
        bias=row(b_bias), mix_g_a=row(mix_norm[:d_a]), mix_g_b=row(mix_norm[d_a:]),
        w_out=w_out.astype(BF16),
        down_t=peer_down.astype(BF16).T, up=peer_up.astype(BF16),
    )
    wq_t = peer_wq.T.reshape(heads, 2, half_key, d).transpose(1, 0, 2, 3).reshape(2 * heads * half_key, d)
    w["wq_t"] = wq_t.astype(BF16)
    eye = jnp.eye(heads, dtype=F32)
    kbd = [jnp.einsum("hnd,hg->nhgd", k, eye).reshape(n_keys * heads, heads * half_key) for k in (peer_k1, peer_k2)]
    w["kbd"] = jnp.stack(kbd).astype(BF16)
    w["heads"] = heads
    return w


def _trunk(x3, w, hf):
    bsz, seq_len, d = x3.shape
    t = bsz * seq_len
    x = x3.reshape(t, d)
    yan, x0, g = _inproj(x, seq_len, w["norm1"], w["w_u"], w["w_v"], w["w_b"], w["ln_g"], w["ln_b"],
                         w["ws"], w["bsb"], w["conv_w"], w["conv_b"], w["mix_g_a"])
    d_b = x0.shape[1]
    yconv = _hyena_long_conv(g.reshape(bsz, seq_len, d_b), *hf).reshape(t, d_b)
    x1, xn = _postmix(x, yan, x0, g, yconv, w["bias"], w["mix_g_b"], w["w_out"], w["norm2"])
    scores = _peer_scores(xn, w["wq_t"], w["kbd"], w["heads"])
    a2, eidx, gate = _peer_a_select(xn, w["down_t"], scores, w["heads"])
    m = _peer_b(a2, eidx, gate)
    out = _peer_c(m, w["up"], x1, w["final_norm"])
    return out.reshape(bsz, seq_len, d)


def kernel(x_prompt, x_sample, norm1, w_in, a_ln_g, a_ln_b, a_ws, a_bs, b_conv_w, b_conv_b, hf_w1, hf_b1, hf_w2, hf_b2, hf_w3, hf_b3, hf_freq, hf_w4, b_bias, mix_norm, w_out, norm2, peer_wq, peer_k1, peer_k2, peer_down, peer_up, final_norm):
    assert norm1.shape[0] == 1, "single-layer trunk"
    w = _prep_weights(norm1[0], w_in[0], a_ln_g[0], a_ln_b[0], a_ws[0], a_bs[0], b_conv_w[0], b_conv_b[0],
                      b_bias[0], mix_norm[0], w_out[0], norm2[0], peer_wq[0], peer_k1[0], peer_k2[0],
                      peer_down[0], peer_up[0], final_norm)
    hf = (hf_w1[0], hf_b1[0].reshape(1, -1), hf_w2[0], hf_b2[0].reshape(1, -1), hf_w3[0], hf_b3[0].reshape(1, -1),
          hf_freq[0], hf_w4[0])
    return (_trunk(x_prompt, w, hf), _trunk(x_sample, w, hf))
```

```python
import functools
import math

import numpy as np
import jax
import jax.numpy as jnp
from jax import lax
from jax.experimental import pallas as pl
from jax.experimental.pallas import tpu as pltpu

F32 = jnp.float32
BF16 = jnp.bfloat16
I32 = jnp.int32
U32 = jnp.uint32

EPS = 1e-6
V7X_SUBLANES = 8
V7X_VMEM_BYTES = 64 * 1024 * 1024
VMEM_LIMIT = V7X_VMEM_BYTES - 8 * 1024 * 1024

CHUNK = 128
A_HEAD_DIM = 128
EMB_DIM = 33
BANDS = (EMB_DIM - 1) // 2
DECAY_TARGET = 1e-2
FAST_DECAY_PCT = 0.3
SLOW_DECAY_PCT = 1.5
FFT_N1 = 64
FFT_N2 = 128
FFT_N = FFT_N1 * FFT_N2
CONV_BLOCK = FFT_N // 2
N_KEYS = 128
KEY_SHIFT = 7
PEER_TOPK = 16
HALF_KEY = 128


def _cparams(sem):
    return pltpu.CompilerParams(dimension_semantics=sem, vmem_limit_bytes=VMEM_LIMIT)


def _rms(xf, g):
    return xf * lax.rsqrt(jnp.mean(xf * xf, axis=-1, keepdims=True) + EPS) * g


def _gelu(x):
    return 0.5 * x * (1.0 + lax.erf(x * np.float32(math.sqrt(0.5))))


def _inproj_kernel(xp_ref, x_ref, xn_ref, n1_ref, wu_ref, wv_ref, wb_ref, lng_ref, lnb_ref,
                   ws_ref, bsb_ref, cw_ref, cb_ref, mga_ref,
                   yan_ref, x0_ref, g_ref, ya_s, zs_s, *, tiles_per_seq, cblk):
    tm = x_ref.shape[0]
    d_a = wu_ref.shape[1]
    d_b = wb_ref.shape[2]
    i = pl.program_id(0)
    not_first = (i % tiles_per_seq != 0).astype(F32)
    not_last = (i % tiles_per_seq != tiles_per_seq - 1).astype(F32)

    xcat = jnp.concatenate([xp_ref[...], x_ref[...], xn_ref[...]], axis=0)
    hcat = _rms(xcat, n1_ref[...])
    h_all = hcat.astype(BF16)
    h = hcat[V7X_SUBLANES:V7X_SUBLANES + tm].astype(BF16)

    u = _gelu(jnp.dot(h, wu_ref[...], preferred_element_type=F32))
    v = _gelu(jnp.dot(h, wv_ref[...], preferred_element_type=F32))
    mu = jnp.mean(v, axis=-1, keepdims=True)
    vc = v - mu
    var = jnp.mean(vc * vc, axis=-1, keepdims=True)
    vb = (vc * lax.rsqrt(var + EPS) * lng_ref[...] + lnb_ref[...]).astype(BF16)
    chunks = [slice(c * CHUNK, (c + 1) * CHUNK) for c in range(tm // CHUNK)]
    for hd in range(d_a // A_HEAD_DIM):
        cols = slice(hd * A_HEAD_DIM, (hd + 1) * A_HEAD_DIM)
        v_head = jnp.concatenate([vb[rows, cols] for rows in chunks], axis=1)
        mixed = jnp.dot(ws_ref[hd], v_head, preferred_element_type=F32)
        for c, rows in enumerate(chunks):
            ya_s[rows, cols] = u[rows, cols] * (mixed[:, c * A_HEAD_DIM:(c + 1) * A_HEAD_DIM] + bsb_ref[hd])
    yan_ref[...] = _rms(ya_s[...], mga_ref[...]).astype(yan_ref.dtype)

    halo = V7X_SUBLANES
    for cb in range(d_b // cblk):
        cols = slice(cb * cblk, (cb + 1) * cblk)
        parts = []
        for p in range(3):
            z = jnp.dot(h_all, wb_ref[p, :, cols], preferred_element_type=F32)
            zs_s[...] = z
            zs_s[0:halo, :] = z[0:halo] * not_first
            zs_s[tm + halo:tm + 2 * halo, :] = z[tm + halo:tm + 2 * halo] * not_last
            w = cw_ref[p]
            zc = (cb_ref[p:p + 1, cols]
                  + zs_s[halo - 1:halo - 1 + tm, :] * w[0:1, cols]
                  + zs_s[halo:halo + tm, :] * w[1:2, cols]
                  + zs_s[halo + 1:halo + 1 + tm, :] * w[2:3, cols])
            parts.append(zc)
        x0_ref[:, cols] = parts[0]
        g_ref[:, cols] = parts[1] * parts[2]


def _inproj(x, seq_len, norm1, w_u, w_v, w_b, ln_g, ln_b, ws, bsb, conv_w, conv_b, mix_g_a, tm=512, cblk=256):
    t, d = x.shape
    d_a = w_u.shape[1]
    d_b = w_b.shape[2]
    assert seq_len % tm == 0 and t % seq_len == 0 and tm % CHUNK == 0 and d_b % cblk == 0, (t, seq_len)
    nblk = tm // V7X_SUBLANES
    last8 = t // V7X_SUBLANES - 1
    const = lambda *shape: pl.BlockSpec(shape, lambda i: (0,) * len(shape), pipeline_mode=pl.Buffered(1))
    kern = functools.partial(_inproj_kernel, tiles_per_seq=seq_len // tm, cblk=cblk)
    return pl.pallas_call(
        kern,
        grid=(t // tm,),
        in_specs=[
            pl.BlockSpec((V7X_SUBLANES, d), lambda i: (jnp.maximum(i * nblk - 1, 0), 0)),
            pl.BlockSpec((tm, d), lambda i: (i, 0)),
            pl.BlockSpec((V7X_SUBLANES, d), lambda i: (jnp.minimum((i + 1) * nblk, last8), 0)),
            const(1, d), const(d, d_a), const(d, d_a), const(3, d, d_b), const(1, d_a), const(1, d_a),
            const(*ws.shape), const(*bsb.shape), const(3, 3, d_b), const(3, d_b), const(1, d_a),
        ],
        out_specs=[
            pl.BlockSpec((tm, d_a), lambda i: (i, 0)),
            pl.BlockSpec((tm, d_b), lambda i: (i, 0)),
            pl.BlockSpec((tm, d_b), lambda i: (i, 0)),
        ],
        out_shape=[
            jax.ShapeDtypeStruct((t, d_a), BF16),
            jax.ShapeDtypeStruct((t, d_b), F32),
            jax.ShapeDtypeStruct((t, d_b), F32),
        ],
        scratch_shapes=[pltpu.VMEM((tm, d_a), F32), pltpu.VMEM((tm + 2 * V7X_SUBLANES, cblk), F32)],
        compiler_params=_cparams(("parallel",)),
        name="inproj",
    )(x, x, x, norm1, w_u, w_v, w_b, ln_g, ln_b, ws, bsb, conv_w, conv_b, mix_g_a)


def _filt_kernel(fr_ref, dl_ref, w1_ref, b1_ref, w2_ref, b2_ref, w3_ref, b3_ref, fq_ref, w4_ref, o_ref,
                 *, seq_len, nblocks, n2_per_step):
    half = FFT_N1 // 2
    tr = half * n2_per_step
    e = pl.program_id(0) - nblocks

    def lags(shape, axis):
        r = lax.broadcasted_iota(I32, shape, axis)
        n2 = pl.program_id(1) * n2_per_step + r // half
        return e * CONV_BLOCK + (r % half) * FFT_N2 + n2

    inv_len = np.float32(max(seq_len - 1, 1))
    pos = jnp.abs(lags((1, tr), 1)).astype(F32)
    t = pos / inv_len
    fw = (np.float32(2.0 * math.pi / seq_len) * pos) * fr_ref[...]
    hi = lax.Precision.HIGHEST
    w1 = w1_ref[...]
    z1 = (w1[:, 0:1] * t
          + jnp.dot(w1[:, 1:1 + BANDS], jnp.cos(fw), precision=hi, preferred_element_type=F32)
          + jnp.dot(w1[:, 1 + BANDS:], -jnp.sin(fw), precision=hi, preferred_element_type=F32))
    fq = fq_ref[...]
    h = jnp.sin(fq[:, 0:1] * (z1 + b1_ref[...]))
    h = jnp.sin(fq[:, 1:2] * (jnp.dot(w2_ref[...], h, precision=hi, preferred_element_type=F32) + b2_ref[...]))
    h = jnp.sin(fq[:, 2:3] * (jnp.dot(w3_ref[...], h, precision=hi, preferred_element_type=F32) + b3_ref[...]))
    h4 = jnp.dot(h.T, w4_ref[...], precision=hi, preferred_element_type=F32)
    lag_col = lags((tr, 1), 0)
    t_col = jnp.abs(lag_col).astype(F32) / inv_len
    window = jnp.exp(-t_col * dl_ref[...])
    res = jnp.where(jnp.abs(lag_col) <= seq_len - 1, h4 * window, 0.0)
    d_b = res.shape[1]
    for i in range(n2_per_step):
        o_ref[:, i * d_b:(i + 1) * d_b] = res[i * half:(i + 1) * half]


def _hyena_filter_blocks(seq_len, w1, b1, w2, b2, w3, b3, freq, w4, n2_per_step=16):
    nblocks = seq_len // CONV_BLOCK
    width = w1.shape[1]
    d_b = w4.shape[1] // 2
    fr = jnp.asarray(np.linspace(1e-4, BANDS - 1, BANDS, dtype=np.float32)[:, None])
    min_decay = math.log(DECAY_TARGET) / SLOW_DECAY_PCT
    max_decay = math.log(DECAY_TARGET) / FAST_DECAY_PCT
    deltas = jnp.asarray(np.abs(np.linspace(min_decay, max_decay, d_b, dtype=np.float32))[None, :])
    w4h = w4.reshape(width, 2, d_b).transpose(1, 0, 2)
    const = lambda *shape: pl.BlockSpec(shape, lambda a, b: (0,) * len(shape))
    half = FFT_N1 // 2
    kern = functools.partial(_filt_kernel, seq_len=seq_len, nblocks=nblocks, n2_per_step=n2_per_step)
    return pl.pallas_call(
        kern,
        grid=(2 * nblocks, FFT_N2 // n2_per_step),
        in_specs=[
            const(BANDS, 1), const(1, d_b), const(width, EMB_DIM), const(width, 1), const(width, width),
            const(width, 1), const(width, width), const(width, 1), const(width, 3),
            pl.BlockSpec((None, width, d_b), lambda ei, r: (jnp.where(ei >= nblocks, 0, 1), 0, 0)),
        ],
        out_specs=pl.BlockSpec((None, half, n2_per_step * d_b), lambda ei, r: (ei, 0, r)),
        out_shape=jax.ShapeDtypeStruct((2 * nblocks, half, FFT_N2 * d_b), F32),
        compiler_params=_cparams(("parallel", "parallel")),
        name="hyena_filter",
    )(fr, deltas, w1.T, b1.reshape(width, 1), w2.T, b2.reshape(width, 1), w3.T, b3.reshape(width, 1), freq.T, w4h)


def _dft_tables():
    n1 = np.arange(FFT_N1)
    f64 = np.exp(-2j * np.pi * np.outer(n1, n1) / FFT_N1)
    half = FFT_N1 // 2
    fr, fi = f64.real, f64.imag
    lhs_data = np.block([[fr[:, :half], -fi[:, :half]], [fi[:, :half], fr[:, :half]]])
    lhs_real = np.concatenate([fr, fi], axis=0)
    gr, gi = fr[:half, :] / FFT_N, -fi[:half, :] / FFT_N
    lhs_inv = np.block([[gr, -gi], [gi, gr]])
    k1 = np.arange(FFT_N1)[:, None, None]
    k2 = np.arange(FFT_N2)[None, :, None]
    n2 = np.arange(FFT_N2)[None, None, :]
    g = np.exp(-2j * np.pi * (n2 * (k1 + FFT_N1 * k2) % FFT_N) / FFT_N)
    gfwd = np.concatenate([np.concatenate([g.real, -g.imag], axis=2),
                           np.concatenate([g.imag, g.real], axis=2)], axis=1)
    ht = np.conj(np.transpose(g, (0, 2, 1)))
    ginv = np.concatenate([np.concatenate([ht.real, -ht.imag], axis=2),
                           np.concatenate([ht.imag, ht.real], axis=2)], axis=1)
    as32 = lambda a: np.asarray(a, dtype=np.float32)
    return as32(lhs_data), as32(lhs_real), as32(lhs_inv), as32(gfwd), as32(ginv)


def _split(x):
    hi = x.astype(BF16)
    lo = (x - hi.astype(F32)).astype(BF16)
    return hi, lo


def _dot3(a_hi, a_lo, b):
    if b.dtype == BF16:
        return jnp.dot(a_hi, b, preferred_element_type=F32) + jnp.dot(a_lo, b, preferred_element_type=F32)
    b_hi, b_lo = _split(b)
    acc = jnp.dot(a_hi, b_hi, preferred_element_type=F32)
    acc = acc + jnp.dot(a_hi, b_lo, preferred_element_type=F32)
    acc = acc + jnp.dot(a_lo, b_hi, preferred_element_type=F32)
    return acc


SPECTRA_DTYPE = BF16
FFT_TN2 = 16


def _rows_to_lanes(x):
    xt = jnp.swapaxes(x, 0, 1)
    return jnp.concatenate([xt[i] for i in range(xt.shape[0])], axis=1)


def _lanes_to_rows(x, s):
    c = x.shape[1] // s
    return jnp.swapaxes(jnp.stack([x[:, i * c:(i + 1) * c] for i in range(s)], axis=0), 0, 1)


def _fft1_kernel(lh_ref, ll_ref, a_ref, b_ref, o_ref, *, zero_lanes, natural):
    if natural:
        a = _rows_to_lanes(a_ref[...])
        b = _rows_to_lanes(b_ref[...])
    else:
        a = a_ref[...]
        b = b_ref[...]
    if zero_lanes:
        row = lax.broadcasted_iota(I32, b.shape, 0)
        lane = lax.broadcasted_iota(I32, b.shape, 1) + pl.program_id(2) * b.shape[1]
        b = jnp.where(jnp.logical_and(row == 0, lane < zero_lanes), 0.0, b)
    res = _dot3(lh_ref[...], ll_ref[...], jnp.concatenate([a, b], axis=0))
    o_ref[...] = _lanes_to_rows(res, FFT_TN2).reshape(o_ref.shape).astype(o_ref.dtype)


def _fft1(lhs, x, a_of, b_of, g, p, d_b, natural, zero_lanes=0):
    half = FFT_N1 // 2
    lh, ll = _split(jnp.asarray(lhs))
    kern = functools.partial(_fft1_kernel, zero_lanes=zero_lanes, natural=natural)
    if natural:
        blk = lambda of: pl.BlockSpec((None, None, half, FFT_TN2, d_b), lambda q, j, t: of(q, j) + (0, t, 0))
    else:
        blk = lambda of: pl.BlockSpec((None, None, half, FFT_TN2 * d_b), lambda q, j, t: of(q, j) + (0, t))
    return pl.pallas_call(
        kern,
        grid=(g, p, FFT_N2 // FFT_TN2),
        in_specs=[
            pl.BlockSpec(lhs.shape, lambda q, j, t: (0, 0)),
            pl.BlockSpec(lhs.shape, lambda q, j, t: (0, 0)),
            blk(a_of), blk(b_of),
        ],
        out_specs=pl.BlockSpec((None, None, 2, FFT_N1, FFT_TN2, d_b), lambda q, j, t: (q, j, 0, 0, t, 0)),
        out_shape=jax.ShapeDtypeStruct((g, p, 2, FFT_N1, FFT_N2, d_b), SPECTRA_DTYPE),
        compiler_params=_cparams(("parallel", "parallel", "parallel")),
        name="fft1",
    )(lh, ll, x, x)


def _fft3_kernel(lh_ref, ll_ref, w_ref, o_ref):
    half = FFT_N1 // 2
    c = w_ref.shape[-1]
    w = _rows_to_lanes(w_ref[...].astype(F32).reshape(2 * FFT_N1, FFT_TN2, c)).astype(w_ref.dtype)
    res = _dot3(lh_ref[...], ll_ref[...], w)
    o_ref[0] = _lanes_to_rows(res[:half], FFT_TN2)
    o_ref[1] = _lanes_to_rows(res[half:], FFT_TN2)


def _fft3(lhs, w6):
    g, p, _, _, _, d_b = w6.shape
    half = FFT_N1 // 2
    lh, ll = _split(jnp.asarray(lhs))
    return pl.pallas_call(
        _fft3_kernel,
        grid=(g, p, FFT_N2 // FFT_TN2),
        in_specs=[
            pl.BlockSpec(lhs.shape, lambda q, j, t: (0, 0)),
            pl.BlockSpec(lhs.shape, lambda q, j, t: (0, 0)),
            pl.BlockSpec((None, None, 2, FFT_N1, FFT_TN2, d_b), lambda q, j, t: (q, j, 0, 0, t, 0)),
        ],
        out_specs=pl.BlockSpec((None, 2, None, half, FFT_TN2, d_b), lambda q, j, t: (q, 0, j, 0, t, 0)),
        out_shape=jax.ShapeDtypeStruct((g, 2, p, half, FFT_N2, d_b), F32),
        compiler_params=_cparams(("parallel", "parallel", "parallel")),
        name="fft3",
    )(lh, ll, w6)


def _fft2_filter_kernel(gh_ref, gl_ref, x_ref, o_ref):
    ct = x_ref.shape[-1]
    for k in range(x_ref.shape[1]):
        z = _dot3(gh_ref[k], gl_ref[k], x_ref[:, k].reshape(2 * FFT_N2, ct))
        o_ref[:, k] = z.reshape(2, FFT_N2, ct).astype(o_ref.dtype)


def _fft2_filter(gfwd_hl, x1f, ct=256, kb=8):
    nd, _, _, _, d_b = x1f.shape
    gh, gl = gfwd_hl
    gspec = pl.BlockSpec((kb, 2 * FFT_N2, 2 * FFT_N2), lambda k, d, c: (k, 0, 0))
    xspec = pl.BlockSpec((None, 2, kb, FFT_N2, ct), lambda k, d, c: (d, 0, k, 0, c))
    return pl.pallas_call(
        _fft2_filter_kernel,
        grid=(FFT_N1 // kb, nd, d_b // ct),
        in_specs=[gspec, gspec, xspec],
        out_specs=xspec,
        out_shape=jax.ShapeDtypeStruct(x1f.shape, SPECTRA_DTYPE),
        compiler_params=_cparams(("parallel", "parallel", "parallel")),
        name="fft2_filter",
    )(gh, gl, x1f)


def _fft2_mix_kernel(gh_ref, gl_ref, ih_ref, il_ref, x_ref, k_ref, o_ref, *, nblocks):
    ct = x_ref.shape[-1]
    for k in range(x_ref.shape[2]):
        zs = []
        for j in range(nblocks):
            z = _dot3(gh_ref[k], gl_ref[k], x_ref[j, :, k].reshape(2 * FFT_N2, ct))
            zs.append((z[:FFT_N2], z[FFT_N2:]))
        for i in range(nblocks):
            yr = jnp.zeros((FFT_N2, ct), F32)
            yi = jnp.zeros((FFT_N2, ct), F32)
            for j in range(nblocks):
                d = i - j + nblocks - 1
                kr, ki = k_ref[d, 0, k].astype(F32), k_ref[d, 1, k].astype(F32)
                zr, zi = zs[j]
                yr = yr + kr * zr - ki * zi
                yi = yi + kr * zi + ki * zr
            w = _dot3(ih_ref[k], il_ref[k], jnp.concatenate([yr, yi], axis=0).astype(SPECTRA_DTYPE))
            o_ref[i, :, k] = w.reshape(2, FFT_N2, ct).astype(o_ref.dtype)


def _fft2_mix(gfwd_hl, ginv_hl, x1, kspec, ct=256):
    g, p, _, _, _, d_b = x1.shape
    nd = kspec.shape[0]
    kb = max(1, 8 // p)
    gspec = pl.BlockSpec((kb, 2 * FFT_N2, 2 * FFT_N2), lambda k, q, c: (k, 0, 0))
    xspec = pl.BlockSpec((None, p, 2, kb, FFT_N2, ct), lambda k, q, c: (q, 0, 0, k, 0, c))
    kern = functools.partial(_fft2_mix_kernel, nblocks=p)
    return pl.pallas_call(
        kern,
        grid=(FFT_N1 // kb, g, d_b // ct),
        in_specs=[gspec, gspec, gspec, gspec, xspec,
                  pl.BlockSpec((nd, 2, kb, FFT_N2, ct), lambda k, q, c: (0, 0, k, 0, c))],
        out_specs=xspec,
        out_shape=jax.ShapeDtypeStruct(x1.shape, SPECTRA_DTYPE),
        compiler_params=_cparams(("parallel", "parallel", "parallel")),
        name="fft2_mix",
    )(gfwd_hl[0], gfwd_hl[1], ginv_hl[0], ginv_hl[1], x1, kspec)


def _hyena_long_conv(g3, w1, b1, w2, b2, w3, b3, freq, w4):
    bsz, seq_len, d_b = g3.shape
    nblocks = seq_len // CONV_BLOCK
    half = FFT_N1 // 2
    lhs_data, lhs_real, lhs_inv, gfwd, ginv = _dft_tables()
    gfwd_hl = _split(jnp.asarray(gfwd))
    ginv_hl = _split(jnp.asarray(ginv))
    kext = _hyena_filter_blocks(seq_len, w1, b1, w2, b2, w3, b3, freq, w4)
    nd = 2 * nblocks - 1
    k1f = _fft1(lhs_real, kext[None], lambda q, j: (0, q + 1), lambda q, j: (0, q), nd, 1, d_b,
                natural=False, zero_lanes=d_b)
    kspec = _fft2_filter(gfwd_hl, k1f.reshape(nd, 2, FFT_N1, FFT_N2, d_b))
    g5 = g3.reshape(bsz, nblocks, half, FFT_N2, d_b)
    x1 = _fft1(lhs_data, g5, lambda q, j: (2 * q, j), lambda q, j: (2 * q + 1, j), bsz // 2, nblocks, d_b,
               natural=True)
    wmix = _fft2_mix(gfwd_hl, ginv_hl, x1, kspec)
    y6 = _fft3(lhs_inv, wmix)
    return y6.reshape(bsz, seq_len, d_b)


def _postmix_kernel(x_ref, yan_ref, x0_ref, g_ref, yc_ref, bias_ref, mgb_ref, wo_ref, n2_ref, x1_ref, xn_ref):
    d_a = yan_ref.shape[1]
    gg = g_ref[...]
    yb = x0_ref[...] * (yc_ref[...] + gg * bias_ref[...])
    ybn = _rms(yb, mgb_ref[...]).astype(BF16)
    y = jnp.dot(yan_ref[...], wo_ref[0:d_a, :], preferred_element_type=F32)
    y = y + jnp.dot(ybn, wo_ref[d_a:, :], preferred_element_type=F32)
    x1 = x_ref[...] + y
    x1_ref[...] = x1
    xn_ref[...] = _rms(x1, n2_ref[...]).astype(xn_ref.dtype)


def _postmix(x, yan, x0, g, yconv, bias, mix_g_b, w_out, norm2, tm=512):
    t, d = x.shape
    d_a = yan.shape[1]
    d_b = x0.shape[1]
    assert t % tm == 0, t
    row = lambda w: pl.BlockSpec((tm, w), lambda i: (i, 0))
    const = lambda *shape: pl.BlockSpec(shape, lambda i: (0,) * len(shape), pipeline_mode=pl.Buffered(1))
    return pl.pallas_call(
        _postmix_kernel,
        grid=(t // tm,),
        in_specs=[row(d), row(d_a), row(d_b), row(d_b), row(d_b), const(1, d_b), const(1, d_b),
                  const(d_a + d_b, d), const(1, d)],
        out_specs=[row(d), row(d)],
        out_shape=[jax.ShapeDtypeStruct((t, d), F32), jax.ShapeDtypeStruct((t, d), BF16)],
        compiler_params=_cparams(("parallel",)),
        name="postmix",
    )(x, yan, x0, g, yconv, bias, mix_g_b, w_out, norm2)


def _staircase():
    return [(i, j) for i in range(PEER_TOPK) for j in range(PEER_TOPK) if (i + 1) * (j + 1) <= PEER_TOPK]


def _stream_argmax(val_ref, tag_ref, nrows, prev, slab):
    neg = np.float32(-np.inf)
    best_v = jnp.full(slab, neg, F32)
    best_i = jnp.zeros(slab, I32)
    best_t = jnp.zeros(slab, I32)
    for n in range(nrows):
        v = jnp.where(prev == n, neg, val_ref[n])
        val_ref[n] = v
        better = v > best_v
        best_v = jnp.maximum(v, best_v)
        best_i = jnp.where(better, n, best_i)
        if tag_ref is not None:
            best_t = jnp.where(better, tag_ref[n], best_t)
    return best_v, best_i, best_t


def _scores_kernel(xn_ref, wqt_ref, kbd_ref, s_ref, *, heads):
    tm = xn_ref.shape[0]
    hk = heads * HALF_KEY
    q_t = lax.dot_general(wqt_ref[...], xn_ref[...], (((1,), (1,)), ((), ())),
                          preferred_element_type=F32).astype(BF16)
    for half in range(2):
        s_t = jnp.dot(kbd_ref[half], q_t[half * hk:(half + 1) * hk], preferred_element_type=F32)
        s_ref[half] = s_t.reshape(N_KEYS, heads, tm)


def _peer_scores(xn, wqt, kbd, heads, tm=512):
    t, d = xn.shape
    assert t % tm == 0 and heads == V7X_SUBLANES, (t, heads)
    const = lambda *shape: pl.BlockSpec(shape, lambda i: (0,) * len(shape), pipeline_mode=pl.Buffered(1))
    kern = functools.partial(_scores_kernel, heads=heads)
    return pl.pallas_call(
        kern,
        grid=(t // tm,),
        in_specs=[pl.BlockSpec((tm, d), lambda i: (i, 0)), const(*wqt.shape), const(*kbd.shape)],
        out_specs=pl.BlockSpec((2, N_KEYS, heads, tm), lambda i: (0, 0, 0, i)),
        out_shape=jax.ShapeDtypeStruct((2, N_KEYS, heads, t), F32),
        compiler_params=_cparams(("parallel",)),
        name="peer_scores",
    )(xn, wqt, kbd)


SELECT_BLOCK = 128


def _peer_a_select_kernel(xn_ref, dlo_ref, dhi_ref, s_ref, a_ref, e_ref, gt_ref, s0_s, s1_s, c_s, ce_s, b_s, es_s,
                          *, heads):
    xn = xn_ref[...]
    a_lo = jnp.dot(xn, dlo_ref[...], preferred_element_type=F32)
    a_hi = jnp.dot(xn, dhi_ref[...], preferred_element_type=F32)
    a_ref[...] = pltpu.pack_elementwise([a_lo, a_hi], packed_dtype=BF16)

    slab = (heads, s_ref.shape[-1])
    no_pick = jnp.full(slab, -1, I32)
    half_s = (s0_s, s1_s)
    vals, idxs, prevs = ([], []), ([], []), [no_pick, no_pick]
    for half in range(2):
        half_s[half][...] = s_ref[half]
    for r in range(PEER_TOPK):
        for half in range(2):
            m, prevs[half], _ = _stream_argmax(half_s[half], None, N_KEYS, prevs[half], slab)
            vals[half].append(m)
            idxs[half].append(prevs[half])
    cands = _staircase()
    for p, (i, j) in enumerate(cands):
        c_s[p] = vals[0][i] + vals[1][j]
        ce_s[p] = idxs[0][i] * N_KEYS + idxs[1][j]
    prev = no_pick
    for r in range(PEER_TOPK):
        m, prev, e = _stream_argmax(c_s, ce_s, len(cands), prev, slab)
        es_s[r] = e
        b_s[r] = m
    best = b_s[...]
    ex = jnp.exp(best - jnp.max(best, axis=0, keepdims=True))
    gate = ex / jnp.sum(ex, axis=0, keepdims=True)
    gt_ref[...] = gate.reshape(PEER_TOPK * heads, slab[1]).T
    e_ref[...] = es_s[...].reshape(PEER_TOPK * heads, slab[1]).T


def _peer_a_select(xn, down_t, scores, heads, tm=1024, tn=1024):
    t, d = xn.shape
    n = down_t.shape[1]
    tb = SELECT_BLOCK
    nsel = PEER_TOPK * heads
    ncand = len(_staircase())
    ni = t // tm
    nj = n // (2 * tn)
    assert t % tm == 0 and n % (2 * tn) == 0 and heads == V7X_SUBLANES, (t, n, heads)
    assert nj * ni * tb == t, "one selection block per grid step must cover all tokens"
    blk = lambda j, i: j * ni + i
    kern = functools.partial(_peer_a_select_kernel, heads=heads)
    return pl.pallas_call(
        kern,
        grid=(nj, ni),
        in_specs=[pl.BlockSpec((tm, d), lambda j, i: (i, 0)),
                  pl.BlockSpec((d, tn), lambda j, i: (0, j)), pl.BlockSpec((d, tn), lambda j, i: (0, j + nj)),
                  pl.BlockSpec((2, N_KEYS, heads, tb), lambda j, i: (0, 0, 0, blk(j, i)))],
        out_specs=[pl.BlockSpec((tm, tn), lambda j, i: (i, j)),
                   pl.BlockSpec((tb, nsel), lambda j, i: (blk(j, i), 0)),
                   pl.BlockSpec((tb, nsel), lambda j, i: (blk(j, i), 0))],
        out_shape=[jax.ShapeDtypeStruct((t, n // 2), U32), jax.ShapeDtypeStruct((t, nsel), I32),
                   jax.ShapeDtypeStruct((t, nsel), F32)],
        scratch_shapes=[
            pltpu.VMEM((N_KEYS, heads, tb), F32),
            pltpu.VMEM((N_KEYS, heads, tb), F32),
            pltpu.VMEM((ncand, heads, tb), F32),
            pltpu.VMEM((ncand, heads, tb), I32),
            pltpu.VMEM((PEER_TOPK, heads, tb), F32),
            pltpu.VMEM((PEER_TOPK, heads, tb), I32),
        ],
        compiler_params=_cparams(("parallel", "parallel")),
        name="peer_a_select",
    )(xn, down_t, down_t, scores)


PEER_B_GROUP = 16


def _peer_b_pick(a_ref, e_ref, gt_ref, rows):
    e = e_ref[rows, :]
    ai = lax.shift_right_logical(e, KEY_SHIFT)
    bi = e & (N_KEYS - 1)
    a_word = ai & (N_KEYS // 2 - 1)
    accs = [jnp.zeros(e.shape, U32) for _ in range(4)]
    for a in range(N_KEYS // 2):
        blk = a_ref[rows, a * N_KEYS:(a + 1) * N_KEYS]
        accs[a % 4] = jnp.where(a_word == a, jnp.take_along_axis(blk, bi, axis=1), accs[a % 4])
    word = (accs[0] | accs[1]) | (accs[2] | accs[3])
    lo = pltpu.unpack_elementwise(word, index=0, packed_dtype=BF16, unpacked_dtype=F32)
    hi = pltpu.unpack_elementwise(word, index=1, packed_dtype=BF16, unpacked_dtype=F32)
    picked = jnp.where(ai >= N_KEYS // 2, hi, lo)
    return gt_ref[rows, :] * _gelu(picked), e


def _peer_b_scatter(m_ref, rows, w, e):
    sub = lax.broadcasted_iota(I32, (e.shape[0], N_KEYS, e.shape[1]), 1)
    ai = lax.shift_right_logical(e, KEY_SHIFT)
    bi = e & (N_KEYS - 1)
    w1t = jnp.where(sub == ai[:, None, :], w[:, None, :], 0.0).astype(BF16)
    e2t = jnp.where(sub == bi[:, None, :], 1.0, 0.0).astype(BF16)
    m3 = jnp.einsum("cas,cbs->cab", w1t, e2t, preferred_element_type=F32)
    mt = jnp.swapaxes(m3, 0, 1).astype(m_ref.dtype)
    for a in range(N_KEYS):
        m_ref[rows, a * N_KEYS:(a + 1) * N_KEYS] = mt[a]


def _peer_b_kernel(a_ref, e_ref, gt_ref, m_ref):
    grp = PEER_B_GROUP
    ngrp = a_ref.shape[0] // grp

    def rows_of(gi):
        return pl.ds(pl.multiple_of(gi * grp, grp), grp)

    def step(gi, carry):
        nxt = _peer_b_pick(a_ref, e_ref, gt_ref, rows_of(gi))
        _peer_b_scatter(m_ref, rows_of(gi - 1), *carry)
        return nxt

    last = lax.fori_loop(1, ngrp, step, _peer_b_pick(a_ref, e_ref, gt_ref, rows_of(0)))
    _peer_b_scatter(m_ref, rows_of(ngrp - 1), *last)


def _peer_b(a2, eidx, gate, tc=256):
    t = a2.shape[0]
    n = 2 * a2.shape[1]
    nsel = eidx.shape[1]
    assert t % tc == 0 and tc % PEER_B_GROUP == 0 and n == N_KEYS * N_KEYS and nsel == N_KEYS, (t, n, nsel)
    return pl.pallas_call(
        _peer_b_kernel,
        grid=(t // tc,),
        in_specs=[pl.BlockSpec((tc, n // 2), lambda i: (i, 0)), pl.BlockSpec((tc, nsel), lambda i: (i, 0)),
                  pl.BlockSpec((tc, nsel), lambda i: (i, 0))],
        out_specs=pl.BlockSpec((tc, n), lambda i: (i, 0)),
        out_shape=jax.ShapeDtypeStruct((t, n), BF16),
        compiler_params=_cparams(("parallel",)),
        name="peer_b",
    )(a2, eidx, gate)


def _peer_c_kernel(m_ref, up_ref, x1_ref, fn_ref, o_ref, acc_s):
    k = pl.program_id(1)

    @pl.when(k == 0)
    def _():
        acc_s[...] = jnp.zeros_like(acc_s)

    acc_s[...] += jnp.dot(m_ref[...], up_ref[...], preferred_element_type=F32)

    @pl.when(k == pl.num_programs(1) - 1)
    def _():
        o_ref[...] = _rms(x1_ref[...] + acc_s[...], fn_ref[...])


def _peer_c(m, up, x1, final_norm, tm=1024, tk=1024):
    t, n = m.shape
    d = up.shape[1]
    assert t % tm == 0 and n % tk == 0, (t, n)
    return pl.pallas_call(
        _peer_c_kernel,
        grid=(t // tm, n // tk),
        in_specs=[pl.BlockSpec((tm, tk), lambda i, k: (i, k)), pl.BlockSpec((tk, d), lambda i, k: (k, 0)),
                  pl.BlockSpec((tm, d), lambda i, k: (i, 0), pipeline_mode=pl.Buffered(1)),
                  pl.BlockSpec((1, d), lambda i, k: (0, 0))],
        out_specs=pl.BlockSpec((tm, d), lambda i, k: (i, 0)),
        out_shape=jax.ShapeDtypeStruct((t, d), F32),
        scratch_shapes=[pltpu.VMEM((tm, d), F32)],
        compiler_params=_cparams(("parallel", "arbitrary")),
        name="peer_c",
    )(m, up, x1, final_norm)


def _prep_weights(norm1, w_in, a_ln_g, a_ln_b, a_ws, a_bs, b_conv_w, b_conv_b, b_bias, mix_norm, w_out, norm2,
                  peer_wq, peer_k1, peer_k2, peer_down, peer_up, final_norm):
    d = w_in.shape[0]
    d_a = a_ln_g.shape[0]
    d_b = b_bias.shape[0]
    heads, n_keys, half_key = peer_k1.shape
    row = lambda v: v.reshape(1, -1).astype(F32)
    w = dict(
        norm1=row(norm1), norm2=row(norm2), final_norm=row(final_norm),
        w_u=w_in[:, :d_a].astype(BF16), w_v=w_in[:, d_a:2 * d_a].astype(BF16),
        w_b=w_in[:, 2 * d_a:].reshape(d, 3, d_b).transpose(1, 0, 2).astype(BF16),
        ln_g=row(a_ln_g), ln_b=row(a_ln_b),
        ws=a_ws.astype(BF16),
        bsb=jnp.broadcast_to(a_bs[:, :, None], a_bs.shape + (A_HEAD_DIM,)).astype(F32),
        conv_w=b_conv_w.reshape(3, 3, d_b).transpose(1, 0, 2).astype(F32),
        conv_b=b_conv_b.reshape(3, d_b).astype(F32),
        bias=row(b_bias), mix_g_a=row(mix_norm[:d_a]), mix_g_b=row(mix_norm[d_a:]),
        w_out=w_out.astype(BF16),
        down_t=peer_down.astype(BF16).T, up=peer_up.astype(BF16),
    )
    wq_t = peer_wq.T.reshape(heads, 2, half_key, d).transpose(1, 0, 2, 3).reshape(2 * heads * half_key, d)
    w["wq_t"] = wq_t.astype(BF16)
    eye = jnp.eye(heads, dtype=F32)
    kbd = [jnp.einsum("hnd,hg->nhgd", k, eye).reshape(n_keys * heads, heads * half_key) for k in (peer_k1, peer_k2)]
    w["kbd"] = jnp.stack(kbd).astype(BF16)
    w["heads"] = heads
    return w


def _trunk(x3, w, hf):
    bsz, seq_len, d = x3.shape
    t = bsz * seq_len
    x = x3.reshape(t, d)
    yan, x0, g = _inproj(x, seq_len, w["norm1"], w["w_u"], w["w_v"], w["w_b"], w["ln_g"], w["ln_b"],
                         w["ws"], w["bsb"], w["conv_w"], w["conv_b"], w["mix_g_a"])
    d_b = x0.shape[1]
    yconv = _hyena_long_conv(g.reshape(bsz, seq_len, d_b), *hf).reshape(t, d_b)
    x1, xn = _postmix(x, yan, x0, g, yconv, w["bias"], w["mix_g_b"], w["w_out"], w["norm2"])
    scores = _peer_scores(xn, w["wq_t"], w["kbd"], w["heads"])
    a2, eidx, gate = _peer_a_select(xn, w["down_t"], scores, w["heads"])
    m = _peer_b(a2, eidx, gate)
    out = _peer_c(m, w["up"], x1, w["final_norm"])
    return out.reshape(bsz, seq_len, d)


def kernel(x_prompt, x_sample, norm1, w_in, a_ln_g, a_ln_b, a_ws, a_bs, b_conv_w, b_conv_b, hf_w1, hf_b1, hf_w2, hf_b2, hf_w3, hf_b3, hf_freq, hf_w4, b_bias, mix_norm, w_out, norm2, peer_wq, peer_k1, peer_k2, peer_down, peer_up, final_norm):
    assert norm1.shape[0] == 1, "single-layer trunk"
    w = _prep_weights(norm1[0], w_in[0], a_ln_g[0], a_ln_b[0], a_ws[0], a_bs[0], b_conv_w[0], b_conv_b[0],
                      b_bias[0], mix_norm[0], w_out[0], norm2[0], peer_wq[0], peer_k1[0], peer_k2[0],
                      peer_down[0], peer_up[0], final_norm)
    hf = (hf_w1[0], hf_b1[0].reshape(1, -1), hf_w2[0], hf_b2[0].reshape(1, -1), hf_w3[0], hf_b3[0].reshape(1, -1),
          hf_freq[0], hf_w4[0])
    return (_trunk(x_prompt, w, hf), _trunk(x_sample, w, hf))
```

```python
import functools
import math

import numpy as np
import jax
import jax.numpy as jnp
from jax import lax
from jax.experimental import pallas as pl
from jax.experimental.pallas import tpu as pltpu

F32 = jnp.float32
BF16 = jnp.bfloat16
I32 = jnp.int32
U32 = jnp.uint32

EPS = 1e-6
V7X_SUBLANES = 8
V7X_VMEM_BYTES = 64 * 1024 * 1024
VMEM_LIMIT = V7X_VMEM_BYTES - 8 * 1024 * 1024

CHUNK = 128
A_HEAD_DIM = 128
EMB_DIM = 33
BANDS = (EMB_DIM - 1) // 2
DECAY_TARGET = 1e-2
FAST_DECAY_PCT = 0.3
SLOW_DECAY_PCT = 1.5
FFT_N1 = 64
FFT_N2 = 128
FFT_N = FFT_N1 * FFT_N2
CONV_BLOCK = FFT_N // 2
N_KEYS = 128
KEY_SHIFT = 7
PEER_TOPK = 16
HALF_KEY = 128


def _cparams(sem):
    return pltpu.CompilerParams(dimension_semantics=sem, vmem_limit_bytes=VMEM_LIMIT)


def _rms(xf, g):
    return xf * lax.rsqrt(jnp.mean(xf * xf, axis=-1, keepdims=True) + EPS) * g


def _gelu(x):
    return 0.5 * x * (1.0 + lax.erf(x * np.float32(math.sqrt(0.5))))


def _inproj_kernel(xp_ref, x_ref, xn_ref, n1_ref, wu_ref, wv_ref, wb_ref, lng_ref, lnb_ref,
                   ws_ref, bsb_ref, cw_ref, cb_ref, mga_ref,
                   yan_ref, x0_ref, g_ref, ya_s, zs_s, *, tiles_per_seq, cblk):
    tm = x_ref.shape[0]
    d_a = wu_ref.shape[1]
    d_b = wb_ref.shape[2]
    i = pl.program_id(0)
    not_first = (i % tiles_per_seq != 0).astype(F32)
    not_last = (i % tiles_per_seq != tiles_per_seq - 1).astype(F32)

    xcat = jnp.concatenate([xp_ref[...], x_ref[...], xn_ref[...]], axis=0)
    hcat = _rms(xcat, n1_ref[...])
    h_all = hcat.astype(BF16)
    h = hcat[V7X_SUBLANES:V7X_SUBLANES + tm].astype(BF16)

    u = _gelu(jnp.dot(h, wu_ref[...], preferred_element_type=F32))
    v = _gelu(jnp.dot(h, wv_ref[...], preferred_element_type=F32))
    mu = jnp.mean(v, axis=-1, keepdims=True)
    vc = v - mu
    var = jnp.mean(vc * vc, axis=-1, keepdims=True)
    vb = (vc * lax.rsqrt(var + EPS) * lng_ref[...] + lnb_ref[...]).astype(BF16)
    chunks = [slice(c * CHUNK, (c + 1) * CHUNK) for c in range(tm // CHUNK)]
    for hd in range(d_a // A_HEAD_DIM):
        cols = slice(hd * A_HEAD_DIM, (hd + 1) * A_HEAD_DIM)
        v_head = jnp.concatenate([vb[rows, cols] for rows in chunks], axis=1)
        mixed = jnp.dot(ws_ref[hd], v_head, preferred_element_type=F32)
        for c, rows in enumerate(chunks):
            ya_s[rows, cols] = u[rows, cols] * (mixed[:, c * A_HEAD_DIM:(c + 1) * A_HEAD_DIM] + bsb_ref[hd])
    yan_ref[...] = _rms(ya_s[...], mga_ref[...]).astype(yan_ref.dtype)

    halo = V7X_SUBLANES
    for cb in range(d_b // cblk):
        cols = slice(cb * cblk, (cb + 1) * cblk)
        parts = []
        for p in range(3):
            z = jnp.dot(h_all, wb_ref[p, :, cols], preferred_element_type=F32)
            zs_s[...] = z
            zs_s[0:halo, :] = z[0:halo] * not_first
            zs_s[tm + halo:tm + 2 * halo, :] = z[tm + halo:tm + 2 * halo] * not_last
            w = cw_ref[p]
            zc = (cb_ref[p:p + 1, cols]
                  + zs_s[halo - 1:halo - 1 + tm, :] * w[0:1, cols]
                  + zs_s[halo:halo + tm, :] * w[1:2, cols]
                  + zs_s[halo + 1:halo + 1 + tm, :] * w[2:3, cols])
            parts.append(zc)
        x0_ref[:, cols] = parts[0]
        g_ref[:, cols] = parts[1] * parts[2]


def _inproj(x, seq_len, norm1, w_u, w_v, w_b, ln_g, ln_b, ws, bsb, conv_w, conv_b, mix_g_a, tm=512, cblk=256):
    t, d = x.shape
    d_a = w_u.shape[1]
    d_b = w_b.shape[2]
    assert seq_len % tm == 0 and t % seq_len == 0 and tm % CHUNK == 0 and d_b % cblk == 0, (t, seq_len)
    nblk = tm // V7X_SUBLANES
    last8 = t // V7X_SUBLANES - 1
    const = lambda *shape: pl.BlockSpec(shape, lambda i: (0,) * len(shape), pipeline_mode=pl.Buffered(1))
    kern = functools.partial(_inproj_kernel, tiles_per_seq=seq_len // tm, cblk=cblk)
    return pl.pallas_call(
        kern,
        grid=(t // tm,),
        in_specs=[
            pl.BlockSpec((V7X_SUBLANES, d), lambda i: (jnp.maximum(i * nblk - 1, 0), 0)),
            pl.BlockSpec((tm, d), lambda i: (i, 0)),
            pl.BlockSpec((V7X_SUBLANES, d), lambda i: (jnp.minimum((i + 1) * nblk, last8), 0)),
            const(1, d), const(d, d_a), const(d, d_a), const(3, d, d_b), const(1, d_a), const(1, d_a),
            const(*ws.shape), const(*bsb.shape), const(3, 3, d_b), const(3, d_b), const(1, d_a),
        ],
        out_specs=[
            pl.BlockSpec((tm, d_a), lambda i: (i, 0)),
            pl.BlockSpec((tm, d_b), lambda i: (i, 0)),
            pl.BlockSpec((tm, d_b), lambda i: (i, 0)),
        ],
        out_shape=[
            jax.ShapeDtypeStruct((t, d_a), BF16),
            jax.ShapeDtypeStruct((t, d_b), F32),
            jax.ShapeDtypeStruct((t, d_b), F32),
        ],
        scratch_shapes=[pltpu.VMEM((tm, d_a), F32), pltpu.VMEM((tm + 2 * V7X_SUBLANES, cblk), F32)],
        compiler_params=_cparams(("parallel",)),
        name="inproj",
    )(x, x, x, norm1, w_u, w_v, w_b, ln_g, ln_b, ws, bsb, conv_w, conv_b, mix_g_a)


def _filt_kernel(fr_ref, dl_ref, w1_ref, b1_ref, w2_ref, b2_ref, w3_ref, b3_ref, fq_ref, w4_ref, o_ref,
                 *, seq_len, nblocks, n2_per_step):
    half = FFT_N1 // 2
    tr = half * n2_per_step
    e = pl.program_id(0) - nblocks

    def lags(shape, axis):
        r = lax.broadcasted_iota(I32, shape, axis)
        n2 = pl.program_id(1) * n2_per_step + r // half
        return e * CONV_BLOCK + (r % half) * FFT_N2 + n2

    inv_len = np.float32(max(seq_len - 1, 1))
    pos = jnp.abs(lags((1, tr), 1)).astype(F32)
    t = pos / inv_len
    fw = (np.float32(2.0 * math.pi / seq_len) * pos) * fr_ref[...]
    hi = lax.Precision.HIGHEST
    w1 = w1_ref[...]
    z1 = (w1[:, 0:1] * t
          + jnp.dot(w1[:, 1:1 + BANDS], jnp.cos(fw), precision=hi, preferred_element_type=F32)
          + jnp.dot(w1[:, 1 + BANDS:], -jnp.sin(fw), precision=hi, preferred_element_type=F32))
    fq = fq_ref[...]
    h = jnp.sin(fq[:, 0:1] * (z1 + b1_ref[...]))
    h = jnp.sin(fq[:, 1:2] * (jnp.dot(w2_ref[...], h, precision=hi, preferred_element_type=F32) + b2_ref[...]))
    h = jnp.sin(fq[:, 2:3] * (jnp.dot(w3_ref[...], h, precision=hi, preferred_element_type=F32) + b3_ref[...]))
    h4 = jnp.dot(h.T, w4_ref[...], precision=hi, preferred_element_type=F32)
    lag_col = lags((tr, 1), 0)
    t_col = jnp.abs(lag_col).astype(F32) / inv_len
    window = jnp.exp(-t_col * dl_ref[...])
    res = jnp.where(jnp.abs(lag_col) <= seq_len - 1, h4 * window, 0.0)
    d_b = res.shape[1]
    for i in range(n2_per_step):
        o_ref[:, i * d_b:(i + 1) * d_b] = res[i * half:(i + 1) * half]


def _hyena_filter_blocks(seq_len, w1, b1, w2, b2, w3, b3, freq, w4, n2_per_step=16):
    nblocks = seq_len // CONV_BLOCK
    width = w1.shape[1]
    d_b = w4.shape[1] // 2
    fr = jnp.asarray(np.linspace(1e-4, BANDS - 1, BANDS, dtype=np.float32)[:, None])
    min_decay = math.log(DECAY_TARGET) / SLOW_DECAY_PCT
    max_decay = math.log(DECAY_TARGET) / FAST_DECAY_PCT
    deltas = jnp.asarray(np.abs(np.linspace(min_decay, max_decay, d_b, dtype=np.float32))[None, :])
    w4h = w4.reshape(width, 2, d_b).transpose(1, 0, 2)
    const = lambda *shape: pl.BlockSpec(shape, lambda a, b: (0,) * len(shape))
    half = FFT_N1 // 2
    kern = functools.partial(_filt_kernel, seq_len=seq_len, nblocks=nblocks, n2_per_step=n2_per_step)
    return pl.pallas_call(
        kern,
        grid=(2 * nblocks, FFT_N2 // n2_per_step),
        in_specs=[
            const(BANDS, 1), const(1, d_b), const(width, EMB_DIM), const(width, 1), const(width, width),
            const(width, 1), const(width, width), const(width, 1), const(width, 3),
            pl.BlockSpec((None, width, d_b), lambda ei, r: (jnp.where(ei >= nblocks, 0, 1), 0, 0)),
        ],
        out_specs=pl.BlockSpec((None, half, n2_per_step * d_b), lambda ei, r: (ei, 0, r)),
        out_shape=jax.ShapeDtypeStruct((2 * nblocks, half, FFT_N2 * d_b), F32),
        compiler_params=_cparams(("parallel", "parallel")),
        name="hyena_filter",
    )(fr, deltas, w1.T, b1.reshape(width, 1), w2.T, b2.reshape(width, 1), w3.T, b3.reshape(width, 1), freq.T, w4h)


def _dft_tables():
    n1 = np.arange(FFT_N1)
    f64 = np.exp(-2j * np.pi * np.outer(n1, n1) / FFT_N1)
    half = FFT_N1 // 2
    fr, fi = f64.real, f64.imag
    lhs_data = np.block([[fr[:, :half], -fi[:, :half]], [fi[:, :half], fr[:, :half]]])
    lhs_real = np.concatenate([fr, fi], axis=0)
    gr, gi = fr[:half, :] / FFT_N, -fi[:half, :] / FFT_N
    lhs_inv = np.block([[gr, -gi], [gi, gr]])
    k1 = np.arange(FFT_N1)[:, None, None]
    k2 = np.arange(FFT_N2)[None, :, None]
    n2 = np.arange(FFT_N2)[None, None, :]
    g = np.exp(-2j * np.pi * (n2 * (k1 + FFT_N1 * k2) % FFT_N) / FFT_N)
    gfwd = np.concatenate([np.concatenate([g.real, -g.imag], axis=2),
                           np.concatenate([g.imag, g.real], axis=2)], axis=1)
    ht = np.conj(np.transpose(g, (0, 2, 1)))
    ginv = np.concatenate([np.concatenate([ht.real, -ht.imag], axis=2),
                           np.concatenate([ht.imag, ht.real], axis=2)], axis=1)
    as32 = lambda a: np.asarray(a, dtype=np.float32)
    return as32(lhs_data), as32(lhs_real), as32(lhs_inv), as32(gfwd), as32(ginv)


def _split(x):
    hi = x.astype(BF16)
    lo = (x - hi.astype(F32)).astype(BF16)
    return hi, lo


def _dot3(a_hi, a_lo, b):
    if b.dtype == BF16:
        return jnp.dot(a_hi, b, preferred_element_type=F32) + jnp.dot(a_lo, b, preferred_element_type=F32)
    b_hi, b_lo = _split(b)
    acc = jnp.dot(a_hi, b_hi, preferred_element_type=F32)
    acc = acc + jnp.dot(a_hi, b_lo, preferred_element_type=F32)
    acc = acc + jnp.dot(a_lo, b_hi, preferred_element_type=F32)
    return acc


SPECTRA_DTYPE = BF16
FFT_TN2 = 16


def _rows_to_lanes(x):
    xt = jnp.swapaxes(x, 0, 1)
    return jnp.concatenate([xt[i] for i in range(xt.shape[0])], axis=1)


def _lanes_to_rows(x, s):
    c = x.shape[1] // s
    return jnp.swapaxes(jnp.stack([x[:, i * c:(i + 1) * c] for i in range(s)], axis=0), 0, 1)


def _fft1_kernel(lh_ref, ll_ref, a_ref, b_ref, o_ref, *, zero_lanes, natural):
    if natural:
        a = _rows_to_lanes(a_ref[...])
        b = _rows_to_lanes(b_ref[...])
    else:
        a = a_ref[...]
        b = b_ref[...]
    if zero_lanes:
        row = lax.broadcasted_iota(I32, b.shape, 0)
        lane = lax.broadcasted_iota(I32, b.shape, 1) + pl.program_id(2) * b.shape[1]
        b = jnp.where(jnp.logical_and(row == 0, lane < zero_lanes), 0.0, b)
    rhs = jnp.concatenate([a, b], axis=0).astype(SPECTRA_DTYPE)
    res = _dot3(lh_ref[...], ll_ref[...], rhs)
    o_ref[...] = _lanes_to_rows(res, FFT_TN2).reshape(o_ref.shape).astype(o_ref.dtype)


def _fft1(lhs, x, a_of, b_of, g, p, d_b, natural, zero_lanes=0):
    half = FFT_N1 // 2
    lh, ll = _split(jnp.asarray(lhs))
    kern = functools.partial(_fft1_kernel, zero_lanes=zero_lanes, natural=natural)
    if natural:
        blk = lambda of: pl.BlockSpec((None, None, half, FFT_TN2, d_b), lambda q, j, t: of(q, j) + (0, t, 0))
    else:
        blk = lambda of: pl.BlockSpec((None, None, half, FFT_TN2 * d_b), lambda q, j, t: of(q, j) + (0, t))
    return pl.pallas_call(
        kern,
        grid=(g, p, FFT_N2 // FFT_TN2),
        in_specs=[
            pl.BlockSpec(lhs.shape, lambda q, j, t: (0, 0)),
            pl.BlockSpec(lhs.shape, lambda q, j, t: (0, 0)),
            blk(a_of), blk(b_of),
        ],
        out_specs=pl.BlockSpec((None, None, 2, FFT_N1, FFT_TN2, d_b), lambda q, j, t: (q, j, 0, 0, t, 0)),
        out_shape=jax.ShapeDtypeStruct((g, p, 2, FFT_N1, FFT_N2, d_b), SPECTRA_DTYPE),
        compiler_params=_cparams(("parallel", "parallel", "parallel")),
        name="fft1",
    )(lh, ll, x, x)


def _fft3_kernel(lh_ref, ll_ref, w_ref, o_ref):
    half = FFT_N1 // 2
    c = w_ref.shape[-1]
    w = _rows_to_lanes(w_ref[...].astype(F32).reshape(2 * FFT_N1, FFT_TN2, c)).astype(w_ref.dtype)
    res = _dot3(lh_ref[...], ll_ref[...], w)
    o_ref[0] = _lanes_to_rows(res[:half], FFT_TN2)
    o_ref[1] = _lanes_to_rows(res[half:], FFT_TN2)


def _fft3(lhs, w6):
    g, p, _, _, _, d_b = w6.shape
    half = FFT_N1 // 2
    lh, ll = _split(jnp.asarray(lhs))
    return pl.pallas_call(
        _fft3_kernel,
        grid=(g, p, FFT_N2 // FFT_TN2),
        in_specs=[
            pl.BlockSpec(lhs.shape, lambda q, j, t: (0, 0)),
            pl.BlockSpec(lhs.shape, lambda q, j, t: (0, 0)),
            pl.BlockSpec((None, None, 2, FFT_N1, FFT_TN2, d_b), lambda q, j, t: (q, j, 0, 0, t, 0)),
        ],
        out_specs=pl.BlockSpec((None, 2, None, half, FFT_TN2, d_b), lambda q, j, t: (q, 0, j, 0, t, 0)),
        out_shape=jax.ShapeDtypeStruct((g, 2, p, half, FFT_N2, d_b), F32),
        compiler_params=_cparams(("parallel", "parallel", "parallel")),
        name="fft3",
    )(lh, ll, w6)


def _fft2_filter_kernel(gh_ref, gl_ref, x_ref, o_ref):
    ct = x_ref.shape[-1]
    for k in range(x_ref.shape[1]):
        z = _dot3(gh_ref[k], gl_ref[k], x_ref[:, k].reshape(2 * FFT_N2, ct))
        o_ref[:, k] = z.reshape(2, FFT_N2, ct).astype(o_ref.dtype)


def _fft2_filter(gfwd_hl, x1f, ct=256, kb=8):
    nd, _, _, _, d_b = x1f.shape
    gh, gl = gfwd_hl
    gspec = pl.BlockSpec((kb, 2 * FFT_N2, 2 * FFT_N2), lambda k, d, c: (k, 0, 0))
    xspec = pl.BlockSpec((None, 2, kb, FFT_N2, ct), lambda k, d, c: (d, 0, k, 0, c))
    return pl.pallas_call(
        _fft2_filter_kernel,
        grid=(FFT_N1 // kb, nd, d_b // ct),
        in_specs=[gspec, gspec, xspec],
        out_specs=xspec,
        out_shape=jax.ShapeDtypeStruct(x1f.shape, SPECTRA_DTYPE),
        compiler_params=_cparams(("parallel", "parallel", "parallel")),
        name="fft2_filter",
    )(gh, gl, x1f)


def _fft2_mix_kernel(gh_ref, gl_ref, ih_ref, il_ref, x_ref, k_ref, o_ref, *, nblocks):
    ct = x_ref.shape[-1]
    for k in range(x_ref.shape[2]):
        zs = []
        for j in range(nblocks):
            z = _dot3(gh_ref[k], gl_ref[k], x_ref[j, :, k].reshape(2 * FFT_N2, ct))
            zs.append((z[:FFT_N2], z[FFT_N2:]))
        for i in range(nblocks):
            yr = jnp.zeros((FFT_N2, ct), F32)
            yi = jnp.zeros((FFT_N2, ct), F32)
            for j in range(nblocks):
                d = i - j + nblocks - 1
                kr, ki = k_ref[d, 0, k].astype(F32), k_ref[d, 1, k].astype(F32)
                zr, zi = zs[j]
                yr = yr + kr * zr - ki * zi
                yi = yi + kr * zi + ki * zr
            w = _dot3(ih_ref[k], il_ref[k], jnp.concatenate([yr, yi], axis=0).astype(SPECTRA_DTYPE))
            o_ref[i, :, k] = w.reshape(2, FFT_N2, ct).astype(o_ref.dtype)


def _fft2_mix(gfwd_hl, ginv_hl, x1, kspec, ct=256):
    g, p, _, _, _, d_b = x1.shape
    nd = kspec.shape[0]
    kb = max(1, 8 // p)
    gspec = pl.BlockSpec((kb, 2 * FFT_N2, 2 * FFT_N2), lambda k, q, c: (k, 0, 0))
    xspec = pl.BlockSpec((None, p, 2, kb, FFT_N2, ct), lambda k, q, c: (q, 0, 0, k, 0, c))
    kern = functools.partial(_fft2_mix_kernel, nblocks=p)
    return pl.pallas_call(
        kern,
        grid=(FFT_N1 // kb, g, d_b // ct),
        in_specs=[gspec, gspec, gspec, gspec, xspec,
                  pl.BlockSpec((nd, 2, kb, FFT_N2, ct), lambda k, q, c: (0, 0, k, 0, c))],
        out_specs=xspec,
        out_shape=jax.ShapeDtypeStruct(x1.shape, SPECTRA_DTYPE),
        compiler_params=_cparams(("parallel", "parallel", "parallel")),
        name="fft2_mix",
    )(gfwd_hl[0], gfwd_hl[1], ginv_hl[0], ginv_hl[1], x1, kspec)


def _hyena_long_conv(g3, w1, b1, w2, b2, w3, b3, freq, w4):
    bsz, seq_len, d_b = g3.shape
    nblocks = seq_len // CONV_BLOCK
    half = FFT_N1 // 2
    lhs_data, lhs_real, lhs_inv, gfwd, ginv = _dft_tables()
    gfwd_hl = _split(jnp.asarray(gfwd))
    ginv_hl = _split(jnp.asarray(ginv))
    kext = _hyena_filter_blocks(seq_len, w1, b1, w2, b2, w3, b3, freq, w4)
    nd = 2 * nblocks - 1
    k1f = _fft1(lhs_real, kext[None], lambda q, j: (0, q + 1), lambda q, j: (0, q), nd, 1, d_b,
                natural=False, zero_lanes=d_b)
    kspec = _fft2_filter(gfwd_hl, k1f.reshape(nd, 2, FFT_N1, FFT_N2, d_b))
    g5 = g3.reshape(bsz, nblocks, half, FFT_N2, d_b)
    x1 = _fft1(lhs_data, g5, lambda q, j: (2 * q, j), lambda q, j: (2 * q + 1, j), bsz // 2, nblocks, d_b,
               natural=True)
    wmix = _fft2_mix(gfwd_hl, ginv_hl, x1, kspec)
    y6 = _fft3(lhs_inv, wmix)
    return y6.reshape(bsz, seq_len, d_b)


def _postmix_kernel(x_ref, yan_ref, x0_ref, g_ref, yc_ref, bias_ref, mgb_ref, wo_ref, n2_ref, x1_ref, xn_ref):
    d_a = yan_ref.shape[1]
    gg = g_ref[...]
    yb = x0_ref[...] * (yc_ref[...] + gg * bias_ref[...])
    ybn = _rms(yb, mgb_ref[...]).astype(BF16)
    y = jnp.dot(yan_ref[...], wo_ref[0:d_a, :], preferred_element_type=F32)
    y = y + jnp.dot(ybn, wo_ref[d_a:, :], preferred_element_type=F32)
    x1 = x_ref[...] + y
    x1_ref[...] = x1
    xn_ref[...] = _rms(x1, n2_ref[...]).astype(xn_ref.dtype)


def _postmix(x, yan, x0, g, yconv, bias, mix_g_b, w_out, norm2, tm=512):
    t, d = x.shape
    d_a = yan.shape[1]
    d_b = x0.shape[1]
    assert t % tm == 0, t
    row = lambda w: pl.BlockSpec((tm, w), lambda i: (i, 0))
    const = lambda *shape: pl.BlockSpec(shape, lambda i: (0,) * len(shape), pipeline_mode=pl.Buffered(1))
    return pl.pallas_call(
        _postmix_kernel,
        grid=(t // tm,),
        in_specs=[row(d), row(d_a), row(d_b), row(d_b), row(d_b), const(1, d_b), const(1, d_b),
                  const(d_a + d_b, d), const(1, d)],
        out_specs=[row(d), row(d)],
        out_shape=[jax.ShapeDtypeStruct((t, d), F32), jax.ShapeDtypeStruct((t, d), BF16)],
        compiler_params=_cparams(("parallel",)),
        name="postmix",
    )(x, yan, x0, g, yconv, bias, mix_g_b, w_out, norm2)


def _staircase():
    return [(i, j) for i in range(PEER_TOPK) for j in range(PEER_TOPK) if (i + 1) * (j + 1) <= PEER_TOPK]


def _stream_argmax(val_ref, tag_ref, nrows, prev, slab):
    neg = np.float32(-np.inf)
    best_v = jnp.full(slab, neg, F32)
    best_i = jnp.zeros(slab, I32)
    best_t = jnp.zeros(slab, I32)
    for n in range(nrows):
        v = jnp.where(prev == n, neg, val_ref[n])
        val_ref[n] = v
        better = v > best_v
        best_v = jnp.maximum(v, best_v)
        best_i = jnp.where(better, n, best_i)
        if tag_ref is not None:
            best_t = jnp.where(better, tag_ref[n], best_t)
    return best_v, best_i, best_t


def _scores_kernel(xn_ref, wqt_ref, kbd_ref, s_ref, *, heads):
    tm = xn_ref.shape[0]
    hk = heads * HALF_KEY
    q_t = lax.dot_general(wqt_ref[...], xn_ref[...], (((1,), (1,)), ((), ())),
                          preferred_element_type=F32).astype(BF16)
    for half in range(2):
        s_t = jnp.dot(kbd_ref[half], q_t[half * hk:(half + 1) * hk], preferred_element_type=F32)
        s_ref[half] = s_t.reshape(N_KEYS, heads, tm)


def _peer_scores(xn, wqt, kbd, heads, tm=512):
    t, d = xn.shape
    assert t % tm == 0 and heads == V7X_SUBLANES, (t, heads)
    const = lambda *shape: pl.BlockSpec(shape, lambda i: (0,) * len(shape), pipeline_mode=pl.Buffered(1))
    kern = functools.partial(_scores_kernel, heads=heads)
    return pl.pallas_call(
        kern,
        grid=(t // tm,),
        in_specs=[pl.BlockSpec((tm, d), lambda i: (i, 0)), const(*wqt.shape), const(*kbd.shape)],
        out_specs=pl.BlockSpec((2, N_KEYS, heads, tm), lambda i: (0, 0, 0, i)),
        out_shape=jax.ShapeDtypeStruct((2, N_KEYS, heads, t), F32),
        compiler_params=_cparams(("parallel",)),
        name="peer_scores",
    )(xn, wqt, kbd)


SELECT_BLOCK = 128


def _peer_a_select_kernel(xn_ref, dlo_ref, dhi_ref, s_ref, a_ref, e_ref, gt_ref, s0_s, s1_s, c_s, ce_s, b_s, es_s,
                          *, heads):
    xn = xn_ref[...]
    a_lo = jnp.dot(xn, dlo_ref[...], preferred_element_type=F32)
    a_hi = jnp.dot(xn, dhi_ref[...], preferred_element_type=F32)
    a_ref[...] = pltpu.pack_elementwise([a_lo, a_hi], packed_dtype=BF16)

    slab = (heads, s_ref.shape[-1])
    no_pick = jnp.full(slab, -1, I32)
    half_s = (s0_s, s1_s)
    vals, idxs, prevs = ([], []), ([], []), [no_pick, no_pick]
    for half in range(2):
        half_s[half][...] = s_ref[half]
    for r in range(PEER_TOPK):
        for half in range(2):
            m, prevs[half], _ = _stream_argmax(half_s[half], None, N_KEYS, prevs[half], slab)
            vals[half].append(m)
            idxs[half].append(prevs[half])
    cands = _staircase()
    for p, (i, j) in enumerate(cands):
        c_s[p] = vals[0][i] + vals[1][j]
        ce_s[p] = idxs[0][i] * N_KEYS + idxs[1][j]
    prev = no_pick
    for r in range(PEER_TOPK):
        m, prev, e = _stream_argmax(c_s, ce_s, len(cands), prev, slab)
        es_s[r] = e
        b_s[r] = m
    best = b_s[...]
    ex = jnp.exp(best - jnp.max(best, axis=0, keepdims=True))
    gate = ex / jnp.sum(ex, axis=0, keepdims=True)
    gt_ref[...] = gate.reshape(PEER_TOPK * heads, slab[1]).T
    e_ref[...] = es_s[...].reshape(PEER_TOPK * heads, slab[1]).T


def _peer_a_select(xn, down_t, scores, heads, tm=1024, tn=1024):
    t, d = xn.shape
    n = down_t.shape[1]
    tb = SELECT_BLOCK
    nsel = PEER_TOPK * heads
    ncand = len(_staircase())
    ni = t // tm
    nj = n // (2 * tn)
    assert t % tm == 0 and n % (2 * tn) == 0 and heads == V7X_SUBLANES, (t, n, heads)
    assert nj * ni * tb == t, "one selection block per grid step must cover all tokens"
    blk = lambda j, i: j * ni + i
    kern = functools.partial(_peer_a_select_kernel, heads=heads)
    return pl.pallas_call(
        kern,
        grid=(nj, ni),
        in_specs=[pl.BlockSpec((tm, d), lambda j, i: (i, 0)),
                  pl.BlockSpec((d, tn), lambda j, i: (0, j)), pl.BlockSpec((d, tn), lambda j, i: (0, j + nj)),
                  pl.BlockSpec((2, N_KEYS, heads, tb), lambda j, i: (0, 0, 0, blk(j, i)))],
        out_specs=[pl.BlockSpec((tm, tn), lambda j, i: (i, j)),
                   pl.BlockSpec((tb, nsel), lambda j, i: (blk(j, i), 0)),
                   pl.BlockSpec((tb, nsel), lambda j, i: (blk(j, i), 0))],
        out_shape=[jax.ShapeDtypeStruct((t, n // 2), U32), jax.ShapeDtypeStruct((t, nsel), I32),
                   jax.ShapeDtypeStruct((t, nsel), F32)],
        scratch_shapes=[
            pltpu.VMEM((N_KEYS, heads, tb), F32),
            pltpu.VMEM((N_KEYS, heads, tb), F32),
            pltpu.VMEM((ncand, heads, tb), F32),
            pltpu.VMEM((ncand, heads, tb), I32),
            pltpu.VMEM((PEER_TOPK, heads, tb), F32),
            pltpu.VMEM((PEER_TOPK, heads, tb), I32),
        ],
        compiler_params=_cparams(("parallel", "parallel")),
        name="peer_a_select",
    )(xn, down_t, down_t, scores)


PEER_B_GROUP = 16


def _peer_b_pick(a_ref, e_ref, gt_ref, rows):
    e = e_ref[rows, :]
    ai = lax.shift_right_logical(e, KEY_SHIFT)
    bi = e & (N_KEYS - 1)
    a_word = ai & (N_KEYS // 2 - 1)
    accs = [jnp.zeros(e.shape, U32) for _ in range(4)]
    for a in range(N_KEYS // 2):
        blk = a_ref[rows, a * N_KEYS:(a + 1) * N_KEYS]
        accs[a % 4] = jnp.where(a_word == a, jnp.take_along_axis(blk, bi, axis=1), accs[a % 4])
    word = (accs[0] | accs[1]) | (accs[2] | accs[3])
    lo = pltpu.unpack_elementwise(word, index=0, packed_dtype=BF16, unpacked_dtype=F32)
    hi = pltpu.unpack_elementwise(word, index=1, packed_dtype=BF16, unpacked_dtype=F32)
    picked = jnp.where(ai >= N_KEYS // 2, hi, lo)
    return gt_ref[rows, :] * _gelu(picked), e


def _peer_b_scatter(m_ref, rows, w, e):
    sub = lax.broadcasted_iota(I32, (e.shape[0], N_KEYS, e.shape[1]), 1)
    ai = lax.shift_right_logical(e, KEY_SHIFT)
    bi = e & (N_KEYS - 1)
    w1t = jnp.where(sub == ai[:, None, :], w[:, None, :], 0.0).astype(BF16)
    e2t = jnp.where(sub == bi[:, None, :], 1.0, 0.0).astype(BF16)
    m3 = jnp.einsum("cas,cbs->cab", w1t, e2t, preferred_element_type=F32)
    mt = jnp.swapaxes(m3, 0, 1).astype(m_ref.dtype)
    for a in range(N_KEYS):
        m_ref[rows, a * N_KEYS:(a + 1) * N_KEYS] = mt[a]


def _peer_b_kernel(a_ref, e_ref, gt_ref, m_ref):
    grp = PEER_B_GROUP
    ngrp = a_ref.shape[0] // grp

    def rows_of(gi):
        return pl.ds(pl.multiple_of(gi * grp, grp), grp)

    def step(gi, carry):
        nxt = _peer_b_pick(a_ref, e_ref, gt_ref, rows_of(gi))
        _peer_b_scatter(m_ref, rows_of(gi - 1), *carry)
        return nxt

    last = lax.fori_loop(1, ngrp, step, _peer_b_pick(a_ref, e_ref, gt_ref, rows_of(0)))
    _peer_b_scatter(m_ref, rows_of(ngrp - 1), *last)


def _peer_b(a2, eidx, gate, tc=256):
    t = a2.shape[0]
    n = 2 * a2.shape[1]
    nsel = eidx.shape[1]
    assert t % tc == 0 and tc % PEER_B_GROUP == 0 and n == N_KEYS * N_KEYS and nsel == N_KEYS, (t, n, nsel)
    return pl.pallas_call(
        _peer_b_kernel,
        grid=(t // tc,),
        in_specs=[pl.BlockSpec((tc, n // 2), lambda i: (i, 0)), pl.BlockSpec((tc, nsel), lambda i: (i, 0)),
                  pl.BlockSpec((tc, nsel), lambda i: (i, 0))],
        out_specs=pl.BlockSpec((tc, n), lambda i: (i, 0)),
        out_shape=jax.ShapeDtypeStruct((t, n), BF16),
        compiler_params=_cparams(("parallel",)),
        name="peer_b",
    )(a2, eidx, gate)


def _peer_c_kernel(m_ref, up_ref, x1_ref, fn_ref, o_ref, acc_s):
    k = pl.program_id(1)

    @pl.when(k == 0)
    def _():
        acc_s[...] = jnp.zeros_like(acc_s)

    acc_s[...] += jnp.dot(m_ref[...], up_ref[...], preferred_element_type=F32)

    @pl.when(k == pl.num_programs(1) - 1)
    def _():
        o_ref[...] = _rms(x1_ref[...] + acc_s[...], fn_ref[...])


def _peer_c(m, up, x1, final_norm, tm=1024, tk=1024):
    t, n = m.shape
    d = up.shape[1]
    assert t % tm == 0 and n % tk == 0, (t, n)
    return pl.pallas_call(
        _peer_c_kernel,
        grid=(t // tm, n // tk),
        in_specs=[pl.BlockSpec((tm, tk), lambda i, k: (i, k)), pl.BlockSpec((tk, d), lambda i, k: (k, 0)),
                  pl.BlockSpec((tm, d), lambda i, k: (i, 0), pipeline_mode=pl.Buffered(1)),
                  pl.BlockSpec((1, d), lambda i, k: (0, 0))],
        out_specs=pl.BlockSpec((tm, d), lambda i, k: (i, 0)),
        out_shape=jax.ShapeDtypeStruct((t, d), F32),
        scratch_shapes=[pltpu.VMEM((tm, d), F32)],
        compiler_params=_cparams(("parallel", "arbitrary")),
        name="peer_c",
    )(m, up, x1, final_norm)


def _prep_weights(norm1, w_in, a_ln_g, a_ln_b, a_ws, a_bs, b_conv_w, b_conv_b, b_bias, mix_norm, w_out, norm2,
                  peer_wq, peer_k1, peer_k2, peer_down, peer_up, final_norm):
    d = w_in.shape[0]
    d_a = a_ln_g.shape[0]
    d_b = b_bias.shape[0]
    heads, n_keys, half_key = peer_k1.shape
    row = lambda v: v.reshape(1, -1).astype(F32)
    w = dict(
        norm1=row(norm1), norm2=row(norm2), final_norm=row(final_norm),
        w_u=w_in[:, :d_a].astype(BF16), w_v=w_in[:, d_a:2 * d_a].astype(BF16),
        w_b=w_in[:, 2 * d_a:].reshape(d, 3, d_b).transpose(1, 0, 2).astype(BF16),
        ln_g=row(a_ln_g), ln_b=row(a_ln_b),
        ws=a_ws.astype(BF16),
        bsb=jnp.broadcast_to(a_bs[:, :, None], a_bs.shape + (A_HEAD_DIM,)).astype(F32),
        conv_w=b_conv_w.reshape(3, 3, d_b).transpose(1, 0, 2).astype(F32),
        conv_b=b_conv_b.reshape(3, d_b).astype(F32),
        bias=row(b_bias), mix_g_a=row(mix_norm[:d_a]), mix_g_b=row(mix_norm[d_a:]),
        w_out=w_out.astype(BF16),
        down_t=peer_down.astype(BF16).T, up=peer_up.astype(BF16),
    )
    wq_t = peer_wq.T.reshape(heads, 2, half_key, d).transpose(1, 0, 2, 3).reshape(2 * heads * half_key, d)
    w["wq_t"] = wq_t.astype(BF16)
    eye = jnp.eye(heads, dtype=F32)
    kbd = [jnp.einsum("hnd,hg->nhgd", k, eye).reshape(n_keys * heads, heads * half_key) for k in (peer_k1, peer_k2)]
    w["kbd"] = jnp.stack(kbd).astype(BF16)
    w["heads"] = heads
    return w


def _trunk(x3, w, hf):
    bsz, seq_len, d = x3.shape
    t = bsz * seq_len
    x = x3.reshape(t, d)
    yan, x0, g = _inproj(x, seq_len, w["norm1"], w["w_u"], w["w_v"], w["w_b"], w["ln_g"], w["ln_b"],
                         w["ws"], w["bsb"], w["conv_w"], w["conv_b"], w["mix_g_a"])
    d_b = x0.shape[1]
    yconv = _hyena_long_conv(g.reshape(bsz, seq_len, d_b), *hf).reshape(t, d_b)
    x1, xn = _postmix(x, yan, x0, g, yconv, w["bias"], w["mix_g_b"], w["w_out"], w["norm2"])
    scores = _peer_scores(xn, w["wq_t"], w["kbd"], w["heads"])
    a2, eidx, gate = _peer_a_select(xn, w["down_t"], scores, w["heads"])
    m = _peer_b(a2, eidx, gate)
    out = _peer_c(m, w["up"], x1, w["final_norm"])
    return out.reshape(bsz, seq_len, d)


def kernel(x_prompt, x_sample, norm1, w_in, a_ln_g, a_ln_b, a_ws, a_bs, b_conv_w, b_conv_b, hf_w1, hf_b1, hf_w2, hf_b2, hf_w3, hf_b3, hf_freq, hf_w4, b_bias, mix_norm, w_out, norm2, peer_wq, peer_k1, peer_k2, peer_down, peer_up, final_norm):
    assert norm1.shape[0] == 1, "single-layer trunk"
    w = _prep_weights(norm1[0], w_in[0], a_ln_g[0], a_ln_b[0], a_ws[0], a_bs[0], b_conv_w[0], b_conv_b[0],
                      b_bias[0], mix_norm[0], w_out[0], norm2[0], peer_wq[0], peer_k1[0], peer_k2[0],
                      peer_down[0], peer_up[0], final_norm)
    hf = (hf_w1[0], hf_b1[0].reshape(1, -1), hf_w2[0], hf_b2[0].reshape(1, -1), hf_w3[0], hf_b3[0].reshape(1, -1),
          hf_freq[0], hf_w4[0])
    return (_trunk(x_prompt, w, hf), _trunk(x_sample, w, hf))
```

```python
import functools
import math

import numpy as np
import jax
import jax.numpy as jnp
from jax import lax
from jax.experimental import pallas as pl
from jax.experimental.pallas import tpu as pltpu

F32 = jnp.float32
BF16 = jnp.bfloat16
I32 = jnp.int32
U32 = jnp.uint32

EPS = 1e-6
V7X_SUBLANES = 8
V7X_VMEM_BYTES = 64 * 1024 * 1024
VMEM_LIMIT = V7X_VMEM_BYTES - 8 * 1024 * 1024

CHUNK = 128
A_HEAD_DIM = 128
EMB_DIM = 33
BANDS = (EMB_DIM - 1) // 2
DECAY_TARGET = 1e-2
FAST_DECAY_PCT = 0.3
SLOW_DECAY_PCT = 1.5
FFT_N1 = 64
FFT_N2 = 128
FFT_N = FFT_N1 * FFT_N2
CONV_BLOCK = FFT_N // 2
N_KEYS = 128
KEY_SHIFT = 7
PEER_TOPK = 16
HALF_KEY = 128


def _cparams(sem):
    return pltpu.CompilerParams(dimension_semantics=sem, vmem_limit_bytes=VMEM_LIMIT)


def _rms(xf, g):
    return xf * lax.rsqrt(jnp.mean(xf * xf, axis=-1, keepdims=True) + EPS) * g


def _gelu(x):
    return 0.5 * x * (1.0 + lax.erf(x * np.float32(math.sqrt(0.5))))


def _inproj_kernel(xp_ref, x_ref, xn_ref, n1_ref, wu_ref, wv_ref, wb_ref, lng_ref, lnb_ref,
                   ws_ref, bsb_ref, cw_ref, cb_ref, mga_ref,
                   yan_ref, x0_ref, g_ref, ya_s, zs_s, *, tiles_per_seq, cblk):
    tm = x_ref.shape[0]
    d_a = wu_ref.shape[1]
    d_b = wb_ref.shape[2]
    i = pl.program_id(0)
    not_first = (i % tiles_per_seq != 0).astype(F32)
    not_last = (i % tiles_per_seq != tiles_per_seq - 1).astype(F32)

    xcat = jnp.concatenate([xp_ref[...], x_ref[...], xn_ref[...]], axis=0)
    hcat = _rms(xcat, n1_ref[...])
    h_all = hcat.astype(BF16)
    h = hcat[V7X_SUBLANES:V7X_SUBLANES + tm].astype(BF16)

    u = _gelu(jnp.dot(h, wu_ref[...], preferred_element_type=F32))
    v = _gelu(jnp.dot(h, wv_ref[...], preferred_element_type=F32))
    mu = jnp.mean(v, axis=-1, keepdims=True)
    vc = v - mu
    var = jnp.mean(vc * vc, axis=-1, keepdims=True)
    vb = (vc * lax.rsqrt(var + EPS) * lng_ref[...] + lnb_ref[...]).astype(BF16)
    chunks = [slice(c * CHUNK, (c + 1) * CHUNK) for c in range(tm // CHUNK)]
    for hd in range(d_a // A_HEAD_DIM):
        cols = slice(hd * A_HEAD_DIM, (hd + 1) * A_HEAD_DIM)
        v_head = jnp.concatenate([vb[rows, cols] for rows in chunks], axis=1)
        mixed = jnp.dot(ws_ref[hd], v_head, preferred_element_type=F32)
        for c, rows in enumerate(chunks):
            ya_s[rows, cols] = u[rows, cols] * (mixed[:, c * A_HEAD_DIM:(c + 1) * A_HEAD_DIM] + bsb_ref[hd])
    yan_ref[...] = _rms(ya_s[...], mga_ref[...]).astype(yan_ref.dtype)

    halo = V7X_SUBLANES
    for cb in range(d_b // cblk):
        cols = slice(cb * cblk, (cb + 1) * cblk)
        parts = []
        for p in range(3):
            z = jnp.dot(h_all, wb_ref[p, :, cols], preferred_element_type=F32)
            zs_s[...] = z
            zs_s[0:halo, :] = z[0:halo] * not_first
            zs_s[tm + halo:tm + 2 * halo, :] = z[tm + halo:tm + 2 * halo] * not_last
            w = cw_ref[p]
            zc = (cb_ref[p:p + 1, cols]
                  + zs_s[halo - 1:halo - 1 + tm, :] * w[0:1, cols]
                  + zs_s[halo:halo + tm, :] * w[1:2, cols]
                  + zs_s[halo + 1:halo + 1 + tm, :] * w[2:3, cols])
            parts.append(zc)
        x0_ref[:, cols] = parts[0]
        g_ref[:, cols] = parts[1] * parts[2]


def _inproj(x, seq_len, norm1, w_u, w_v, w_b, ln_g, ln_b, ws, bsb, conv_w, conv_b, mix_g_a, tm=512, cblk=256):
    t, d = x.shape
    d_a = w_u.shape[1]
    d_b = w_b.shape[2]
    assert seq_len % tm == 0 and t % seq_len == 0 and tm % CHUNK == 0 and d_b % cblk == 0, (t, seq_len)
    nblk = tm // V7X_SUBLANES
    last8 = t // V7X_SUBLANES - 1
    const = lambda *shape: pl.BlockSpec(shape, lambda i: (0,) * len(shape), pipeline_mode=pl.Buffered(1))
    kern = functools.partial(_inproj_kernel, tiles_per_seq=seq_len // tm, cblk=cblk)
    return pl.pallas_call(
        kern,
        grid=(t // tm,),
        in_specs=[
            pl.BlockSpec((V7X_SUBLANES, d), lambda i: (jnp.maximum(i * nblk - 1, 0), 0)),
            pl.BlockSpec((tm, d), lambda i: (i, 0)),
            pl.BlockSpec((V7X_SUBLANES, d), lambda i: (jnp.minimum((i + 1) * nblk, last8), 0)),
            const(1, d), const(d, d_a), const(d, d_a), const(3, d, d_b), const(1, d_a), const(1, d_a),
            const(*ws.shape), const(*bsb.shape), const(3, 3, d_b), const(3, d_b), const(1, d_a),
        ],
        out_specs=[
            pl.BlockSpec((tm, d_a), lambda i: (i, 0)),
            pl.BlockSpec((tm, d_b), lambda i: (i, 0)),
            pl.BlockSpec((tm, d_b), lambda i: (i, 0)),
        ],
        out_shape=[
            jax.ShapeDtypeStruct((t, d_a), BF16),
            jax.ShapeDtypeStruct((t, d_b), F32),
            jax.ShapeDtypeStruct((t, d_b), F32),
        ],
        scratch_shapes=[pltpu.VMEM((tm, d_a), F32), pltpu.VMEM((tm + 2 * V7X_SUBLANES, cblk), F32)],
        compiler_params=_cparams(("parallel",)),
        name="inproj",
    )(x, x, x, norm1, w_u, w_v, w_b, ln_g, ln_b, ws, bsb, conv_w, conv_b, mix_g_a)


def _filt_kernel(fr_ref, dl_ref, w1_ref, b1_ref, w2_ref, b2_ref, w3_ref, b3_ref, fq_ref, w4_ref, o_ref,
                 *, seq_len, nblocks, n2_per_step):
    half = FFT_N1 // 2
    tr = half * n2_per_step
    e = pl.program_id(0) - nblocks

    def lags(shape, axis):
        r = lax.broadcasted_iota(I32, shape, axis)
        n2 = pl.program_id(1) * n2_per_step + r // half
        return e * CONV_BLOCK + (r % half) * FFT_N2 + n2

    inv_len = np.float32(max(seq_len - 1, 1))
    pos = jnp.abs(lags((1, tr), 1)).astype(F32)
    t = pos / inv_len
    fw = (np.float32(2.0 * math.pi / seq_len) * pos) * fr_ref[...]
    hi = lax.Precision.HIGHEST
    w1 = w1_ref[...]
    z1 = (w1[:, 0:1] * t
          + jnp.dot(w1[:, 1:1 + BANDS], jnp.cos(fw), precision=hi, preferred_element_type=F32)
          + jnp.dot(w1[:, 1 + BANDS:], -jnp.sin(fw), precision=hi, preferred_element_type=F32))
    fq = fq_ref[...]
    h = jnp.sin(fq[:, 0:1] * (z1 + b1_ref[...]))
    h = jnp.sin(fq[:, 1:2] * (jnp.dot(w2_ref[...], h, precision=hi, preferred_element_type=F32) + b2_ref[...]))
    h = jnp.sin(fq[:, 2:3] * (jnp.dot(w3_ref[...], h, precision=hi, preferred_element_type=F32) + b3_ref[...]))
    h4 = jnp.dot(h.T, w4_ref[...], precision=hi, preferred_element_type=F32)
    lag_col = lags((tr, 1), 0)
    t_col = jnp.abs(lag_col).astype(F32) / inv_len
    window = jnp.exp(-t_col * dl_ref[...])
    res = jnp.where(jnp.abs(lag_col) <= seq_len - 1, h4 * window, 0.0)
    d_b = res.shape[1]
    for i in range(n2_per_step):
        o_ref[:, i * d_b:(i + 1) * d_b] = res[i * half:(i + 1) * half]


def _hyena_filter_blocks(seq_len, w1, b1, w2, b2, w3, b3, freq, w4, n2_per_step=16):
    nblocks = seq_len // CONV_BLOCK
    width = w1.shape[1]
    d_b = w4.shape[1] // 2
    fr = jnp.asarray(np.linspace(1e-4, BANDS - 1, BANDS, dtype=np.float32)[:, None])
    min_decay = math.log(DECAY_TARGET) / SLOW_DECAY_PCT
    max_decay = math.log(DECAY_TARGET) / FAST_DECAY_PCT
    deltas = jnp.asarray(np.abs(np.linspace(min_decay, max_decay, d_b, dtype=np.float32))[None, :])
    w4h = w4.reshape(width, 2, d_b).transpose(1, 0, 2)
    const = lambda *shape: pl.BlockSpec(shape, lambda a, b: (0,) * len(shape))
    half = FFT_N1 // 2
    kern = functools.partial(_filt_kernel, seq_len=seq_len, nblocks=nblocks, n2_per_step=n2_per_step)
    return pl.pallas_call(
        kern,
        grid=(2 * nblocks, FFT_N2 // n2_per_step),
        in_specs=[
            const(BANDS, 1), const(1, d_b), const(width, EMB_DIM), const(width, 1), const(width, width),
            const(width, 1), const(width, width), const(width, 1), const(width, 3),
            pl.BlockSpec((None, width, d_b), lambda ei, r: (jnp.where(ei >= nblocks, 0, 1), 0, 0)),
        ],
        out_specs=pl.BlockSpec((None, half, n2_per_step * d_b), lambda ei, r: (ei, 0, r)),
        out_shape=jax.ShapeDtypeStruct((2 * nblocks, half, FFT_N2 * d_b), F32),
        compiler_params=_cparams(("parallel", "parallel")),
        name="hyena_filter",
    )(fr, deltas, w1.T, b1.reshape(width, 1), w2.T, b2.reshape(width, 1), w3.T, b3.reshape(width, 1), freq.T, w4h)


def _dft_tables():
    n1 = np.arange(FFT_N1)
    f64 = np.exp(-2j * np.pi * np.outer(n1, n1) / FFT_N1)
    half = FFT_N1 // 2
    fr, fi = f64.real, f64.imag
    lhs_data = np.block([[fr[:, :half], -fi[:, :half]], [fi[:, :half], fr[:, :half]]])
    lhs_real = np.concatenate([fr, fi], axis=0)
    gr, gi = fr[:half, :] / FFT_N, -fi[:half, :] / FFT_N
    lhs_inv = np.block([[gr, -gi], [gi, gr]])
    k1 = np.arange(FFT_N1)[:, None, None]
    k2 = np.arange(FFT_N2)[None, :, None]
    n2 = np.arange(FFT_N2)[None, None, :]
    g = np.exp(-2j * np.pi * (n2 * (k1 + FFT_N1 * k2) % FFT_N) / FFT_N)
    gfwd = np.concatenate([np.concatenate([g.real, -g.imag], axis=2),
                           np.concatenate([g.imag, g.real], axis=2)], axis=1)
    ht = np.conj(np.transpose(g, (0, 2, 1)))
    ginv = np.concatenate([np.concatenate([ht.real, -ht.imag], axis=2),
                           np.concatenate([ht.imag, ht.real], axis=2)], axis=1)
    as32 = lambda a: np.asarray(a, dtype=np.float32)
    return as32(lhs_data), as32(lhs_real), as32(lhs_inv), as32(gfwd), as32(ginv)


def _split(x):
    hi = x.astype(BF16)
    lo = (x - hi.astype(F32)).astype(BF16)
    return hi, lo


def _dot3(a_hi, a_lo, b):
    if b.dtype == BF16:
        return jnp.dot(a_hi, b, preferred_element_type=F32) + jnp.dot(a_lo, b, preferred_element_type=F32)
    b_hi, b_lo = _split(b)
    acc = jnp.dot(a_hi, b_hi, preferred_element_type=F32)
    acc = acc + jnp.dot(a_hi, b_lo, preferred_element_type=F32)
    acc = acc + jnp.dot(a_lo, b_hi, preferred_element_type=F32)
    return acc


SPECTRA_DTYPE = BF16
FFT_TN2 = 16


def _rows_to_lanes(x):
    xt = jnp.swapaxes(x, 0, 1)
    return jnp.concatenate([xt[i] for i in range(xt.shape[0])], axis=1)


def _lanes_to_rows(x, s):
    c = x.shape[1] // s
    return jnp.swapaxes(jnp.stack([x[:, i * c:(i + 1) * c] for i in range(s)], axis=0), 0, 1)


def _fft1_kernel(lh_ref, ll_ref, a_ref, b_ref, o_ref, *, zero_lanes, natural):
    if natural:
        a = _rows_to_lanes(a_ref[...])
        b = _rows_to_lanes(b_ref[...])
    else:
        a = a_ref[...]
        b = b_ref[...]
    if zero_lanes:
        row = lax.broadcasted_iota(I32, b.shape, 0)
        lane = lax.broadcasted_iota(I32, b.shape, 1) + pl.program_id(2) * b.shape[1]
        b = jnp.where(jnp.logical_and(row == 0, lane < zero_lanes), 0.0, b)
    rhs = jnp.concatenate([a, b], axis=0).astype(SPECTRA_DTYPE)
    res = _dot3(lh_ref[...], ll_ref[...], rhs)
    o_ref[...] = _lanes_to_rows(res, FFT_TN2).reshape(o_ref.shape).astype(o_ref.dtype)


def _fft1(lhs, x, a_of, b_of, g, p, d_b, natural, zero_lanes=0):
    half = FFT_N1 // 2
    lh, ll = _split(jnp.asarray(lhs))
    kern = functools.partial(_fft1_kernel, zero_lanes=zero_lanes, natural=natural)
    if natural:
        blk = lambda of: pl.BlockSpec((None, None, half, FFT_TN2, d_b), lambda q, j, t: of(q, j) + (0, t, 0))
    else:
        blk = lambda of: pl.BlockSpec((None, None, half, FFT_TN2 * d_b), lambda q, j, t: of(q, j) + (0, t))
    return pl.pallas_call(
        kern,
        grid=(g, p, FFT_N2 // FFT_TN2),
        in_specs=[
            pl.BlockSpec(lhs.shape, lambda q, j, t: (0, 0)),
            pl.BlockSpec(lhs.shape, lambda q, j, t: (0, 0)),
            blk(a_of), blk(b_of),
        ],
        out_specs=pl.BlockSpec((None, None, 2, FFT_N1, FFT_TN2, d_b), lambda q, j, t: (q, j, 0, 0, t, 0)),
        out_shape=jax.ShapeDtypeStruct((g, p, 2, FFT_N1, FFT_N2, d_b), SPECTRA_DTYPE),
        compiler_params=_cparams(("parallel", "parallel", "parallel")),
        name="fft1",
    )(lh, ll, x, x)


def _fft3_kernel(lh_ref, ll_ref, w_ref, o_ref):
    half = FFT_N1 // 2
    c = w_ref.shape[-1]
    w = _rows_to_lanes(w_ref[...].astype(F32).reshape(2 * FFT_N1, FFT_TN2, c)).astype(w_ref.dtype)
    res = _dot3(lh_ref[...], ll_ref[...], w)
    o_ref[0] = _lanes_to_rows(res[:half], FFT_TN2)
    o_ref[1] = _lanes_to_rows(res[half:], FFT_TN2)


def _fft3(lhs, w6):
    g, p, _, _, _, d_b = w6.shape
    half = FFT_N1 // 2
    lh, ll = _split(jnp.asarray(lhs))
    return pl.pallas_call(
        _fft3_kernel,
        grid=(g, p, FFT_N2 // FFT_TN2),
        in_specs=[
            pl.BlockSpec(lhs.shape, lambda q, j, t: (0, 0)),
            pl.BlockSpec(lhs.shape, lambda q, j, t: (0, 0)),
            pl.BlockSpec((None, None, 2, FFT_N1, FFT_TN2, d_b), lambda q, j, t: (q, j, 0, 0, t, 0)),
        ],
        out_specs=pl.BlockSpec((None, 2, None, half, FFT_TN2, d_b), lambda q, j, t: (q, 0, j, 0, t, 0)),
        out_shape=jax.ShapeDtypeStruct((g, 2, p, half, FFT_N2, d_b), F32),
        compiler_params=_cparams(("parallel", "parallel", "parallel")),
        name="fft3",
    )(lh, ll, w6)


def _fft2_filter_kernel(gh_ref, gl_ref, x_ref, o_ref):
    ct = x_ref.shape[-1]
    for k in range(x_ref.shape[1]):
        z = _dot3(gh_ref[k], gl_ref[k], x_ref[:, k].reshape(2 * FFT_N2, ct))
        o_ref[:, k] = z.reshape(2, FFT_N2, ct).astype(o_ref.dtype)


def _fft2_filter(gfwd_hl, x1f, ct=256, kb=8):
    nd, _, _, _, d_b = x1f.shape
    gh, gl = gfwd_hl
    gspec = pl.BlockSpec((kb, 2 * FFT_N2, 2 * FFT_N2), lambda k, d, c: (k, 0, 0))
    xspec = pl.BlockSpec((None, 2, kb, FFT_N2, ct), lambda k, d, c: (d, 0, k, 0, c))
    return pl.pallas_call(
        _fft2_filter_kernel,
        grid=(FFT_N1 // kb, nd, d_b // ct),
        in_specs=[gspec, gspec, xspec],
        out_specs=xspec,
        out_shape=jax.ShapeDtypeStruct(x1f.shape, SPECTRA_DTYPE),
        compiler_params=_cparams(("parallel", "parallel", "parallel")),
        name="fft2_filter",
    )(gh, gl, x1f)


def _fft2_mix_kernel(gh_ref, gl_ref, ih_ref, il_ref, x_ref, k_ref, o_ref, *, nblocks):
    ct = x_ref.shape[-1]
    for k in range(x_ref.shape[2]):
        zs = []
        for j in range(nblocks):
            z = _dot3(gh_ref[k], gl_ref[k], x_ref[j, :, k].reshape(2 * FFT_N2, ct))
            zs.append((z[:FFT_N2], z[FFT_N2:]))
        for i in range(nblocks):
            yr = jnp.zeros((FFT_N2, ct), F32)
            yi = jnp.zeros((FFT_N2, ct), F32)
            for j in range(nblocks):
                d = i - j + nblocks - 1
                kr, ki = k_ref[d, 0, k].astype(F32), k_ref[d, 1, k].astype(F32)
                zr, zi = zs[j]
                yr = yr + kr * zr - ki * zi
                yi = yi + kr * zi + ki * zr
            w = _dot3(ih_ref[k], il_ref[k], jnp.concatenate([yr, yi], axis=0).astype(SPECTRA_DTYPE))
            o_ref[i, :, k] = w.reshape(2, FFT_N2, ct).astype(o_ref.dtype)


def _fft2_mix(gfwd_hl, ginv_hl, x1, kspec, ct=256):
    g, p, _, _, _, d_b = x1.shape
    nd = kspec.shape[0]
    kb = max(1, 8 // p)
    gspec = pl.BlockSpec((kb, 2 * FFT_N2, 2 * FFT_N2), lambda k, q, c: (k, 0, 0))
    xspec = pl.BlockSpec((None, p, 2, kb, FFT_N2, ct), lambda k, q, c: (q, 0, 0, k, 0, c))
    kern = functools.partial(_fft2_mix_kernel, nblocks=p)
    return pl.pallas_call(
        kern,
        grid=(FFT_N1 // kb, g, d_b // ct),
        in_specs=[gspec, gspec, gspec, gspec, xspec,
                  pl.BlockSpec((nd, 2, kb, FFT_N2, ct), lambda k, q, c: (0, 0, k, 0, c))],
        out_specs=xspec,
        out_shape=jax.ShapeDtypeStruct(x1.shape, SPECTRA_DTYPE),
        compiler_params=_cparams(("parallel", "parallel", "parallel")),
        name="fft2_mix",
    )(gfwd_hl[0], gfwd_hl[1], ginv_hl[0], ginv_hl[1], x1, kspec)


def _hyena_long_conv(g3, w1, b1, w2, b2, w3, b3, freq, w4):
    bsz, seq_len, d_b = g3.shape
    nblocks = seq_len // CONV_BLOCK
    half = FFT_N1 // 2
    lhs_data, lhs_real, lhs_inv, gfwd, ginv = _dft_tables()
    gfwd_hl = _split(jnp.asarray(gfwd))
    ginv_hl = _split(jnp.asarray(ginv))
    kext = _hyena_filter_blocks(seq_len, w1, b1, w2, b2, w3, b3, freq, w4)
    nd = 2 * nblocks - 1
    k1f = _fft1(lhs_real, kext[None], lambda q, j: (0, q + 1), lambda q, j: (0, q), nd, 1, d_b,
                natural=False, zero_lanes=d_b)
    kspec = _fft2_filter(gfwd_hl, k1f.reshape(nd, 2, FFT_N1, FFT_N2, d_b))
    g5 = g3.reshape(bsz, nblocks, half, FFT_N2, d_b)
    x1 = _fft1(lhs_data, g5, lambda q, j: (2 * q, j), lambda q, j: (2 * q + 1, j), bsz // 2, nblocks, d_b,
               natural=True)
    wmix = _fft2_mix(gfwd_hl, ginv_hl, x1, kspec)
    y6 = _fft3(lhs_inv, wmix)
    return y6.reshape(bsz, seq_len, d_b)


def _postmix_kernel(x_ref, yan_ref, x0_ref, g_ref, yc_ref, bias_ref, mgb_ref, wo_ref, n2_ref, x1_ref, xn_ref):
    d_a = yan_ref.shape[1]
    gg = g_ref[...]
    yb = x0_ref[...] * (yc_ref[...] + gg * bias_ref[...])
    ybn = _rms(yb, mgb_ref[...]).astype(BF16)
    y = jnp.dot(yan_ref[...], wo_ref[0:d_a, :], preferred_element_type=F32)
    y = y + jnp.dot(ybn, wo_ref[d_a:, :], preferred_element_type=F32)
    x1 = x_ref[...] + y
    x1_ref[...] = x1
    xn_ref[...] = _rms(x1, n2_ref[...]).astype(xn_ref.dtype)


def _postmix(x, yan, x0, g, yconv, bias, mix_g_b, w_out, norm2, tm=512):
    t, d = x.shape
    d_a = yan.shape[1]
    d_b = x0.shape[1]
    assert t % tm == 0, t
    row = lambda w: pl.BlockSpec((tm, w), lambda i: (i, 0))
    const = lambda *shape: pl.BlockSpec(shape, lambda i: (0,) * len(shape), pipeline_mode=pl.Buffered(1))
    return pl.pallas_call(
        _postmix_kernel,
        grid=(t // tm,),
        in_specs=[row(d), row(d_a), row(d_b), row(d_b), row(d_b), const(1, d_b), const(1, d_b),
                  const(d_a + d_b, d), const(1, d)],
        out_specs=[row(d), row(d)],
        out_shape=[jax.ShapeDtypeStruct((t, d), F32), jax.ShapeDtypeStruct((t, d), BF16)],
        compiler_params=_cparams(("parallel",)),
        name="postmix",
    )(x, yan, x0, g, yconv, bias, mix_g_b, w_out, norm2)


def _staircase():
    return [(i, j) for i in range(PEER_TOPK) for j in range(PEER_TOPK) if (i + 1) * (j + 1) <= PEER_TOPK]


def _stream_argmax(val_ref, tag_ref, nrows, prev, slab):
    neg = np.float32(-np.inf)
    best_v = jnp.full(slab, neg, F32)
    best_i = jnp.zeros(slab, I32)
    best_t = jnp.zeros(slab, I32)
    for n in range(nrows):
        v = jnp.where(prev == n, neg, val_ref[n])
        val_ref[n] = v
        better = v > best_v
        best_v = jnp.maximum(v, best_v)
        best_i = jnp.where(better, n, best_i)
        if tag_ref is not None:
            best_t = jnp.where(better, tag_ref[n], best_t)
    return best_v, best_i, best_t


def _scores_kernel(xn_ref, wqt_ref, kbd_ref, s_ref, *, heads):
    tm = xn_ref.shape[0]
    hk = heads * HALF_KEY
    q_t = lax.dot_general(wqt_ref[...], xn_ref[...], (((1,), (1,)), ((), ())),
                          preferred_element_type=F32).astype(BF16)
    for half in range(2):
        per_head = [jnp.dot(kbd_ref[half, h], q_t[half * hk + h * HALF_KEY:half * hk + (h + 1) * HALF_KEY],
                            preferred_element_type=F32) for h in range(heads)]
        s_ref[half] = jnp.swapaxes(jnp.stack(per_head, axis=0), 0, 1)


def _peer_scores(xn, wqt, kbd, heads, tm=512):
    t, d = xn.shape
    assert t % tm == 0 and heads == V7X_SUBLANES, (t, heads)
    const = lambda *shape: pl.BlockSpec(shape, lambda i: (0,) * len(shape), pipeline_mode=pl.Buffered(1))
    kern = functools.partial(_scores_kernel, heads=heads)
    return pl.pallas_call(
        kern,
        grid=(t // tm,),
        in_specs=[pl.BlockSpec((tm, d), lambda i: (i, 0)), const(*wqt.shape), const(*kbd.shape)],
        out_specs=pl.BlockSpec((2, N_KEYS, heads, tm), lambda i: (0, 0, 0, i)),
        out_shape=jax.ShapeDtypeStruct((2, N_KEYS, heads, t), F32),
        compiler_params=_cparams(("parallel",)),
        name="peer_scores",
    )(xn, wqt, kbd)


SELECT_BLOCK = 128


def _peer_a_select_kernel(xn_ref, dlo_ref, dhi_ref, s_ref, a_ref, e_ref, gt_ref, s0_s, s1_s, c_s, ce_s, b_s, es_s,
                          *, heads):
    xn = xn_ref[...]
    a_lo = jnp.dot(xn, dlo_ref[...], preferred_element_type=F32)
    a_hi = jnp.dot(xn, dhi_ref[...], preferred_element_type=F32)
    a_ref[...] = pltpu.pack_elementwise([a_lo, a_hi], packed_dtype=BF16)

    slab = (heads, s_ref.shape[-1])
    no_pick = jnp.full(slab, -1, I32)
    half_s = (s0_s, s1_s)
    vals, idxs, prevs = ([], []), ([], []), [no_pick, no_pick]
    for half in range(2):
        half_s[half][...] = s_ref[half]
    for r in range(PEER_TOPK):
        for half in range(2):
            m, prevs[half], _ = _stream_argmax(half_s[half], None, N_KEYS, prevs[half], slab)
            vals[half].append(m)
            idxs[half].append(prevs[half])
    cands = _staircase()
    for p, (i, j) in enumerate(cands):
        c_s[p] = vals[0][i] + vals[1][j]
        ce_s[p] = idxs[0][i] * N_KEYS + idxs[1][j]
    prev = no_pick
    for r in range(PEER_TOPK):
        m, prev, e = _stream_argmax(c_s, ce_s, len(cands), prev, slab)
        es_s[r] = e
        b_s[r] = m
    best = b_s[...]
    ex = jnp.exp(best - jnp.max(best, axis=0, keepdims=True))
    gate = ex / jnp.sum(ex, axis=0, keepdims=True)
    gt_ref[...] = gate.reshape(PEER_TOPK * heads, slab[1]).T
    e_ref[...] = es_s[...].reshape(PEER_TOPK * heads, slab[1]).T


def _peer_a_select(xn, down_t, scores, heads, tm=1024, tn=1024):
    t, d = xn.shape
    n = down_t.shape[1]
    tb = SELECT_BLOCK
    nsel = PEER_TOPK * heads
    ncand = len(_staircase())
    ni = t // tm
    nj = n // (2 * tn)
    assert t % tm == 0 and n % (2 * tn) == 0 and heads == V7X_SUBLANES, (t, n, heads)
    assert nj * ni * tb == t, "one selection block per grid step must cover all tokens"
    blk = lambda j, i: j * ni + i
    kern = functools.partial(_peer_a_select_kernel, heads=heads)
    return pl.pallas_call(
        kern,
        grid=(nj, ni),
        in_specs=[pl.BlockSpec((tm, d), lambda j, i: (i, 0)),
                  pl.BlockSpec((d, tn), lambda j, i: (0, j)), pl.BlockSpec((d, tn), lambda j, i: (0, j + nj)),
                  pl.BlockSpec((2, N_KEYS, heads, tb), lambda j, i: (0, 0, 0, blk(j, i)))],
        out_specs=[pl.BlockSpec((tm, tn), lambda j, i: (i, j)),
                   pl.BlockSpec((tb, nsel), lambda j, i: (blk(j, i), 0)),
                   pl.BlockSpec((tb, nsel), lambda j, i: (blk(j, i), 0))],
        out_shape=[jax.ShapeDtypeStruct((t, n // 2), U32), jax.ShapeDtypeStruct((t, nsel), I32),
                   jax.ShapeDtypeStruct((t, nsel), F32)],
        scratch_shapes=[
            pltpu.VMEM((N_KEYS, heads, tb), F32),
            pltpu.VMEM((N_KEYS, heads, tb), F32),
            pltpu.VMEM((ncand, heads, tb), F32),
            pltpu.VMEM((ncand, heads, tb), I32),
            pltpu.VMEM((PEER_TOPK, heads, tb), F32),
            pltpu.VMEM((PEER_TOPK, heads, tb), I32),
        ],
        compiler_params=_cparams(("parallel", "parallel")),
        name="peer_a_select",
    )(xn, down_t, down_t, scores)


PEER_B_GROUP = 16


def _peer_b_pick(a_ref, e_ref, gt_ref, rows):
    e = e_ref[rows, :]
    ai = lax.shift_right_logical(e, KEY_SHIFT)
    bi = e & (N_KEYS - 1)
    a_word = ai & (N_KEYS // 2 - 1)
    accs = [jnp.zeros(e.shape, U32) for _ in range(4)]
    for a in range(N_KEYS // 2):
        blk = a_ref[rows, a * N_KEYS:(a + 1) * N_KEYS]
        accs[a % 4] = jnp.where(a_word == a, jnp.take_along_axis(blk, bi, axis=1), accs[a % 4])
    word = (accs[0] | accs[1]) | (accs[2] | accs[3])
    lo = pltpu.unpack_elementwise(word, index=0, packed_dtype=BF16, unpacked_dtype=F32)
    hi = pltpu.unpack_elementwise(word, index=1, packed_dtype=BF16, unpacked_dtype=F32)
    picked = jnp.where(ai >= N_KEYS // 2, hi, lo)
    return gt_ref[rows, :] * _gelu(picked), e


def _peer_b_scatter(m_ref, rows, w, e):
    sub = lax.broadcasted_iota(I32, (e.shape[0], N_KEYS, e.shape[1]), 1)
    ai = lax.shift_right_logical(e, KEY_SHIFT)
    bi = e & (N_KEYS - 1)
    w1t = jnp.where(sub == ai[:, None, :], w[:, None, :], 0.0).astype(BF16)
    e2t = jnp.where(sub == bi[:, None, :], 1.0, 0.0).astype(BF16)
    m3 = jnp.einsum("cas,cbs->cab", w1t, e2t, preferred_element_type=F32)
    mt = jnp.swapaxes(m3, 0, 1).astype(m_ref.dtype)
    for a in range(N_KEYS):
        m_ref[rows, a * N_KEYS:(a + 1) * N_KEYS] = mt[a]


def _peer_b_kernel(a_ref, e_ref, gt_ref, m_ref):
    grp = PEER_B_GROUP
    ngrp = a_ref.shape[0] // grp

    def rows_of(gi):
        return pl.ds(pl.multiple_of(gi * grp, grp), grp)

    def step(gi, carry):
        nxt = _peer_b_pick(a_ref, e_ref, gt_ref, rows_of(gi))
        _peer_b_scatter(m_ref, rows_of(gi - 1), *carry)
        return nxt

    last = lax.fori_loop(1, ngrp, step, _peer_b_pick(a_ref, e_ref, gt_ref, rows_of(0)))
    _peer_b_scatter(m_ref, rows_of(ngrp - 1), *last)


def _peer_b(a2, eidx, gate, tc=256):
    t = a2.shape[0]
    n = 2 * a2.shape[1]
    nsel = eidx.shape[1]
    assert t % tc == 0 and tc % PEER_B_GROUP == 0 and n == N_KEYS * N_KEYS and nsel == N_KEYS, (t, n, nsel)
    return pl.pallas_call(
        _peer_b_kernel,
        grid=(t // tc,),
        in_specs=[pl.BlockSpec((tc, n // 2), lambda i: (i, 0)), pl.BlockSpec((tc, nsel), lambda i: (i, 0)),
                  pl.BlockSpec((tc, nsel), lambda i: (i, 0))],
        out_specs=pl.BlockSpec((tc, n), lambda i: (i, 0)),
        out_shape=jax.ShapeDtypeStruct((t, n), BF16),
        compiler_params=_cparams(("parallel",)),
        name="peer_b",
    )(a2, eidx, gate)


def _peer_c_kernel(m_ref, up_ref, x1_ref, fn_ref, o_ref, acc_s):
    k = pl.program_id(1)

    @pl.when(k == 0)
    def _():
        acc_s[...] = jnp.zeros_like(acc_s)

    acc_s[...] += jnp.dot(m_ref[...], up_ref[...], preferred_element_type=F32)

    @pl.when(k == pl.num_programs(1) - 1)
    def _():
        o_ref[...] = _rms(x1_ref[...] + acc_s[...], fn_ref[...])


def _peer_c(m, up, x1, final_norm, tm=1024, tk=1024):
    t, n = m.shape
    d = up.shape[1]
    assert t % tm == 0 and n % tk == 0, (t, n)
    return pl.pallas_call(
        _peer_c_kernel,
        grid=(t // tm, n // tk),
        in_specs=[pl.BlockSpec((tm, tk), lambda i, k: (i, k)), pl.BlockSpec((tk, d), lambda i, k: (k, 0)),
                  pl.BlockSpec((tm, d), lambda i, k: (i, 0), pipeline_mode=pl.Buffered(1)),
                  pl.BlockSpec((1, d), lambda i, k: (0, 0))],
        out_specs=pl.BlockSpec((tm, d), lambda i, k: (i, 0)),
        out_shape=jax.ShapeDtypeStruct((t, d), F32),
        scratch_shapes=[pltpu.VMEM((tm, d), F32)],
        compiler_params=_cparams(("parallel", "arbitrary")),
        name="peer_c",
    )(m, up, x1, final_norm)


def _prep_weights(norm1, w_in, a_ln_g, a_ln_b, a_ws, a_bs, b_conv_w, b_conv_b, b_bias, mix_norm, w_out, norm2,
                  peer_wq, peer_k1, peer_k2, peer_down, peer_up, final_norm):
    d = w_in.shape[0]
    d_a = a_ln_g.shape[0]
    d_b = b_bias.shape[0]
    heads, n_keys, half_key = peer_k1.shape
    row = lambda v: v.reshape(1, -1).astype(F32)
    w = dict(
        norm1=row(norm1), norm2=row(norm2), final_norm=row(final_norm),
        w_u=w_in[:, :d_a].astype(BF16), w_v=w_in[:, d_a:2 * d_a].astype(BF16),
        w_b=w_in[:, 2 * d_a:].reshape(d, 3, d_b).transpose(1, 0, 2).astype(BF16),
        ln_g=row(a_ln_g), ln_b=row(a_ln_b),
        ws=a_ws.astype(BF16),
        bsb=jnp.broadcast_to(a_bs[:, :, None], a_bs.shape + (A_HEAD_DIM,)).astype(F32),
        conv_w=b_conv_w.reshape(3, 3, d_b).transpose(1, 0, 2).astype(F32),
        conv_b=b_conv_b.reshape(3, d_b).astype(F32),
        bias=row(b_bias), mix_g_a=row(mix_norm[:d_a]), mix_g_b=row(mix_norm[d_a:]),
        w_out=w_out.astype(BF16),
        down_t=peer_down.astype(BF16).T, up=peer_up.astype(BF16),
    )
    wq_t = peer_wq.T.reshape(heads, 2, half_key, d).transpose(1, 0, 2, 3).reshape(2 * heads * half_key, d)
    w["wq_t"] = wq_t.astype(BF16)
    w["kbd"] = jnp.stack([peer_k1, peer_k2]).astype(BF16)
    w["heads"] = heads
    return w


def _trunk(x3, w, hf):
    bsz, seq_len, d = x3.shape
    t = bsz * seq_len
    x = x3.reshape(t, d)
    yan, x0, g = _inproj(x, seq_len, w["norm1"], w["w_u"], w["w_v"], w["w_b"], w["ln_g"], w["ln_b"],
                         w["ws"], w["bsb"], w["conv_w"], w["conv_b"], w["mix_g_a"])
    d_b = x0.shape[1]
    yconv = _hyena_long_conv(g.reshape(bsz, seq_len, d_b), *hf).reshape(t, d_b)
    x1, xn = _postmix(x, yan, x0, g, yconv, w["bias"], w["mix_g_b"], w["w_out"], w["norm2"])
    scores = _peer_scores(xn, w["wq_t"], w["kbd"], w["heads"])
    a2, eidx, gate = _peer_a_select(xn, w["down_t"], scores, w["heads"])
    m = _peer_b(a2, eidx, gate)
    out = _peer_c(m, w["up"], x1, w["final_norm"])
    return out.reshape(bsz, seq_len, d)


def kernel(x_prompt, x_sample, norm1, w_in, a_ln_g, a_ln_b, a_ws, a_bs, b_conv_w, b_conv_b, hf_w1, hf_b1, hf_w2, hf_b2, hf_w3, hf_b3, hf_freq, hf_w4, b_bias, mix_norm, w_out, norm2, peer_wq, peer_k1, peer_k2, peer_down, peer_up, final_norm):
    assert norm1.shape[0] == 1, "single-layer trunk"
    w = _prep_weights(norm1[0], w_in[0], a_ln_g[0], a_ln_b[0], a_ws[0], a_bs[0], b_conv_w[0], b_conv_b[0],
                      b_bias[0], mix_norm[0], w_out[0], norm2[0], peer_wq[0], peer_k1[0], peer_k2[0],
                      peer_down[0], peer_up[0], final_norm)
    hf = (hf_w1[0], hf_b1[0].reshape(1, -1), hf_w2[0], hf_b2[0].reshape(1, -1), hf_w3[0], hf_b3[0].reshape(1, -1),
          hf_freq[0], hf_w4[0])
    return (_trunk(x_prompt, w, hf), _trunk(x_sample, w, hf))
```

```python
import functools
import math

import numpy as np
import jax
import jax.numpy as jnp
from jax import lax
from jax.experimental import pallas as pl
from jax.experimental.pallas import tpu as pltpu

F32 = jnp.float32
BF16 = jnp.bfloat16
I32 = jnp.int32
U32 = jnp.uint32

EPS = 1e-6
V7X_SUBLANES = 8
V7X_VMEM_BYTES = 64 * 1024 * 1024
VMEM_LIMIT = V7X_VMEM_BYTES - 8 * 1024 * 1024

CHUNK = 128
A_HEAD_DIM = 128
EMB_DIM = 33
BANDS = (EMB_DIM - 1) // 2
DECAY_TARGET = 1e-2
FAST_DECAY_PCT = 0.3
SLOW_DECAY_PCT = 1.5
FFT_N1 = 64
FFT_N2 = 128
FFT_N = FFT_N1 * FFT_N2
CONV_BLOCK = FFT_N // 2
N_KEYS = 128
KEY_SHIFT = 7
PEER_TOPK = 16
HALF_KEY = 128


def _cparams(sem):
    return pltpu.CompilerParams(dimension_semantics=sem, vmem_limit_bytes=VMEM_LIMIT)


def _rms(xf, g):
    return xf * lax.rsqrt(jnp.mean(xf * xf, axis=-1, keepdims=True) + EPS) * g


def _gelu(x):
    return 0.5 * x * (1.0 + lax.erf(x * np.float32(math.sqrt(0.5))))


def _inproj_kernel(xp_ref, x_ref, xn_ref, n1_ref, wu_ref, wv_ref, wb_ref, lng_ref, lnb_ref,
                   ws_ref, bsb_ref, cw_ref, cb_ref, mga_ref,
                   yan_ref, x0_ref, g_ref, ya_s, zs_s, *, tiles_per_seq, cblk):
    tm = x_ref.shape[0]
    d_a = wu_ref.shape[1]
    d_b = wb_ref.shape[2]
    i = pl.program_id(0)
    not_first = (i % tiles_per_seq != 0).astype(F32)
    not_last = (i % tiles_per_seq != tiles_per_seq - 1).astype(F32)

    xcat = jnp.concatenate([xp_ref[...], x_ref[...], xn_ref[...]], axis=0)
    hcat = _rms(xcat, n1_ref[...])
    h_all = hcat.astype(BF16)
    h = hcat[V7X_SUBLANES:V7X_SUBLANES + tm].astype(BF16)

    u = _gelu(jnp.dot(h, wu_ref[...], preferred_element_type=F32))
    v = _gelu(jnp.dot(h, wv_ref[...], preferred_element_type=F32))
    mu = jnp.mean(v, axis=-1, keepdims=True)
    vc = v - mu
    var = jnp.mean(vc * vc, axis=-1, keepdims=True)
    vb = (vc * lax.rsqrt(var + EPS) * lng_ref[...] + lnb_ref[...]).astype(BF16)
    chunks = [slice(c * CHUNK, (c + 1) * CHUNK) for c in range(tm // CHUNK)]
    for hd in range(d_a // A_HEAD_DIM):
        cols = slice(hd * A_HEAD_DIM, (hd + 1) * A_HEAD_DIM)
        v_head = jnp.concatenate([vb[rows, cols] for rows in chunks], axis=1)
        mixed = jnp.dot(ws_ref[hd], v_head, preferred_element_type=F32)
        for c, rows in enumerate(chunks):
            ya_s[rows, cols] = u[rows, cols] * (mixed[:, c * A_HEAD_DIM:(c + 1) * A_HEAD_DIM] + bsb_ref[hd])
    yan_ref[...] = _rms(ya_s[...], mga_ref[...]).astype(yan_ref.dtype)

    halo = V7X_SUBLANES
    for cb in range(d_b // cblk):
        cols = slice(cb * cblk, (cb + 1) * cblk)
        parts = []
        for p in range(3):
            z = jnp.dot(h_all, wb_ref[p, :, cols], preferred_element_type=F32)
            zs_s[...] = z
            zs_s[0:halo, :] = z[0:halo] * not_first
            zs_s[tm + halo:tm + 2 * halo, :] = z[tm + halo:tm + 2 * halo] * not_last
            w = cw_ref[p]
            zc = (cb_ref[p:p + 1, cols]
                  + zs_s[halo - 1:halo - 1 + tm, :] * w[0:1, cols]
                  + zs_s[halo:halo + tm, :] * w[1:2, cols]
                  + zs_s[halo + 1:halo + 1 + tm, :] * w[2:3, cols])
            parts.append(zc)
        x0_ref[:, cols] = parts[0]
        g_ref[:, cols] = parts[1] * parts[2]


def _inproj(x, seq_len, norm1, w_u, w_v, w_b, ln_g, ln_b, ws, bsb, conv_w, conv_b, mix_g_a, tm=512, cblk=256):
    t, d = x.shape
    d_a = w_u.shape[1]
    d_b = w_b.shape[2]
    assert seq_len % tm == 0 and t % seq_len == 0 and tm % CHUNK == 0 and d_b % cblk == 0, (t, seq_len)
    nblk = tm // V7X_SUBLANES
    last8 = t // V7X_SUBLANES - 1
    const = lambda *shape: pl.BlockSpec(shape, lambda i: (0,) * len(shape), pipeline_mode=pl.Buffered(1))
    kern = functools.partial(_inproj_kernel, tiles_per_seq=seq_len // tm, cblk=cblk)
    return pl.pallas_call(
        kern,
        grid=(t // tm,),
        in_specs=[
            pl.BlockSpec((V7X_SUBLANES, d), lambda i: (jnp.maximum(i * nblk - 1, 0), 0)),
            pl.BlockSpec((tm, d), lambda i: (i, 0)),
            pl.BlockSpec((V7X_SUBLANES, d), lambda i: (jnp.minimum((i + 1) * nblk, last8), 0)),
            const(1, d), const(d, d_a), const(d, d_a), const(3, d, d_b), const(1, d_a), const(1, d_a),
            const(*ws.shape), const(*bsb.shape), const(3, 3, d_b), const(3, d_b), const(1, d_a),
        ],
        out_specs=[
            pl.BlockSpec((tm, d_a), lambda i: (i, 0)),
            pl.BlockSpec((tm, d_b), lambda i: (i, 0)),
            pl.BlockSpec((tm, d_b), lambda i: (i, 0)),
        ],
        out_shape=[
            jax.ShapeDtypeStruct((t, d_a), BF16),
            jax.ShapeDtypeStruct((t, d_b), F32),
            jax.ShapeDtypeStruct((t, d_b), F32),
        ],
        scratch_shapes=[pltpu.VMEM((tm, d_a), F32), pltpu.VMEM((tm + 2 * V7X_SUBLANES, cblk), F32)],
        compiler_params=_cparams(("parallel",)),
        name="inproj",
    )(x, x, x, norm1, w_u, w_v, w_b, ln_g, ln_b, ws, bsb, conv_w, conv_b, mix_g_a)


def _filt_kernel(fr_ref, dl_ref, w1_ref, b1_ref, w2_ref, b2_ref, w3_ref, b3_ref, fq_ref, w4_ref, o_ref,
                 *, seq_len, nblocks, n2_per_step):
    half = FFT_N1 // 2
    tr = half * n2_per_step
    e = pl.program_id(0) - nblocks

    def lags(shape, axis):
        r = lax.broadcasted_iota(I32, shape, axis)
        n2 = pl.program_id(1) * n2_per_step + r // half
        return e * CONV_BLOCK + (r % half) * FFT_N2 + n2

    inv_len = np.float32(max(seq_len - 1, 1))
    pos = jnp.abs(lags((1, tr), 1)).astype(F32)
    t = pos / inv_len
    fw = (np.float32(2.0 * math.pi / seq_len) * pos) * fr_ref[...]
    hi = lax.Precision.HIGHEST
    w1 = w1_ref[...]
    z1 = (w1[:, 0:1] * t
          + jnp.dot(w1[:, 1:1 + BANDS], jnp.cos(fw), precision=hi, preferred_element_type=F32)
          + jnp.dot(w1[:, 1 + BANDS:], -jnp.sin(fw), precision=hi, preferred_element_type=F32))
    fq = fq_ref[...]
    h = jnp.sin(fq[:, 0:1] * (z1 + b1_ref[...]))
    h = jnp.sin(fq[:, 1:2] * (jnp.dot(w2_ref[...], h, precision=hi, preferred_element_type=F32) + b2_ref[...]))
    h = jnp.sin(fq[:, 2:3] * (jnp.dot(w3_ref[...], h, precision=hi, preferred_element_type=F32) + b3_ref[...]))
    h4 = jnp.dot(h.T, w4_ref[...], precision=hi, preferred_element_type=F32)
    lag_col = lags((tr, 1), 0)
    t_col = jnp.abs(lag_col).astype(F32) / inv_len
    window = jnp.exp(-t_col * dl_ref[...])
    res = jnp.where(jnp.abs(lag_col) <= seq_len - 1, h4 * window, 0.0)
    d_b = res.shape[1]
    for i in range(n2_per_step):
        o_ref[:, i * d_b:(i + 1) * d_b] = res[i * half:(i + 1) * half]


def _hyena_filter_blocks(seq_len, w1, b1, w2, b2, w3, b3, freq, w4, n2_per_step=16):
    nblocks = seq_len // CONV_BLOCK
    width = w1.shape[1]
    d_b = w4.shape[1] // 2
    fr = jnp.asarray(np.linspace(1e-4, BANDS - 1, BANDS, dtype=np.float32)[:, None])
    min_decay = math.log(DECAY_TARGET) / SLOW_DECAY_PCT
    max_decay = math.log(DECAY_TARGET) / FAST_DECAY_PCT
    deltas = jnp.asarray(np.abs(np.linspace(min_decay, max_decay, d_b, dtype=np.float32))[None, :])
    w4h = w4.reshape(width, 2, d_b).transpose(1, 0, 2)
    const = lambda *shape: pl.BlockSpec(shape, lambda a, b: (0,) * len(shape))
    half = FFT_N1 // 2
    kern = functools.partial(_filt_kernel, seq_len=seq_len, nblocks=nblocks, n2_per_step=n2_per_step)
    return pl.pallas_call(
        kern,
        grid=(2 * nblocks, FFT_N2 // n2_per_step),
        in_specs=[
            const(BANDS, 1), const(1, d_b), const(width, EMB_DIM), const(width, 1), const(width, width),
            const(width, 1), const(width, width), const(width, 1), const(width, 3),
            pl.BlockSpec((None, width, d_b), lambda ei, r: (jnp.where(ei >= nblocks, 0, 1), 0, 0)),
        ],
        out_specs=pl.BlockSpec((None, half, n2_per_step * d_b), lambda ei, r: (ei, 0, r)),
        out_shape=jax.ShapeDtypeStruct((2 * nblocks, half, FFT_N2 * d_b), F32),
        compiler_params=_cparams(("parallel", "parallel")),
        name="hyena_filter",
    )(fr, deltas, w1.T, b1.reshape(width, 1), w2.T, b2.reshape(width, 1), w3.T, b3.reshape(width, 1), freq.T, w4h)


def _dft_tables():
    n1 = np.arange(FFT_N1)
    f64 = np.exp(-2j * np.pi * np.outer(n1, n1) / FFT_N1)
    half = FFT_N1 // 2
    fr, fi = f64.real, f64.imag
    lhs_data = np.block([[fr[:, :half], -fi[:, :half]], [fi[:, :half], fr[:, :half]]])
    lhs_real = np.concatenate([fr, fi], axis=0)
    gr, gi = fr[:half, :] / FFT_N, -fi[:half, :] / FFT_N
    lhs_inv = np.block([[gr, -gi], [gi, gr]])
    k1 = np.arange(FFT_N1)[:, None, None]
    k2 = np.arange(FFT_N2)[None, :, None]
    n2 = np.arange(FFT_N2)[None, None, :]
    g = np.exp(-2j * np.pi * (n2 * (k1 + FFT_N1 * k2) % FFT_N) / FFT_N)
    gfwd = np.concatenate([np.concatenate([g.real, -g.imag], axis=2),
                           np.concatenate([g.imag, g.real], axis=2)], axis=1)
    ht = np.conj(np.transpose(g, (0, 2, 1)))
    ginv = np.concatenate([np.concatenate([ht.real, -ht.imag], axis=2),
                           np.concatenate([ht.imag, ht.real], axis=2)], axis=1)
    as32 = lambda a: np.asarray(a, dtype=np.float32)
    return as32(lhs_data), as32(lhs_real), as32(lhs_inv), as32(gfwd), as32(ginv)


def _split(x):
    hi = x.astype(BF16)
    lo = (x - hi.astype(F32)).astype(BF16)
    return hi, lo


def _dot3(a_hi, a_lo, b):
    if b.dtype == BF16:
        return jnp.dot(a_hi, b, preferred_element_type=F32) + jnp.dot(a_lo, b, preferred_element_type=F32)
    b_hi, b_lo = _split(b)
    acc = jnp.dot(a_hi, b_hi, preferred_element_type=F32)
    acc = acc + jnp.dot(a_hi, b_lo, preferred_element_type=F32)
    acc = acc + jnp.dot(a_lo, b_hi, preferred_element_type=F32)
    return acc


SPECTRA_DTYPE = BF16
FFT_TN2 = 16


def _rows_to_lanes(x):
    xt = jnp.swapaxes(x, 0, 1)
    return jnp.concatenate([xt[i] for i in range(xt.shape[0])], axis=1)


def _lanes_to_rows(x, s):
    c = x.shape[1] // s
    return jnp.swapaxes(jnp.stack([x[:, i * c:(i + 1) * c] for i in range(s)], axis=0), 0, 1)


def _fft1_kernel(lh_ref, ll_ref, a_ref, b_ref, o_ref, *, zero_lanes, natural):
    if natural:
        a = _rows_to_lanes(a_ref[...])
        b = _rows_to_lanes(b_ref[...])
    else:
        a = a_ref[...]
        b = b_ref[...]
    if zero_lanes:
        row = lax.broadcasted_iota(I32, b.shape, 0)
        lane = lax.broadcasted_iota(I32, b.shape, 1) + pl.program_id(2) * b.shape[1]
        b = jnp.where(jnp.logical_and(row == 0, lane < zero_lanes), 0.0, b)
    rhs = jnp.concatenate([a, b], axis=0).astype(SPECTRA_DTYPE)
    res = _dot3(lh_ref[...], ll_ref[...], rhs)
    o_ref[...] = _lanes_to_rows(res, FFT_TN2).reshape(o_ref.shape).astype(o_ref.dtype)


def _fft1(lhs, x, a_of, b_of, g, p, d_b, natural, zero_lanes=0):
    half = FFT_N1 // 2
    lh, ll = _split(jnp.asarray(lhs))
    kern = functools.partial(_fft1_kernel, zero_lanes=zero_lanes, natural=natural)
    if natural:
        blk = lambda of: pl.BlockSpec((None, None, half, FFT_TN2, d_b), lambda q, j, t: of(q, j) + (0, t, 0))
    else:
        blk = lambda of: pl.BlockSpec((None, None, half, FFT_TN2 * d_b), lambda q, j, t: of(q, j) + (0, t))
    return pl.pallas_call(
        kern,
        grid=(g, p, FFT_N2 // FFT_TN2),
        in_specs=[
            pl.BlockSpec(lhs.shape, lambda q, j, t: (0, 0)),
            pl.BlockSpec(lhs.shape, lambda q, j, t: (0, 0)),
            blk(a_of), blk(b_of),
        ],
        out_specs=pl.BlockSpec((None, None, 2, FFT_N1, FFT_TN2, d_b), lambda q, j, t: (q, j, 0, 0, t, 0)),
        out_shape=jax.ShapeDtypeStruct((g, p, 2, FFT_N1, FFT_N2, d_b), SPECTRA_DTYPE),
        compiler_params=_cparams(("parallel", "parallel", "parallel")),
        name="fft1",
    )(lh, ll, x, x)


def _fft3_kernel(lh_ref, ll_ref, w_ref, o_ref):
    half = FFT_N1 // 2
    c = w_ref.shape[-1]
    w = _rows_to_lanes(w_ref[...].astype(F32).reshape(2 * FFT_N1, FFT_TN2, c)).astype(w_ref.dtype)
    res = _dot3(lh_ref[...], ll_ref[...], w)
    o_ref[0] = _lanes_to_rows(res[:half], FFT_TN2)
    o_ref[1] = _lanes_to_rows(res[half:], FFT_TN2)


def _fft3(lhs, w6):
    g, p, _, _, _, d_b = w6.shape
    half = FFT_N1 // 2
    lh, ll = _split(jnp.asarray(lhs))
    return pl.pallas_call(
        _fft3_kernel,
        grid=(g, p, FFT_N2 // FFT_TN2),
        in_specs=[
            pl.BlockSpec(lhs.shape, lambda q, j, t: (0, 0)),
            pl.BlockSpec(lhs.shape, lambda q, j, t: (0, 0)),
            pl.BlockSpec((None, None, 2, FFT_N1, FFT_TN2, d_b), lambda q, j, t: (q, j, 0, 0, t, 0)),
        ],
        out_specs=pl.BlockSpec((None, 2, None, half, FFT_TN2, d_b), lambda q, j, t: (q, 0, j, 0, t, 0)),
        out_shape=jax.ShapeDtypeStruct((g, 2, p, half, FFT_N2, d_b), F32),
        compiler_params=_cparams(("parallel", "parallel", "parallel")),
        name="fft3",
    )(lh, ll, w6)


def _fft2_filter_kernel(gh_ref, gl_ref, x_ref, o_ref):
    ct = x_ref.shape[-1]
    for k in range(x_ref.shape[1]):
        z = _dot3(gh_ref[k], gl_ref[k], x_ref[:, k].reshape(2 * FFT_N2, ct))
        o_ref[:, k] = z.reshape(2, FFT_N2, ct).astype(o_ref.dtype)


def _fft2_filter(gfwd_hl, x1f, ct=256, kb=8):
    nd, _, _, _, d_b = x1f.shape
    gh, gl = gfwd_hl
    gspec = pl.BlockSpec((kb, 2 * FFT_N2, 2 * FFT_N2), lambda k, d, c: (k, 0, 0))
    xspec = pl.BlockSpec((None, 2, kb, FFT_N2, ct), lambda k, d, c: (d, 0, k, 0, c))
    return pl.pallas_call(
        _fft2_filter_kernel,
        grid=(FFT_N1 // kb, nd, d_b // ct),
        in_specs=[gspec, gspec, xspec],
        out_specs=xspec,
        out_shape=jax.ShapeDtypeStruct(x1f.shape, SPECTRA_DTYPE),
        compiler_params=_cparams(("parallel", "parallel", "parallel")),
        name="fft2_filter",
    )(gh, gl, x1f)


def _fft2_mix_kernel(gh_ref, gl_ref, ih_ref, il_ref, x_ref, k_ref, o_ref, *, nblocks):
    ct = x_ref.shape[-1]
    for k in range(x_ref.shape[2]):
        zs = []
        for j in range(nblocks):
            z = _dot3(gh_ref[k], gl_ref[k], x_ref[j, :, k].reshape(2 * FFT_N2, ct))
            zs.append((z[:FFT_N2], z[FFT_N2:]))
        for i in range(nblocks):
            yr = jnp.zeros((FFT_N2, ct), F32)
            yi = jnp.zeros((FFT_N2, ct), F32)
            for j in range(nblocks):
                d = i - j + nblocks - 1
                kr, ki = k_ref[d, 0, k].astype(F32), k_ref[d, 1, k].astype(F32)
                zr, zi = zs[j]
                yr = yr + kr * zr - ki * zi
                yi = yi + kr * zi + ki * zr
            w = _dot3(ih_ref[k], il_ref[k], jnp.concatenate([yr, yi], axis=0).astype(SPECTRA_DTYPE))
            o_ref[i, :, k] = w.reshape(2, FFT_N2, ct).astype(o_ref.dtype)


def _fft2_mix(gfwd_hl, ginv_hl, x1, kspec, ct=256):
    g, p, _, _, _, d_b = x1.shape
    nd = kspec.shape[0]
    kb = max(1, 8 // p)
    gspec = pl.BlockSpec((kb, 2 * FFT_N2, 2 * FFT_N2), lambda k, q, c: (k, 0, 0))
    xspec = pl.BlockSpec((None, p, 2, kb, FFT_N2, ct), lambda k, q, c: (q, 0, 0, k, 0, c))
    kern = functools.partial(_fft2_mix_kernel, nblocks=p)
    return pl.pallas_call(
        kern,
        grid=(FFT_N1 // kb, g, d_b // ct),
        in_specs=[gspec, gspec, gspec, gspec, xspec,
                  pl.BlockSpec((nd, 2, kb, FFT_N2, ct), lambda k, q, c: (0, 0, k, 0, c))],
        out_specs=xspec,
        out_shape=jax.ShapeDtypeStruct(x1.shape, SPECTRA_DTYPE),
        compiler_params=_cparams(("parallel", "parallel", "parallel")),
        name="fft2_mix",
    )(gfwd_hl[0], gfwd_hl[1], ginv_hl[0], ginv_hl[1], x1, kspec)


def _hyena_long_conv(g3, w1, b1, w2, b2, w3, b3, freq, w4):
    bsz, seq_len, d_b = g3.shape
    nblocks = seq_len // CONV_BLOCK
    half = FFT_N1 // 2
    lhs_data, lhs_real, lhs_inv, gfwd, ginv = _dft_tables()
    gfwd_hl = _split(jnp.asarray(gfwd))
    ginv_hl = _split(jnp.asarray(ginv))
    kext = _hyena_filter_blocks(seq_len, w1, b1, w2, b2, w3, b3, freq, w4)
    nd = 2 * nblocks - 1
    k1f = _fft1(lhs_real, kext[None], lambda q, j: (0, q + 1), lambda q, j: (0, q), nd, 1, d_b,
                natural=False, zero_lanes=d_b)
    kspec = _fft2_filter(gfwd_hl, k1f.reshape(nd, 2, FFT_N1, FFT_N2, d_b))
    g5 = g3.reshape(bsz, nblocks, half, FFT_N2, d_b)
    x1 = _fft1(lhs_data, g5, lambda q, j: (2 * q, j), lambda q, j: (2 * q + 1, j), bsz // 2, nblocks, d_b,
               natural=True)
    wmix = _fft2_mix(gfwd_hl, ginv_hl, x1, kspec)
    y6 = _fft3(lhs_inv, wmix)
    return y6.reshape(bsz, seq_len, d_b)


def _postmix_kernel(x_ref, yan_ref, x0_ref, g_ref, yc_ref, bias_ref, mgb_ref, wo_ref, n2_ref, x1_ref, xn_ref):
    d_a = yan_ref.shape[1]
    gg = g_ref[...]
    yb = x0_ref[...] * (yc_ref[...] + gg * bias_ref[...])
    ybn = _rms(yb, mgb_ref[...]).astype(BF16)
    y = jnp.dot(yan_ref[...], wo_ref[0:d_a, :], preferred_element_type=F32)
    y = y + jnp.dot(ybn, wo_ref[d_a:, :], preferred_element_type=F32)
    x1 = x_ref[...] + y
    x1_ref[...] = x1
    xn_ref[...] = _rms(x1, n2_ref[...]).astype(xn_ref.dtype)


def _postmix(x, yan, x0, g, yconv, bias, mix_g_b, w_out, norm2, tm=512):
    t, d = x.shape
    d_a = yan.shape[1]
    d_b = x0.shape[1]
    assert t % tm == 0, t
    row = lambda w: pl.BlockSpec((tm, w), lambda i: (i, 0))
    const = lambda *shape: pl.BlockSpec(shape, lambda i: (0,) * len(shape), pipeline_mode=pl.Buffered(1))
    return pl.pallas_call(
        _postmix_kernel,
        grid=(t // tm,),
        in_specs=[row(d), row(d_a), row(d_b), row(d_b), row(d_b), const(1, d_b), const(1, d_b),
                  const(d_a + d_b, d), const(1, d)],
        out_specs=[row(d), row(d)],
        out_shape=[jax.ShapeDtypeStruct((t, d), F32), jax.ShapeDtypeStruct((t, d), BF16)],
        compiler_params=_cparams(("parallel",)),
        name="postmix",
    )(x, yan, x0, g, yconv, bias, mix_g_b, w_out, norm2)


def _staircase():
    return [(i, j) for i in range(PEER_TOPK) for j in range(PEER_TOPK) if (i + 1) * (j + 1) <= PEER_TOPK]


def _stream_argmax(val_ref, tag_ref, nrows, prev, slab):
    neg = np.float32(-np.inf)
    best_v = jnp.full(slab, neg, F32)
    best_i = jnp.zeros(slab, I32)
    best_t = jnp.zeros(slab, I32)
    for n in range(nrows):
        v = jnp.where(prev == n, neg, val_ref[n])
        val_ref[n] = v
        better = v > best_v
        best_v = jnp.maximum(v, best_v)
        best_i = jnp.where(better, n, best_i)
        if tag_ref is not None:
            best_t = jnp.where(better, tag_ref[n], best_t)
    return best_v, best_i, best_t


def _scores_kernel(xn_ref, wqt_ref, kbd_ref, s_ref, *, heads):
    tm = xn_ref.shape[0]
    hk = heads * HALF_KEY
    q_t = lax.dot_general(wqt_ref[...], xn_ref[...], (((1,), (1,)), ((), ())),
                          preferred_element_type=F32).astype(BF16)
    for half in range(2):
        per_head = [jnp.dot(kbd_ref[half, h], q_t[half * hk + h * HALF_KEY:half * hk + (h + 1) * HALF_KEY],
                            preferred_element_type=F32) for h in range(heads)]
        s_ref[half] = jnp.swapaxes(jnp.stack(per_head, axis=0), 0, 1)


def _peer_scores(xn, wqt, kbd, heads, tm=512):
    t, d = xn.shape
    assert t % tm == 0 and heads == V7X_SUBLANES, (t, heads)
    const = lambda *shape: pl.BlockSpec(shape, lambda i: (0,) * len(shape), pipeline_mode=pl.Buffered(1))
    kern = functools.partial(_scores_kernel, heads=heads)
    return pl.pallas_call(
        kern,
        grid=(t // tm,),
        in_specs=[pl.BlockSpec((tm, d), lambda i: (i, 0)), const(*wqt.shape), const(*kbd.shape)],
        out_specs=pl.BlockSpec((2, N_KEYS, heads, tm), lambda i: (0, 0, 0, i)),
        out_shape=jax.ShapeDtypeStruct((2, N_KEYS, heads, t), F32),
        compiler_params=_cparams(("parallel",)),
        name="peer_scores",
    )(xn, wqt, kbd)


SELECT_BLOCK = 128


def _peer_a_select_kernel(xn_ref, dlo_ref, dhi_ref, s_ref, a_ref, e_ref, gt_ref, s0_s, s1_s, c_s, ce_s, b_s, es_s,
                          *, heads):
    xn = xn_ref[...]
    a_lo = jnp.dot(xn, dlo_ref[...], preferred_element_type=F32)
    a_hi = jnp.dot(xn, dhi_ref[...], preferred_element_type=F32)
    a_ref[...] = pltpu.pack_elementwise([a_lo, a_hi], packed_dtype=BF16)

    slab = (heads, s_ref.shape[-1])
    no_pick = jnp.full(slab, -1, I32)
    half_s = (s0_s, s1_s)
    vals, idxs, prevs = ([], []), ([], []), [no_pick, no_pick]
    for half in range(2):
        half_s[half][...] = s_ref[half]
    for r in range(PEER_TOPK):
        for half in range(2):
            m, prevs[half], _ = _stream_argmax(half_s[half], None, N_KEYS, prevs[half], slab)
            vals[half].append(m)
            idxs[half].append(prevs[half])
    cands = _staircase()
    for p, (i, j) in enumerate(cands):
        c_s[p] = vals[0][i] + vals[1][j]
        ce_s[p] = idxs[0][i] * N_KEYS + idxs[1][j]
    prev = no_pick
    for r in range(PEER_TOPK):
        m, prev, e = _stream_argmax(c_s, ce_s, len(cands), prev, slab)
        es_s[r] = e
        b_s[r] = m
    best = b_s[...]
    ex = jnp.exp(best - jnp.max(best, axis=0, keepdims=True))
    gate = ex / jnp.sum(ex, axis=0, keepdims=True)
    gt_ref[...] = gate.reshape(PEER_TOPK * heads, slab[1]).T
    e_ref[...] = es_s[...].reshape(PEER_TOPK * heads, slab[1]).T


def _peer_a_select(xn, down_t, scores, heads, tm=1024, tn=1024):
    t, d = xn.shape
    n = down_t.shape[1]
    tb = SELECT_BLOCK
    nsel = PEER_TOPK * heads
    ncand = len(_staircase())
    ni = t // tm
    nj = n // (2 * tn)
    assert t % tm == 0 and n % (2 * tn) == 0 and heads == V7X_SUBLANES, (t, n, heads)
    assert nj * ni * tb == t, "one selection block per grid step must cover all tokens"
    blk = lambda j, i: j * ni + i
    kern = functools.partial(_peer_a_select_kernel, heads=heads)
    return pl.pallas_call(
        kern,
        grid=(nj, ni),
        in_specs=[pl.BlockSpec((tm, d), lambda j, i: (i, 0)),
                  pl.BlockSpec((d, tn), lambda j, i: (0, j)), pl.BlockSpec((d, tn), lambda j, i: (0, j + nj)),
                  pl.BlockSpec((2, N_KEYS, heads, tb), lambda j, i: (0, 0, 0, blk(j, i)))],
        out_specs=[pl.BlockSpec((tm, tn), lambda j, i: (i, j)),
                   pl.BlockSpec((tb, nsel), lambda j, i: (blk(j, i), 0)),
                   pl.BlockSpec((tb, nsel), lambda j, i: (blk(j, i), 0))],
        out_shape=[jax.ShapeDtypeStruct((t, n // 2), U32), jax.ShapeDtypeStruct((t, nsel), I32),
                   jax.ShapeDtypeStruct((t, nsel), F32)],
        scratch_shapes=[
            pltpu.VMEM((N_KEYS, heads, tb), F32),
            pltpu.VMEM((N_KEYS, heads, tb), F32),
            pltpu.VMEM((ncand, heads, tb), F32),
            pltpu.VMEM((ncand, heads, tb), I32),
            pltpu.VMEM((PEER_TOPK, heads, tb), F32),
            pltpu.VMEM((PEER_TOPK, heads, tb), I32),
        ],
        compiler_params=_cparams(("parallel", "parallel")),
        name="peer_a_select",
    )(xn, down_t, down_t, scores)


PEER_B_GROUP = 16


def _peer_b_pick(a_ref, e_ref, gt_ref, rows):
    e = e_ref[rows, :]
    ai = lax.shift_right_logical(e, KEY_SHIFT)
    bi = e & (N_KEYS - 1)
    a_word = ai & (N_KEYS // 2 - 1)
    accs = [jnp.zeros(e.shape, U32) for _ in range(4)]
    for a in range(N_KEYS // 2):
        blk = a_ref[rows, a * N_KEYS:(a + 1) * N_KEYS]
        accs[a % 4] = jnp.where(a_word == a, jnp.take_along_axis(blk, bi, axis=1), accs[a % 4])
    word = (accs[0] | accs[1]) | (accs[2] | accs[3])
    lo = pltpu.unpack_elementwise(word, index=0, packed_dtype=BF16, unpacked_dtype=F32)
    hi = pltpu.unpack_elementwise(word, index=1, packed_dtype=BF16, unpacked_dtype=F32)
    picked = jnp.where(ai >= N_KEYS // 2, hi, lo)
    return gt_ref[rows, :] * _gelu(picked), e


def _peer_b_scatter(m_ref, rows, w, e):
    sub = lax.broadcasted_iota(I32, (e.shape[0], N_KEYS, e.shape[1]), 1)
    ai = lax.shift_right_logical(e, KEY_SHIFT)
    bi = e & (N_KEYS - 1)
    w1t = jnp.where(sub == ai[:, None, :], w[:, None, :], 0.0).astype(BF16)
    e2t = jnp.where(sub == bi[:, None, :], 1.0, 0.0).astype(BF16)
    m3 = jnp.einsum("cas,cbs->cab", w1t, e2t, preferred_element_type=F32)
    mt = jnp.swapaxes(m3, 0, 1).astype(m_ref.dtype)
    for a in range(N_KEYS):
        m_ref[rows, a * N_KEYS:(a + 1) * N_KEYS] = mt[a]


def _peer_b_kernel(a_ref, e_ref, gt_ref, m_ref):
    grp = PEER_B_GROUP
    ngrp = a_ref.shape[0] // grp

    def rows_of(gi):
        return pl.ds(pl.multiple_of(gi * grp, grp), grp)

    def step(gi, carry):
        nxt = _peer_b_pick(a_ref, e_ref, gt_ref, rows_of(gi))
        _peer_b_scatter(m_ref, rows_of(gi - 1), *carry)
        return nxt

    last = lax.fori_loop(1, ngrp, step, _peer_b_pick(a_ref, e_ref, gt_ref, rows_of(0)))
    _peer_b_scatter(m_ref, rows_of(ngrp - 1), *last)


def _peer_b(a2, eidx, gate, tc=256):
    t = a2.shape[0]
    n = 2 * a2.shape[1]
    nsel = eidx.shape[1]
    assert t % tc == 0 and tc % PEER_B_GROUP == 0 and n == N_KEYS * N_KEYS and nsel == N_KEYS, (t, n, nsel)
    return pl.pallas_call(
        _peer_b_kernel,
        grid=(t // tc,),
        in_specs=[pl.BlockSpec((tc, n // 2), lambda i: (i, 0)), pl.BlockSpec((tc, nsel), lambda i: (i, 0)),
                  pl.BlockSpec((tc, nsel), lambda i: (i, 0))],
        out_specs=pl.BlockSpec((tc, n), lambda i: (i, 0)),
        out_shape=jax.ShapeDtypeStruct((t, n), BF16),
        compiler_params=_cparams(("parallel",)),
        name="peer_b",
    )(a2, eidx, gate)


def _peer_c_kernel(m_ref, up_ref, x1_ref, fn_ref, o_ref, acc_s):
    k = pl.program_id(1)

    @pl.when(k == 0)
    def _():
        acc_s[...] = jnp.zeros_like(acc_s)

    acc_s[...] += jnp.dot(m_ref[...], up_ref[...], preferred_element_type=F32)

    @pl.when(k == pl.num_programs(1) - 1)
    def _():
        o_ref[...] = _rms(x1_ref[...] + acc_s[...], fn_ref[...])


def _peer_c(m, up, x1, final_norm, tm=512, tk=2048):
    t, n = m.shape
    d = up.shape[1]
    assert t % tm == 0 and n % tk == 0, (t, n)
    return pl.pallas_call(
        _peer_c_kernel,
        grid=(t // tm, n // tk),
        in_specs=[pl.BlockSpec((tm, tk), lambda i, k: (i, k)), pl.BlockSpec((tk, d), lambda i, k: (k, 0)),
                  pl.BlockSpec((tm, d), lambda i, k: (i, 0), pipeline_mode=pl.Buffered(1)),
                  pl.BlockSpec((1, d), lambda i, k: (0, 0))],
        out_specs=pl.BlockSpec((tm, d), lambda i, k: (i, 0)),
        out_shape=jax.ShapeDtypeStruct((t, d), F32),
        scratch_shapes=[pltpu.VMEM((tm, d), F32)],
        compiler_params=_cparams(("parallel", "arbitrary")),
        name="peer_c",
    )(m, up, x1, final_norm)


def _prep_weights(norm1, w_in, a_ln_g, a_ln_b, a_ws, a_bs, b_conv_w, b_conv_b, b_bias, mix_norm, w_out, norm2,
                  peer_wq, peer_k1, peer_k2, peer_down, peer_up, final_norm):
    d = w_in.shape[0]
    d_a = a_ln_g.shape[0]
    d_b = b_bias.shape[0]
    heads, n_keys, half_key = peer_k1.shape
    row = lambda v: v.reshape(1, -1).astype(F32)
    w = dict(
        norm1=row(norm1), norm2=row(norm2), final_norm=row(final_norm),
        w_u=w_in[:, :d_a].astype(BF16), w_v=w_in[:, d_a:2 * d_a].astype(BF16),
        w_b=w_in[:, 2 * d_a:].reshape(d, 3, d_b).transpose(1, 0, 2).astype(BF16),
        ln_g=row(a_ln_g), ln_b=row(a_ln_b),
        ws=a_ws.astype(BF16),
        bsb=jnp.broadcast_to(a_bs[:, :, None], a_bs.shape + (A_HEAD_DIM,)).astype(F32),
        conv_w=b_conv_w.reshape(3, 3, d_b).transpose(1, 0, 2).astype(F32),
        conv_b=b_conv_b.reshape(3, d_b).astype(F32),
        bias=row(b_bias), mix_g_a=row(mix_norm[:d_a]), mix_g_b=row(mix_norm[d_a:]),
        w_out=w_out.astype(BF16),
        down_t=peer_down.astype(BF16).T, up=peer_up.astype(BF16),
    )
    wq_t = peer_wq.T.reshape(heads, 2, half_key, d).transpose(1, 0, 2, 3).reshape(2 * heads * half_key, d)
    w["wq_t"] = wq_t.astype(BF16)
    w["kbd"] = jnp.stack([peer_k1, peer_k2]).astype(BF16)
    w["heads"] = heads
    return w


def _trunk(x3, w, hf):
    bsz, seq_len, d = x3.shape
    t = bsz * seq_len
    x = x3.reshape(t, d)
    yan, x0, g = _inproj(x, seq_len, w["norm1"], w["w_u"], w["w_v"], w["w_b"], w["ln_g"], w["ln_b"],
                         w["ws"], w["bsb"], w["conv_w"], w["conv_b"], w["mix_g_a"])
    d_b = x0.shape[1]
    yconv = _hyena_long_conv(g.reshape(bsz, seq_len, d_b), *hf).reshape(t, d_b)
    x1, xn = _postmix(x, yan, x0, g, yconv, w["bias"], w["mix_g_b"], w["w_out"], w["norm2"])
    scores = _peer_scores(xn, w["wq_t"], w["kbd"], w["heads"])
    a2, eidx, gate = _peer_a_select(xn, w["down_t"], scores, w["heads"])
    m = _peer_b(a2, eidx, gate)
    out = _peer_c(m, w["up"], x1, w["final_norm"])
    return out.reshape(bsz, seq_len, d)


def kernel(x_prompt, x_sample, norm1, w_in, a_ln_g, a_ln_b, a_ws, a_bs, b_conv_w, b_conv_b, hf_w1, hf_b1, hf_w2, hf_b2, hf_w3, hf_b3, hf_freq, hf_w4, b_bias, mix_norm, w_out, norm2, peer_wq, peer_k1, peer_k2, peer_down, peer_up, final_norm):
    assert norm1.shape[0] == 1, "single-layer trunk"
    w = _prep_weights(norm1[0], w_in[0], a_ln_g[0], a_ln_b[0], a_ws[0], a_bs[0], b_conv_w[0], b_conv_b[0],
                      b_bias[0], mix_norm[0], w_out[0], norm2[0], peer_wq[0], peer_k1[0], peer_k2[0],
                      peer_down[0], peer_up[0], final_norm)
    hf = (hf_w1[0], hf_b1[0].reshape(1, -1), hf_w2[0], hf_b2[0].reshape(1, -1), hf_w3[0], hf_b3[0].reshape(1, -1),
          hf_freq[0], hf_w4[0])
    return (_trunk(x_prompt, w, hf), _trunk(x_sample, w, hf))
```
